```python
import math
import jax
import jax.numpy as jnp
from jax import lax
import numpy as np

D_MODEL = 1024
BATCH = 2
SEQ = 8192
DEPTH = 4

GRID_W = 64
CTX_LEN = 256
CHUNK = 64
EPS = 1e-6

CONV_DIM = 512
CONV_WIDTH = 31
DN_HEADS = 4
DN_HEAD_DIM = 128
DN_DIM = DN_HEADS * DN_HEAD_DIM
DN_CONV_WIDTH = 5
FN_GROUPS = 4
FN_GROUP_DIM = 128
FN_DIM = FN_GROUPS * FN_GROUP_DIM
GLA_HEADS = 4
GLA_DK = 64
GLA_DV = 128
GLA_KDIM = GLA_HEADS * GLA_DK
GLA_VDIM = GLA_HEADS * GLA_DV
GLA_GATE_RANK = 16
GLA_TAU = 16.0
N_BRANCH = 4
BRANCH_DIM = 512

DN_COLS = 4 * DN_DIM + 4 * DN_HEADS
GLA_COLS = 2 * GLA_KDIM + 2 * GLA_VDIM + 2 * GLA_GATE_RANK
CONV_COLS = 2 * CONV_DIM
GATE_COLS = N_BRANCH * D_MODEL
IN_SPLITS = (DN_COLS, GLA_COLS, CONV_COLS, FN_DIM, GATE_COLS)
IN_DIM = sum(IN_SPLITS)
REC_COLS = DN_COLS + GLA_COLS

N_EXPERTS = 64
N_EXPERT_GROUPS = 8
TOP_GROUPS = 4
TOP_K = 6
EXPERT_DIM = 256
ROUTED_SCALE = 2.5

DEEPNORM_ALPHA = (2 * DEPTH) ** 0.25
DEEPNORM_BETA = (8 * DEPTH) ** -0.25

kernel_name = 'hybrid_gated_branch_moe_diffusion_trunk'


def layer_norm(t):
    tf = t.astype(jnp.float32)
    mu = jnp.mean(tf, axis=-1, keepdims=True)
    var = jnp.mean(jnp.square(tf - mu), axis=-1, keepdims=True)
    return ((tf - mu) * lax.rsqrt(var + EPS)).astype(t.dtype)


def post_norm(t, g, b):
    return layer_norm(t) * g + b


def modulate(t, shift, scale):
    return t * (1 + scale) + shift


def l2_normalize(t):
    tf = t.astype(jnp.float32)
    return tf * lax.rsqrt(jnp.sum(tf * tf, axis=-1, keepdims=True) + EPS)


def split_cols(t, sizes):
    return jnp.split(t, [int(s) for s in np.cumsum(sizes)[:-1]], axis=-1)


def split_heads(t, n):
    b, l, _ = t.shape
    return t.reshape(b, l, n, -1).transpose(0, 2, 1, 3)


def merge_heads(t):
    b, n, l, d = t.shape
    return t.transpose(0, 2, 1, 3).reshape(b, l, n * d)


def flip_seq(t):
    return jnp.flip(t, axis=2)


def to_col_major(t, rows):
    b, l, ch = t.shape
    return t.reshape(b, rows, GRID_W, ch).transpose(0, 2, 1, 3).reshape(b, l, ch)


def from_col_major(t, rows):
    b, l, ch = t.shape
    return t.reshape(b, GRID_W, rows, ch).transpose(0, 2, 1, 3).reshape(b, l, ch)


def depthwise_conv(t, w):
    k = w.shape[0]
    pad = (k - 1) // 2
    return lax.conv_general_dilated(t, w[:, None, :].astype(t.dtype), window_strides=(1,), padding=((pad, k - 1 - pad),), dimension_numbers=('NWC', 'WIO', 'NWC'), feature_group_count=t.shape[-1])


def gated_head_norm(o, z, g):
    o = o * lax.rsqrt(jnp.mean(o * o, axis=-1, keepdims=True) + EPS) * g.astype(jnp.float32)
    return (merge_heads(o) * jax.nn.silu(z.astype(jnp.float32))).astype(z.dtype)


def unit_lower_inverse(m):
    eye = jnp.eye(m.shape[-1], dtype=m.dtype)
    inv = eye - m
    p = m
    for _ in range(int(math.log2(CHUNK)) - 1):
        p = p @ p
        inv = inv @ (eye + p)
    return inv


def gated_delta_rule(q, k, v, g, beta, s0):
    b, h, l, dk = q.shape
    n = l // CHUNK
    chunks = lambda t: jnp.moveaxis(t.astype(jnp.float32).reshape(b, h, n, CHUNK, -1), 2, 0)
    qs = chunks(q) * dk ** -0.5
    ks, vs = chunks(k), chunks(v)
    gs = jnp.cumsum(chunks(g[..., None])[..., 0], axis=-1)
    bt = chunks(beta[..., None])
    incl = jnp.tril(jnp.ones((CHUNK, CHUNK), bool))
    strict = jnp.tril(jnp.ones((CHUNK, CHUNK), bool), -1)
    decay = jnp.exp(jnp.where(incl, gs[..., :, None] - gs[..., None, :], -jnp.inf))
    kb = ks * bt
    m = jnp.where(strict, jnp.einsum('nbhik,nbhjk->nbhij', kb, ks) * decay, 0.0)
    t_inv = unit_lower_inverse(m)
    u = t_inv @ (vs * bt)
    w = t_inv @ (kb * jnp.exp(gs)[..., None])
    a = jnp.einsum('nbhik,nbhjk->nbhij', qs, ks) * decay

    def step(s, xs):
        q_n, k_n, u_n, w_n, g_n, a_n = xs
        v_new = u_n - jnp.einsum('bhck,bhkv->bhcv', w_n, s)
        o = jnp.einsum('bhck,bhkv->bhcv', q_n * jnp.exp(g_n)[..., None], s) + jnp.einsum('bhij,bhjv->bhiv', a_n, v_new)
        g_last = g_n[..., -1:]
        s = s * jnp.exp(g_last)[..., None] + jnp.einsum('bhck,bhcv->bhkv', k_n * jnp.exp(g_last - g_n)[..., None], v_new)
        return s, o

    s_fin, o = lax.scan(step, s0.astype(jnp.float32), (qs, ks, u, w, gs, a))
    return jnp.moveaxis(o, 0, 2).reshape(b, h, l, -1), s_fin


def gla_chunked(q, k, v, log_a, s0):
    b, h, l, dk = q.shape
    n = l // CHUNK
    chunks = lambda t: jnp.moveaxis(t.astype(jnp.float32).reshape(b, h, n, CHUNK, -1), 2, 0)
    qs = chunks(q) * dk ** -0.5
    ks, vs = chunks(k), chunks(v)
    bs = jnp.cumsum(chunks(log_a), axis=3)
    incl = jnp.tril(jnp.ones((CHUNK, CHUNK), bool))[:, :, None]

    def step(s, xs):
        q_n, k_n, v_n, b_n = xs
        o_inter = jnp.einsum('bhik,bhkv->bhiv', q_n * jnp.exp(b_n), s)
        dec = jnp.exp(jnp.where(incl, b_n[:, :, :, None, :] - b_n[:, :, None, :, :], -jnp.inf))
        a = jnp.einsum('bhik,bhjk,bhijk->bhij', q_n, k_n, dec)
        b_last = b_n[:, :, -1:, :]
        s = s * jnp.exp(b_last[:, :, 0, :, None]) + jnp.einsum('bhjk,bhjv->bhkv', k_n * jnp.exp(b_last - b_n), v_n)
        return s, o_inter + jnp.einsum('bhij,bhjv->bhiv', a, v_n)

    s_fin, o = lax.scan(step, s0.astype(jnp.float32), (qs, ks, vs, bs))
    return jnp.moveaxis(o, 0, 2).reshape(b, h, l, -1), s_fin


def conformer_conv(p, conv_w, conv_b, ln_g, ln_b):
    val, gate = jnp.split(p, 2, axis=-1)
    y = depthwise_conv(val * jax.nn.sigmoid(gate), conv_w) + conv_b
    return jax.nn.silu(layer_norm(y) * ln_g + ln_b)


def fourier_mix(p):
    b, l, _ = p.shape
    y = p.astype(jnp.float32).reshape(b, l, FN_GROUPS, FN_GROUP_DIM)
    y = jnp.fft.fft2(y, axes=(1, 3), norm='ortho').real
    return y.reshape(b, l, FN_DIM).astype(p.dtype)


def _deltanet_inputs(p, conv_w, a_log, dt_bias):
    b, l, _ = p.shape
    qkv = jax.nn.silu(depthwise_conv(p[..., :3 * DN_DIM], conv_w))
    q, k, v = jnp.split(qkv, 3, axis=-1)
    z = p[..., 3 * DN_DIM:4 * DN_DIM]
    bg = p[..., 4 * DN_DIM:].astype(jnp.float32)
    beta = jax.nn.sigmoid(bg[..., :2 * DN_HEADS])
    g = -jnp.exp(a_log.astype(jnp.float32).reshape(-1)) * jax.nn.softplus(bg[..., 2 * DN_HEADS:] + dt_bias.astype(jnp.float32).reshape(-1))
    to_dir = lambda t: t.reshape(b, l, 2, DN_HEADS).transpose(2, 0, 3, 1)
    return (l2_normalize(split_heads(q, DN_HEADS)), l2_normalize(split_heads(k, DN_HEADS)), split_heads(v, DN_HEADS), z, to_dir(beta), to_dir(g))


def deltanet_branch(p_x, p_c, ctx_out, conv_w, a_log, dt_bias, norm_g):
    qx, kx, vx, zx, bx, gx = _deltanet_inputs(p_x, conv_w, a_log, dt_bias)
    qc, kc, vc, zc, bc, gc = _deltanet_inputs(p_c, conv_w, a_log, dt_bias)
    s0 = jnp.zeros((p_x.shape[0], DN_HEADS, DN_HEAD_DIM, DN_HEAD_DIM), jnp.float32)
    oc_f, sc_f = gated_delta_rule(qc, kc, vc, gc[0], bc[0], s0)
    oc_b, sc_b = gated_delta_rule(*[flip_seq(t) for t in (qc, kc, vc, gc[1], bc[1])], s0)
    ox_f, _ = gated_delta_rule(qx, kx, vx, gx[0], bx[0], sc_f)
    ox_b, _ = gated_delta_rule(*[flip_seq(t) for t in (qx, kx, vx, gx[1], bx[1])], sc_b)
    y_x = gated_head_norm(ox_f + flip_seq(ox_b), zx, norm_g)
    y_c = gated_head_norm(oc_f + flip_seq(oc_b), zc, norm_g) if ctx_out else None
    return y_x, y_c


def _gla_inputs(p, w2, b2):
    b, l, _ = p.shape
    q, k, v, r, lr = split_cols(p, (GLA_KDIM, GLA_KDIM, GLA_VDIM, GLA_VDIM, 2 * GLA_GATE_RANK))
    lr = lr.astype(jnp.float32).reshape(b, l, 2, GLA_GATE_RANK)
    log_a = jax.nn.log_sigmoid(jnp.einsum('blnr,nrk->nblk', lr, w2.astype(jnp.float32)) + b2.astype(jnp.float32)[:, None, None, :]) / GLA_TAU
    log_a = jnp.stack([split_heads(log_a[0], GLA_HEADS), split_heads(log_a[1], GLA_HEADS)])
    return split_heads(q, GLA_HEADS), split_heads(k, GLA_HEADS), split_heads(v, GLA_HEADS), r, log_a


def gla_branch(p_x, p_c, ctx_out, rows, w2, b2, norm_g):
    qx, kx, vx, rx, ax = _gla_inputs(to_col_major(p_x, rows), w2, b2)
    qc, kc, vc, rc, ac = _gla_inputs(p_c, w2, b2)
    s0 = jnp.zeros((p_x.shape[0], GLA_HEADS, GLA_DK, GLA_DV), jnp.float32)
    oc_f, sc_f = gla_chunked(qc, kc, vc, ac[0], s0)
    oc_b, sc_b = gla_chunked(*[flip_seq(t) for t in (qc, kc, vc, ac[1])], s0)
    ox_f, _ = gla_chunked(qx, kx, vx, ax[0], sc_f)
    ox_b, _ = gla_chunked(*[flip_seq(t) for t in (qx, kx, vx, ax[1])], sc_b)
    y_x = from_col_major(gated_head_norm(ox_f + flip_seq(ox_b), rx, norm_g), rows)
    y_c = gated_head_norm(oc_f + flip_seq(oc_b), rc, norm_g) if ctx_out else None
    return y_x, y_c


def merge_branches(feats, gate_logits, w_branch, w_o):
    b, l = feats.shape[:2]
    proj = jnp.einsum('blns,nsd->blnd', feats, w_branch)
    gates = jax.nn.sigmoid(gate_logits).reshape(b, l, N_BRANCH, D_MODEL)
    return jnp.sum(gates * proj, axis=2) @ w_o


def token_mixer(u_x, u_c, ctx_out, rows, w_in, b_in, dn_conv_w, dn_a_log, dn_dt_bias, dn_norm_g, gla_w2, gla_b2, gla_norm_g, conv_w, conv_b, conv_ln_g, conv_ln_b, w_branch, w_o):
    dn_x, gla_x, cv_x, fn_x, gt_x = split_cols(u_x @ w_in + b_in, IN_SPLITS)
    if ctx_out:
        dn_c, gla_c, cv_c, fn_c, gt_c = split_cols(u_c @ w_in + b_in, IN_SPLITS)
    else:
        dn_c, gla_c = split_cols(u_c @ w_in[:, :REC_COLS] + b_in[:REC_COLS], IN_SPLITS[:2])
    dn_ox, dn_oc = deltanet_branch(dn_x, dn_c, ctx_out, dn_conv_w, dn_a_log, dn_dt_bias, dn_norm_g)
    gla_ox, gla_oc = gla_branch(gla_x, gla_c, ctx_out, rows, gla_w2, gla_b2, gla_norm_g)
    feats_x = jnp.stack([conformer_conv(cv_x, conv_w, conv_b, conv_ln_g, conv_ln_b), dn_ox, fourier_mix(fn_x), gla_ox], axis=2)
    y_x = merge_branches(feats_x, gt_x, w_branch, w_o)
    if not ctx_out:
        return y_x, None
    feats_c = jnp.stack([conformer_conv(cv_c, conv_w, conv_b, conv_ln_g, conv_ln_b), dn_oc, fourier_mix(fn_c), gla_oc], axis=2)
    return y_x, merge_branches(feats_c, gt_c, w_branch, w_o)


def moe_ffn(t, w_router, b_router, w_gate, w_up, w_down, ws_gate, ws_up, ws_down):
    n_tok = t.shape[0]
    per_group = N_EXPERTS // N_EXPERT_GROUPS
    scores = jax.nn.sigmoid(t.astype(jnp.float32) @ w_router.astype(jnp.float32))
    biased = scores + b_router.astype(jnp.float32)
    grp_score = jnp.sum(lax.top_k(biased.reshape(n_tok, N_EXPERT_GROUPS, per_group), 2)[0], axis=-1)
    _, top_grp = lax.top_k(grp_score, TOP_GROUPS)
    grp_mask = jnp.sum(jax.nn.one_hot(top_grp, N_EXPERT_GROUPS, dtype=jnp.float32), axis=1) > 0
    masked = jnp.where(jnp.repeat(grp_mask, per_group, axis=1), biased, -jnp.inf)
    _, idx = lax.top_k(masked, TOP_K)
    wts = jnp.take_along_axis(scores, idx, axis=1)
    wts = wts / jnp.sum(wts, axis=-1, keepdims=True) * ROUTED_SCALE
    comb = jnp.einsum('tk,tke->te', wts, jax.nn.one_hot(idx, N_EXPERTS, dtype=jnp.float32)).astype(t.dtype)
    out = (jax.nn.silu(t @ ws_gate) * (t @ ws_up)) @ ws_down
    for gi in range(N_EXPERT_GROUPS):
        sl = slice(gi * per_group, (gi + 1) * per_group)
        hid = jax.nn.silu(jnp.einsum('td,edh->teh', t, w_gate[sl])) * jnp.einsum('td,edh->teh', t, w_up[sl])
        out = out + jnp.einsum('teh,ehd->td', hid * comb[:, sl, None], w_down[sl])
    return out


def setup_inputs(seed: int = 0) -> dict:
    key = jax.random.key(seed)
    ks = iter(jax.random.split(key, 40))
    nrm = lambda shape, scale: jax.random.normal(next(ks), shape, jnp.float32) * scale
    x = nrm((BATCH, SEQ, D_MODEL), 1.0)
    c = nrm((BATCH, D_MODEL), 1.0)
    ctx = nrm((BATCH, CTX_LEN, D_MODEL), 1.0)
    c_ctx = nrm((D_MODEL,), 1.0)
    w_mod = nrm((DEPTH, D_MODEL, 6 * D_MODEL), 0.5 * D_MODEL ** -0.5)
    b_mod = nrm((DEPTH, 6 * D_MODEL), 0.02)
    w_in = nrm((DEPTH, D_MODEL, IN_DIM), D_MODEL ** -0.5)
    b_in = nrm((DEPTH, IN_DIM), 0.02)
    dn_conv_w = nrm((DEPTH, DN_CONV_WIDTH, 3 * DN_DIM), DN_CONV_WIDTH ** -0.5)
    dn_a_log = jnp.log(jax.random.uniform(next(ks), (DEPTH, 2, DN_HEADS), jnp.float32, minval=1.0, maxval=16.0))
    dt = jnp.exp(jax.random.uniform(next(ks), (DEPTH, 2, DN_HEADS), jnp.float32, minval=math.log(1e-3), maxval=math.log(1e-1)))
    dn_dt_bias = dt + jnp.log(-jnp.expm1(-dt))
    dn_norm_g = 1.0 + nrm((DEPTH, DN_HEAD_DIM), 0.02)
    gla_w2 = nrm((DEPTH, 2, GLA_GATE_RANK, GLA_KDIM), GLA_GATE_RANK ** -0.5)
    gla_b2 = nrm((DEPTH, 2, GLA_KDIM), 0.02)
    gla_norm_g = 1.0 + nrm((DEPTH, GLA_DV), 0.02)
    conv_w = nrm((DEPTH, CONV_WIDTH, CONV_DIM), CONV_WIDTH ** -0.5)
    conv_b = nrm((DEPTH, CONV_DIM), 0.02)
    conv_ln_g = 1.0 + nrm((DEPTH, CONV_DIM), 0.02)
    conv_ln_b = nrm((DEPTH, CONV_DIM), 0.02)
    w_branch = nrm((DEPTH, N_BRANCH, BRANCH_DIM, D_MODEL), BRANCH_DIM ** -0.5)
    w_o = nrm((DEPTH, D_MODEL, D_MODEL), DEEPNORM_BETA * D_MODEL ** -0.5)
    ln_g = 1.0 + nrm((DEPTH, 2, D_MODEL), 0.02)
    ln_b = nrm((DEPTH, 2, D_MODEL), 0.02)
    w_router = nrm((DEPTH, D_MODEL, N_EXPERTS), D_MODEL ** -0.5)
    b_router = nrm((DEPTH, N_EXPERTS), 0.01)
    w_gate = nrm((DEPTH, N_EXPERTS, D_MODEL, EXPERT_DIM), D_MODEL ** -0.5)
    w_up = nrm((DEPTH, N_EXPERTS, D_MODEL, EXPERT_DIM), D_MODEL ** -0.5)
    w_down = nrm((DEPTH, N_EXPERTS, EXPERT_DIM, D_MODEL), DEEPNORM_BETA * EXPERT_DIM ** -0.5)
    ws_gate = nrm((DEPTH, D_MODEL, EXPERT_DIM), D_MODEL ** -0.5)
    ws_up = nrm((DEPTH, D_MODEL, EXPERT_DIM), D_MODEL ** -0.5)
    ws_down = nrm((DEPTH, EXPERT_DIM, D_MODEL), DEEPNORM_BETA * EXPERT_DIM ** -0.5)
    return {'x': x, 'c': c, 'ctx': ctx, 'c_ctx': c_ctx, 'w_mod': w_mod, 'b_mod': b_mod, 'w_in': w_in, 'b_in': b_in,
            'dn_conv_w': dn_conv_w, 'dn_a_log': dn_a_log, 'dn_dt_bias': dn_dt_bias, 'dn_norm_g': dn_norm_g,
            'gla_w2': gla_w2, 'gla_b2': gla_b2, 'gla_norm_g': gla_norm_g,
            'conv_w': conv_w, 'conv_b': conv_b, 'conv_ln_g': conv_ln_g, 'conv_ln_b': conv_ln_b,
            'w_branch': w_branch, 'w_o': w_o, 'ln_g': ln_g, 'ln_b': ln_b,
            'w_router': w_router, 'b_router': b_router, 'w_gate': w_gate, 'w_up': w_up, 'w_down': w_down,
            'ws_gate': ws_gate, 'ws_up': ws_up, 'ws_down': ws_down}


def reference(x, c, ctx, c_ctx, w_mod, b_mod, w_in, b_in, dn_conv_w, dn_a_log, dn_dt_bias, dn_norm_g, gla_w2, gla_b2, gla_norm_g, conv_w, conv_b, conv_ln_g, conv_ln_b, w_branch, w_o, ln_g, ln_b, w_router, b_router, w_gate, w_up, w_down, ws_gate, ws_up, ws_down):
    batch, seq, _ = x.shape
    ctx_len = ctx.shape[1]
    rows = seq // GRID_W
    c_act = jax.nn.silu(c)
    cc_act = jax.nn.silu(c_ctx)
    h, hc = x, ctx
    for l in range(DEPTH):
        ctx_out = l < DEPTH - 1
        sh_a, sc_a, g_a, sh_f, sc_f, g_f = [m[:, None, :] for m in jnp.split(c_act @ w_mod[l] + b_mod[l], 6, axis=-1)]
        csh_a, csc_a, cg_a, csh_f, csc_f, cg_f = jnp.split(cc_act @ w_mod[l] + b_mod[l], 6, axis=-1)
        y_x, y_c = token_mixer(modulate(layer_norm(h), sh_a, sc_a), modulate(layer_norm(hc), csh_a, csc_a), ctx_out, rows,
                               w_in[l], b_in[l], dn_conv_w[l], dn_a_log[l], dn_dt_bias[l], dn_norm_g[l],
                               gla_w2[l], gla_b2[l], gla_norm_g[l], conv_w[l], conv_b[l], conv_ln_g[l], conv_ln_b[l],
                               w_branch[l], w_o[l])
        h = post_norm(DEEPNORM_ALPHA * h + g_a * y_x, ln_g[l, 0], ln_b[l, 0])
        v_x = modulate(layer_norm(h), sh_f, sc_f)
        moe_args = (w_router[l], b_router[l], w_gate[l], w_up[l], w_down[l], ws_gate[l], ws_up[l], ws_down[l])
        if ctx_out:
            hc = post_norm(DEEPNORM_ALPHA * hc + cg_a * y_c, ln_g[l, 0], ln_b[l, 0])
            v_c = modulate(layer_norm(hc), csh_f, csc_f)
            f = moe_ffn(jnp.concatenate([v_c, v_x], axis=1).reshape(-1, D_MODEL), *moe_args).reshape(batch, ctx_len + seq, D_MODEL)
            hc = post_norm(DEEPNORM_ALPHA * hc + cg_f * f[:, :ctx_len], ln_g[l, 1], ln_b[l, 1])
            f_x = f[:, ctx_len:]
        else:
            f_x = moe_ffn(v_x.reshape(-1, D_MODEL), *moe_args).reshape(batch, seq, D_MODEL)
        h = post_norm(DEEPNORM_ALPHA * h + g_f * f_x, ln_g[l, 1], ln_b[l, 1])
    return h
```

```python
import functools
import math

import jax
import jax.numpy as jnp
import numpy as np
from jax import lax
from jax.experimental import pallas as pl
from jax.experimental.pallas import tpu as pltpu

F32 = jnp.float32
BF16 = jnp.bfloat16
HI = lax.Precision.HIGHEST

D_MODEL = 1024
DEPTH = 4
GRID_W = 64
CHUNK = 64
EPS = 1e-6
CONV_DIM = 512
CONV_WIDTH = 31
DN_HEADS = 4
DN_HEAD_DIM = 128
DN_DIM = 512
DN_CONV_WIDTH = 5
FN_GROUPS = 4
FN_GROUP_DIM = 128
FN_DIM = 512
GLA_HEADS = 4
GLA_DK = 64
GLA_DV = 128
GLA_KDIM = 256
GLA_VDIM = 512
GLA_GATE_RANK = 16
GLA_TAU = 16.0
N_BRANCH = 4
BRANCH_DIM = 512
N_EXPERTS = 64
N_EXPERT_GROUPS = 8
TOP_GROUPS = 4
TOP_K = 6
EXPERT_DIM = 256
ROUTED_SCALE = 2.5
DEEPNORM_ALPHA = (2 * DEPTH) ** 0.25

_DN0 = 0
_GLA0 = 4 * DN_DIM + 4 * DN_HEADS
_CONV0 = _GLA0 + 2 * GLA_KDIM + 2 * GLA_VDIM + 2 * GLA_GATE_RANK
_FN0 = _CONV0 + 2 * CONV_DIM
_GATE0 = _FN0 + FN_DIM
IN_DIM = _GATE0 + N_BRANCH * D_MODEL

MAIN_GATE, MAIN_DN, MAIN_CONV, MAIN_FN = 0, 4096, 6144, 7168
MAIN_COLS = 7680
HALO = 16
VMEM_LIMIT = 56 * 1024 * 1024


def _cp(sem, vmem=None):
    return pltpu.CompilerParams(dimension_semantics=sem, vmem_limit_bytes=vmem or VMEM_LIMIT)


def _sigmoid(x):
    return 1.0 / (1.0 + jnp.exp(-x))


def _silu(x):
    return x * _sigmoid(x)


def _softplus(x):
    return jnp.maximum(x, 0.0) + jnp.log(1.0 + jnp.exp(-jnp.abs(x)))


def _ln(x):
    mu = jnp.mean(x, axis=-1, keepdims=True)
    xc = x - mu
    var = jnp.mean(xc * xc, axis=-1, keepdims=True)
    return xc * lax.rsqrt(var + EPS)


def _dot(a, b):
    return jnp.dot(a.astype(BF16), b.astype(BF16), preferred_element_type=F32)


def _dot_nt(a, b):
    return lax.dot_general(a.astype(BF16), b.astype(BF16), (((1,), (1,)), ((), ())), preferred_element_type=F32)


def _dot_tn(a, b):
    return lax.dot_general(a.astype(BF16), b.astype(BF16), (((0,), (0,)), ((), ())), preferred_element_type=F32)


def _dot_hi(a, b):
    return jnp.dot(a, b, precision=HI, preferred_element_type=F32)


def _mod_kernel(c_ref, w_ref, b_ref, o_ref):
    o_ref[...] = _dot_hi(_silu(c_ref[...]), w_ref[...]) + b_ref[...]


def adaln_vectors(c8, w_mod, b_mod):
    depth, d, n = w_mod.shape
    tn = 1536
    return pl.pallas_call(
        _mod_kernel,
        grid=(depth, n // tn),
        in_specs=[pl.BlockSpec((8, d), lambda l, j: (0, 0)),
                  pl.BlockSpec((None, d, tn), lambda l, j: (l, 0, j)),
                  pl.BlockSpec((None, 1, tn), lambda l, j: (l, 0, j))],
        out_specs=pl.BlockSpec((None, 8, tn), lambda l, j: (l, 0, j)),
        out_shape=jax.ShapeDtypeStruct((depth, 8, n), F32),
        compiler_params=_cp(("parallel", "parallel")),
        name="adaln_vectors",
    )(c8, w_mod, b_mod.reshape(depth, 1, n))


def _lnmod_kernel(h_ref, mod_ref, o_ref):
    u = _ln(h_ref[...]) * (1.0 + mod_ref[1:2, :]) + mod_ref[0:1, :]
    o_ref[...] = u.astype(o_ref.dtype)


def ln_modulate(h, mod):
    b, l, d = h.shape
    tm = min(l, 512)
    return pl.pallas_call(
        _lnmod_kernel,
        grid=(b, l // tm),
        in_specs=[pl.BlockSpec((None, tm, d), lambda i, j: (i, j, 0)),
                  pl.BlockSpec((None, 8, d), lambda i, j: (i, 0, 0))],
        out_specs=pl.BlockSpec((None, tm, d), lambda i, j: (i, j, 0)),
        out_shape=jax.ShapeDtypeStruct((b, l, d), BF16),
        compiler_params=_cp(("parallel", "parallel")),
        name="ln_modulate",
    )(h, mod)


def _mm_kernel(x_ref, w_ref, b_ref, o_ref):
    o_ref[...] = (jnp.dot(x_ref[...], w_ref[...], preferred_element_type=F32) + b_ref[...]).astype(o_ref.dtype)


def matmul_bias(x, w, b, out_dtype, tn):
    m, k = x.shape
    n = w.shape[1]
    tm = 1024 if m % 1024 == 0 else 512
    return pl.pallas_call(
        _mm_kernel,
        grid=(m // tm, n // tn),
        in_specs=[pl.BlockSpec((tm, k), lambda i, j: (i, 0)),
                  pl.BlockSpec((k, tn), lambda i, j: (0, j)),
                  pl.BlockSpec((1, tn), lambda i, j: (0, j))],
        out_specs=pl.BlockSpec((tm, tn), lambda i, j: (i, j)),
        out_shape=jax.ShapeDtypeStruct((m, n), out_dtype),
        compiler_params=_cp(("parallel", "parallel")),
        name="matmul_bias",
    )(x, w, b.reshape(1, n).astype(F32))


def _conv_rows(g_scr, w_ref, width, r0, rs):
    pad = (width - 1) // 2
    acc = None
    for k in range(width):
        term = w_ref[k:k + 1, :] * g_scr[pl.ds(r0 + HALO + k - pad, rs), :]
        acc = term if acc is None else acc + term
    return acc


def _conformer_kernel(vc, vp, vn, gc, gp, gn, w_ref, cb_ref, lg_ref, lb_ref, o_ref, g_scr, *, tl, rs):
    i = pl.program_id(1)
    nt = pl.num_programs(1)
    glu = lambda v, g: v[...].astype(F32) * _sigmoid(g[...].astype(F32))
    g_scr[HALO:HALO + tl, :] = glu(vc, gc)
    g_scr[0:HALO, :] = jnp.where(i > 0, glu(vp, gp), 0.0)
    g_scr[HALO + tl:2 * HALO + tl, :] = jnp.where(i < nt - 1, glu(vn, gn), 0.0)
    for s in range(tl // rs):
        y = _conv_rows(g_scr, w_ref, CONV_WIDTH, s * rs, rs) + cb_ref[...]
        y = _silu(_ln(y) * lg_ref[...] + lb_ref[...])
        o_ref[s * rs:(s + 1) * rs, :] = y.astype(o_ref.dtype)


def conformer_conv(p_main, conv_w, conv_b, ln_g, ln_b):
    b, l, _ = p_main.shape
    c = CONV_DIM
    tl = min(l, 512)
    rs = 64
    hb = tl // HALO
    nhb = l // HALO
    vblk, gblk = MAIN_CONV // c, MAIN_CONV // c + 1
    cur = lambda cb: pl.BlockSpec((None, tl, c), lambda bi, i: (bi, i, cb))
    prv = lambda cb: pl.BlockSpec((None, HALO, c), lambda bi, i: (bi, jnp.maximum(i * hb - 1, 0), cb))
    nxt = lambda cb: pl.BlockSpec((None, HALO, c), lambda bi, i: (bi, jnp.minimum((i + 1) * hb, nhb - 1), cb))
    vec = pl.BlockSpec((1, c), lambda bi, i: (0, 0))
    return pl.pallas_call(
        functools.partial(_conformer_kernel, tl=tl, rs=rs),
        grid=(b, l // tl),
        in_specs=[cur(vblk), prv(vblk), nxt(vblk), cur(gblk), prv(gblk), nxt(gblk),
                  pl.BlockSpec((CONV_WIDTH, c), lambda bi, i: (0, 0)), vec, vec, vec],
        out_specs=pl.BlockSpec((None, tl, c), lambda bi, i: (bi, i, 0)),
        out_shape=jax.ShapeDtypeStruct((b, l, c), BF16),
        scratch_shapes=[pltpu.VMEM((tl + 2 * HALO, c), F32)],
        compiler_params=_cp(("parallel", "parallel")),
        name="conformer_conv",
    )(p_main, p_main, p_main, p_main, p_main, p_main, conv_w,
      conv_b.reshape(1, c), ln_g.reshape(1, c), ln_b.reshape(1, c))


def _shortconv_kernel(xc, xp, xn, w_ref, o_ref, g_scr, *, tl, rs):
    i = pl.program_id(1)
    nt = pl.num_programs(1)
    g_scr[HALO:HALO + tl, :] = xc[...].astype(F32)
    g_scr[0:HALO, :] = jnp.where(i > 0, xp[...].astype(F32), 0.0)
    g_scr[HALO + tl:2 * HALO + tl, :] = jnp.where(i < nt - 1, xn[...].astype(F32), 0.0)
    for s in range(tl // rs):
        y = _silu(_conv_rows(g_scr, w_ref, DN_CONV_WIDTH, s * rs, rs))
        o_ref[s * rs:(s + 1) * rs, :] = y.astype(o_ref.dtype)


def deltanet_shortconv(p_main, dn_conv_w):
    b, l, _ = p_main.shape
    c = 512
    tl = min(l, 512)
    rs = 64
    hb = tl // HALO
    nhb = l // HALO
    cb0 = MAIN_DN // c
    return pl.pallas_call(
        functools.partial(_shortconv_kernel, tl=tl, rs=rs),
        grid=(b, l // tl, 3),
        in_specs=[pl.BlockSpec((None, tl, c), lambda bi, i, j: (bi, i, cb0 + j)),
                  pl.BlockSpec((None, HALO, c), lambda bi, i, j: (bi, jnp.maximum(i * hb - 1, 0), cb0 + j)),
                  pl.BlockSpec((None, HALO, c), lambda bi, i, j: (bi, jnp.minimum((i + 1) * hb, nhb - 1), cb0 + j)),
                  pl.BlockSpec((DN_CONV_WIDTH, c), lambda bi, i, j: (0, j))],
        out_specs=pl.BlockSpec((None, tl, c), lambda bi, i, j: (bi, i, j)),
        out_shape=jax.ShapeDtypeStruct((b, l, 3 * c), BF16),
        scratch_shapes=[pltpu.VMEM((tl + 2 * HALO, c), F32)],
        compiler_params=_cp(("parallel", "parallel", "parallel")),
        name="deltanet_shortconv",
    )(p_main, p_main, p_main, dn_conv_w)


def _chunk_masks(rev):
    r = lax.broadcasted_iota(jnp.int32, (CHUNK, CHUNK), 0)
    c = lax.broadcasted_iota(jnp.int32, (CHUNK, CHUNK), 1)
    if rev:
        cum, sx, incl, strict = c >= r, r < c, c >= r, c > r
    else:
        cum, sx, incl, strict = c <= r, r > c, c <= r, c < r
    return cum.astype(F32), sx.astype(F32), incl, strict, (r == c).astype(F32)


def _dn_chunk(q, k, v, beta_b, g_b, s, masks, rev):
    cum, sx, incl, strict, eye = masks
    gs = _dot_hi(cum, g_b)
    e = _dot_hi(cum, g_b[:, :CHUNK] * sx)
    decay = jnp.where(incl, jnp.exp(e), 0.0)
    kb = k * beta_b
    m = jnp.where(strict, _dot_nt(kb, k) * decay, 0.0)
    inv = eye - m
    p = m
    for _ in range(int(math.log2(CHUNK)) - 1):
        p = _dot(p, p)
        inv = inv + _dot(inv, p)
    u = _dot(inv, v * beta_b)
    w = _dot(inv, kb * jnp.exp(gs))
    qs = q * (DN_HEAD_DIM ** -0.5)
    a = _dot_nt(qs, k) * decay
    v_new = u - _dot(w, s)
    o = _dot(qs * jnp.exp(gs), s) + _dot(a, v_new)
    g_tot = gs[0:1, :] if rev else gs[CHUNK - 1:CHUNK, :]
    s_new = s * jnp.exp(g_tot) + _dot_tn(k * jnp.exp(g_tot - gs), v_new)
    return o, s_new


def _l2n(t):
    return t * lax.rsqrt(jnp.sum(t * t, axis=-1, keepdims=True) + EPS)


def _dn_scan_kernel(xf_ref, bgf_ref, xb_ref, bgb_ref, par_ref, s0_ref, of_ref, ob_ref, sfin_ref, s_scr):
    i = pl.program_id(1)
    n = pl.num_programs(1)

    @pl.when(i == 0)
    def _():
        s_scr[...] = s0_ref[...]

    hd = DN_HEAD_DIM
    for d, (x_ref, bg_ref, o_ref) in enumerate(((xf_ref, bgf_ref, of_ref), (xb_ref, bgb_ref, ob_ref))):
        masks = _chunk_masks(rev=bool(d))
        bg = bg_ref[...]
        beta_all = _sigmoid(bg)
        g_all = -jnp.exp(par_ref[0:1, :]) * _softplus(bg + par_ref[1:2, :])
        for h in range(DN_HEADS):
            q = _l2n(x_ref[:, h * hd:(h + 1) * hd].astype(F32))
            k = _l2n(x_ref[:, DN_DIM + h * hd:DN_DIM + (h + 1) * hd].astype(F32))
            v = x_ref[:, 2 * DN_DIM + h * hd:2 * DN_DIM + (h + 1) * hd].astype(F32)
            cb = d * DN_HEADS + h
            beta_b = jnp.broadcast_to(beta_all[:, cb:cb + 1], (CHUNK, hd))
            g_b = jnp.broadcast_to(g_all[:, 2 * DN_HEADS + cb:2 * DN_HEADS + cb + 1], (CHUNK, hd))
            o, s_new = _dn_chunk(q, k, v, beta_b, g_b, s_scr[d, h], masks, bool(d))
            o_ref[:, h * hd:(h + 1) * hd] = o
            s_scr[d, h] = s_new

    @pl.when(i == n - 1)
    def _():
        sfin_ref[...] = s_scr[...]


def deltanet_scan(qkv, bg, par, s0):
    b, l, _ = qkv.shape
    n = l // CHUNK
    fwd = lambda w: pl.BlockSpec((None, CHUNK, w), lambda bi, i: (bi, i, 0))
    bwd = lambda w: pl.BlockSpec((None, CHUNK, w), lambda bi, i: (bi, n - 1 - i, 0))
    st = pl.BlockSpec((None, 2, DN_HEADS, DN_HEAD_DIM, DN_HEAD_DIM), lambda bi, i: (bi, 0, 0, 0, 0))
    return pl.pallas_call(
        _dn_scan_kernel,
        grid=(b, n),
        in_specs=[fwd(3 * DN_DIM), fwd(128), bwd(3 * DN_DIM), bwd(128),
                  pl.BlockSpec((8, 128), lambda bi, i: (0, 0)), st],
        out_specs=[fwd(DN_DIM), bwd(DN_DIM), st],
        out_shape=[jax.ShapeDtypeStruct((b, l, DN_DIM), F32), jax.ShapeDtypeStruct((b, l, DN_DIM), F32),
                   jax.ShapeDtypeStruct(s0.shape, F32)],
        scratch_shapes=[pltpu.VMEM((2, DN_HEADS, DN_HEAD_DIM, DN_HEAD_DIM), F32)],
        compiler_params=_cp(("parallel", "arbitrary")),
        name="deltanet_scan",
    )(qkv, bg, qkv, bg, par, s0)


GLA_SUB = 16


def _gla_chunk(qp, kp, vp, bp, st, rev):
    row = lax.broadcasted_iota(jnp.int32, (CHUNK, 1), 0)
    lane = lax.broadcasted_iota(jnp.int32, (1, 2 * GLA_DK), 1)
    o_inter = _dot_nt(qp * jnp.exp(bp), st)
    blocks = ([], [])
    for blk in range(CHUNK // GLA_SUB):
        i0 = blk * GLA_SUB
        mid = i0 + GLA_SUB // 2
        ref = bp[mid:mid + 1, :]
        qt = qp[i0:i0 + GLA_SUB, :] * jnp.exp(bp[i0:i0 + GLA_SUB, :] - ref)
        valid = (row >= i0) if rev else (row < i0 + GLA_SUB)
        kt = kp * jnp.exp(jnp.where(valid, ref - bp, 0.0))
        ri = lax.broadcasted_iota(jnp.int32, (GLA_SUB, CHUNK), 0) + i0
        ci = lax.broadcasted_iota(jnp.int32, (GLA_SUB, CHUNK), 1)
        causal = (ci >= ri) if rev else (ci <= ri)
        for hh in range(2):
            qh = jnp.where((lane >= hh * GLA_DK) & (lane < (hh + 1) * GLA_DK), qt, 0.0)
            blocks[hh].append(jnp.where(causal, _dot_nt(qh, kt), 0.0))
    o_intra = [_dot(jnp.concatenate(blocks[hh], axis=0), vp[:, hh * GLA_DV:(hh + 1) * GLA_DV]) for hh in range(2)]
    o = o_inter + jnp.concatenate(o_intra, axis=1)
    b_last = bp[0:1, :] if rev else bp[CHUNK - 1:CHUNK, :]
    srow = lax.broadcasted_iota(jnp.int32, (2 * GLA_DV, 2 * GLA_DK), 0)
    scol = lax.broadcasted_iota(jnp.int32, (2 * GLA_DV, 2 * GLA_DK), 1)
    bd = (srow < GLA_DV) == (scol < GLA_DK)
    st_new = jnp.where(bd, st * jnp.exp(b_last) + _dot_tn(vp, kp * jnp.exp(b_last - bp)), 0.0)
    return o, st_new


def _gla_scan_kernel(qkf, vf, lrf, qkb, vb, lrb, w2_ref, b2_ref, s0_ref, of_ref, ob_ref, sfin_ref, s_scr):
    i = pl.program_id(1)
    n = pl.num_programs(1)

    @pl.when(i == 0)
    def _():
        s_scr[...] = s0_ref[...]

    for d, (qk_ref, v_ref, lr_ref, o_ref) in enumerate(((qkf, vf, lrf, of_ref), (qkb, vb, lrb, ob_ref))):
        cum = _chunk_masks(rev=bool(d))[0]
        z = _dot_hi(lr_ref[...], w2_ref[d]) + b2_ref[d]
        bs = _dot_hi(cum, -_softplus(-z) * (1.0 / GLA_TAU))
        for p in range(GLA_HEADS // 2):
            lo = p * 2 * GLA_DK
            qp = qk_ref[:, lo:lo + 2 * GLA_DK].astype(F32) * (GLA_DK ** -0.5)
            kp = qk_ref[:, GLA_KDIM + lo:GLA_KDIM + lo + 2 * GLA_DK].astype(F32)
            vp = v_ref[:, p * 2 * GLA_DV:(p + 1) * 2 * GLA_DV].astype(F32)
            o, st_new = _gla_chunk(qp, kp, vp, bs[:, lo:lo + 2 * GLA_DK], s_scr[d, p], bool(d))
            o_ref[:, p * 2 * GLA_DV:(p + 1) * 2 * GLA_DV] = o
            s_scr[d, p] = st_new

    @pl.when(i == n - 1)
    def _():
        sfin_ref[...] = s_scr[...]


def gla_scan(p_gla, lr, w2p, b2, s0):
    b, l, _ = p_gla.shape
    n = l // CHUNK
    fwd = lambda w, cb: pl.BlockSpec((None, CHUNK, w), lambda bi, i: (bi, i, cb))
    bwd = lambda w, cb: pl.BlockSpec((None, CHUNK, w), lambda bi, i: (bi, n - 1 - i, cb))
    st = pl.BlockSpec((None, 2, 2, 2 * GLA_DV, 2 * GLA_DK), lambda bi, i: (bi, 0, 0, 0, 0))
    return pl.pallas_call(
        _gla_scan_kernel,
        grid=(b, n),
        in_specs=[fwd(512, 0), fwd(512, 1), fwd(128, 0), bwd(512, 0), bwd(512, 1), bwd(128, 0),
                  pl.BlockSpec((2, 128, GLA_KDIM), lambda bi, i: (0, 0, 0)),
                  pl.BlockSpec((2, 1, GLA_KDIM), lambda bi, i: (0, 0, 0)), st],
        out_specs=[fwd(GLA_VDIM, 0), bwd(GLA_VDIM, 0), st],
        out_shape=[jax.ShapeDtypeStruct((b, l, GLA_VDIM), F32), jax.ShapeDtypeStruct((b, l, GLA_VDIM), F32),
                   jax.ShapeDtypeStruct(s0.shape, F32)],
        scratch_shapes=[pltpu.VMEM((2, 2, 2 * GLA_DV, 2 * GLA_DK), F32)],
        compiler_params=_cp(("parallel", "arbitrary")),
        name="gla_scan",
    )(p_gla, p_gla, lr, p_gla, p_gla, lr, w2p, b2, s0)


def _headnorm_kernel(of_ref, ob_ref, z_ref, g_ref, o_ref):
    o = of_ref[...] + ob_ref[...]
    z = z_ref[...].astype(F32)
    for h in range(4):
        sl = slice(h * 128, (h + 1) * 128)
        oh = o[:, sl]
        oh = oh * lax.rsqrt(jnp.mean(oh * oh, axis=-1, keepdims=True) + EPS) * g_ref[...]
        o_ref[:, sl] = (oh * _silu(z[:, sl])).astype(o_ref.dtype)


def gated_head_norm(o_f, o_b, z_arr, z_blk, g):
    b, l, c = o_f.shape
    tl = min(l, 512)
    blk = pl.BlockSpec((None, tl, c), lambda bi, i: (bi, i, 0))
    return pl.pallas_call(
        _headnorm_kernel,
        grid=(b, l // tl),
        in_specs=[blk, blk, pl.BlockSpec((None, tl, c), lambda bi, i: (bi, i, z_blk)),
                  pl.BlockSpec((1, 128), lambda bi, i: (0, 0))],
        out_specs=blk,
        out_shape=jax.ShapeDtypeStruct((b, l, c), BF16),
        compiler_params=_cp(("parallel", "parallel")),
        name="gated_head_norm",
    )(o_f, o_b, z_arr, g.reshape(1, 128).astype(F32))


def _dft_tables(n):
    ang = 2.0 * np.pi * (np.outer(np.arange(n), np.arange(n)) % n) / n
    return np.cos(ang), np.sin(ang)


def _fnet_small_kernel(x_ref, wc_ref, cl_ref, sl_ref, o_ref):
    y = jnp.dot(x_ref[...], wc_ref[...], preferred_element_type=F32)
    out = _dot(cl_ref[...], y[:, :FN_GROUP_DIM]) + _dot(sl_ref[...], y[:, FN_GROUP_DIM:])
    o_ref[...] = out.astype(o_ref.dtype)


def _fnet_big_kernel(x_ref, wc_ref, f1_ref, twc_ref, tws_ref, f2_ref, o_ref, y_scr, yi_scr, b_scr, *, n1):
    n2 = FN_GROUP_DIM
    l = n1 * n2
    rb = 512

    def step0(r, c):
        r0 = pl.multiple_of(r * rb, rb)
        y = jnp.dot(x_ref[pl.ds(r0, rb), :], wc_ref[...], preferred_element_type=F32)
        y_scr[pl.ds(r0, rb), :] = y[:, :n2]
        yi_scr[pl.ds(r0, rb), :] = y[:, n2:]
        return c

    lax.fori_loop(0, l // rb, step0, 0)

    def step1(j, c):
        mr = _dot(f1_ref[...], y_scr[pl.ds(j, n1, stride=n2), :])
        mi = _dot(f1_ref[...], yi_scr[pl.ds(j, n1, stride=n2), :])
        ar = mr[:n1] + mi[n1:]
        ai = mi[:n1] - mr[n1:]
        tc, ts = twc_ref[j], tws_ref[j]
        b_scr[pl.ds(j, n1, stride=2 * n2), :] = ar * tc + ai * ts
        b_scr[pl.ds(n2 + j, n1, stride=2 * n2), :] = ai * tc - ar * ts
        return c

    lax.fori_loop(0, n2, step1, 0)

    def step2(k1, c):
        bk = b_scr[pl.ds(pl.multiple_of(k1 * 2 * n2, 2 * n2), 2 * n2), :]
        y_scr[pl.ds(k1, n2, stride=n1), :] = _dot(f2_ref[...], bk)
        return c

    lax.fori_loop(0, n1, step2, 0)

    def step3(r, c):
        r0 = pl.multiple_of(r * rb, rb)
        o_ref[pl.ds(r0, rb), :] = y_scr[pl.ds(r0, rb), :].astype(o_ref.dtype)
        return c

    lax.fori_loop(0, l // rb, step3, 0)


def fourier_mix(p_main):
    b, l, _ = p_main.shape
    gd = FN_GROUP_DIM
    cc, sc = _dft_tables(gd)
    wc = jnp.asarray(np.concatenate([cc, -sc], axis=1) / math.sqrt(gd), BF16)
    x_spec = pl.BlockSpec((None, l, gd), lambda bi, g: (bi, 0, MAIN_FN // gd + g))
    o_spec = pl.BlockSpec((None, l, gd), lambda bi, g: (bi, 0, g))
    full = lambda shape: pl.BlockSpec(shape, lambda bi, g: (0,) * len(shape))
    out_shape = jax.ShapeDtypeStruct((b, l, FN_DIM), BF16)
    if l <= 512:
        cl, sl = _dft_tables(l)
        scale = 1.0 / math.sqrt(l)
        return pl.pallas_call(
            _fnet_small_kernel, grid=(b, FN_GROUPS),
            in_specs=[x_spec, full((gd, 2 * gd)), full((l, l)), full((l, l))],
            out_specs=o_spec, out_shape=out_shape,
            compiler_params=_cp(("parallel", "parallel")), name="fourier_mix_small",
        )(p_main, wc, jnp.asarray(cl * scale, BF16), jnp.asarray(sl * scale, BF16))
    n1, n2 = l // gd, gd
    c1, s1 = _dft_tables(n1)
    c2, s2 = _dft_tables(n2)
    f1 = jnp.asarray(np.concatenate([c1, s1], axis=0), BF16)
    f2 = jnp.asarray(np.concatenate([c2, s2], axis=1) / math.sqrt(l), BF16)
    ang = 2.0 * np.pi * np.outer(np.arange(n2), np.arange(n1)) / l
    twc = jnp.asarray(np.broadcast_to(np.cos(ang)[:, :, None], (n2, n1, gd)), F32)
    tws = jnp.asarray(np.broadcast_to(np.sin(ang)[:, :, None], (n2, n1, gd)), F32)
    return pl.pallas_call(
        functools.partial(_fnet_big_kernel, n1=n1), grid=(b, FN_GROUPS),
        in_specs=[x_spec, full((gd, 2 * gd)), full((2 * n1, n1)), full((n2, n1, gd)), full((n2, n1, gd)),
                  full((n2, 2 * n2))],
        out_specs=o_spec, out_shape=out_shape,
        scratch_shapes=[pltpu.VMEM((l, gd), F32), pltpu.VMEM((l, gd), F32), pltpu.VMEM((n1 * 2 * n2, gd), F32)],
        compiler_params=_cp(("parallel", "parallel")), name="fourier_mix_big",
    )(p_main, wc, f1, twc, tws, f2)


def _merge_kernel(fa, fb, fc, fd, gt_ref, wb_ref, wo_ref, h_ref, mod_ref, lng_ref, lnb_ref, wr_ref,
                  h1_ref, v_ref, lg_ref):
    acc = None
    for n, f_ref in enumerate((fa, fb, fc, fd)):
        proj = jnp.dot(f_ref[...], wb_ref[n], preferred_element_type=F32)
        term = _sigmoid(gt_ref[:, n * D_MODEL:(n + 1) * D_MODEL].astype(F32)) * proj
        acc = term if acc is None else acc + term
    y = _dot(acc, wo_ref[...])
    h1 = _ln(DEEPNORM_ALPHA * h_ref[...] + mod_ref[2:3, :] * y) * lng_ref[...] + lnb_ref[...]
    h1_ref[...] = h1
    v = _ln(h1) * (1.0 + mod_ref[4:5, :]) + mod_ref[3:4, :]
    v_ref[...] = v.astype(v_ref.dtype)
    lg_ref[...] = lax.dot_general(wr_ref[...], v, (((1,), (1,)), ((), ())), precision=HI,
                                  preferred_element_type=F32)


def merge_branches(feats, p_main, w_branch, w_o, h, mod, ln_g, ln_b, w_router_t):
    b, l, d = h.shape
    tm = 256
    fblk = pl.BlockSpec((None, tm, BRANCH_DIM), lambda bi, i: (bi, i, 0))
    hblk = pl.BlockSpec((None, tm, d), lambda bi, i: (bi, i, 0))
    vec = pl.BlockSpec((1, d), lambda bi, i: (0, 0))
    return pl.pallas_call(
        _merge_kernel,
        grid=(b, l // tm),
        in_specs=[fblk, fblk, fblk, fblk,
                  pl.BlockSpec((None, tm, N_BRANCH * d), lambda bi, i: (bi, i, MAIN_GATE // (N_BRANCH * d))),
                  pl.BlockSpec((N_BRANCH, BRANCH_DIM, d), lambda bi, i: (0, 0, 0)),
                  pl.BlockSpec((d, d), lambda bi, i: (0, 0)),
                  hblk, pl.BlockSpec((None, 8, d), lambda bi, i: (bi, 0, 0)), vec, vec,
                  pl.BlockSpec((N_EXPERTS, d), lambda bi, i: (0, 0))],
        out_specs=[hblk, hblk, pl.BlockSpec((None, N_EXPERTS, tm), lambda bi, i: (bi, 0, i))],
        out_shape=[jax.ShapeDtypeStruct((b, l, d), F32), jax.ShapeDtypeStruct((b, l, d), BF16),
                   jax.ShapeDtypeStruct((b, N_EXPERTS, l), F32)],
        compiler_params=_cp(("parallel", "parallel")),
        name="merge_branches",
    )(*feats, p_main, w_branch, w_o, h, mod, ln_g.reshape(1, d), ln_b.reshape(1, d), w_router_t)


def _first_argmax(x, axis, size):
    m = jnp.max(x, axis=axis, keepdims=True)
    idx = lax.broadcasted_iota(jnp.int32, x.shape, axis)
    first = jnp.min(jnp.where(x == m, idx, size), axis=axis, keepdims=True)
    return m, idx == first


def _route_kernel(lg_ref, rb_ref, o_ref):
    tm = lg_ref.shape[-1]
    per = N_EXPERTS // N_EXPERT_GROUPS
    scores = _sigmoid(lg_ref[...])
    biased = scores + rb_ref[...]
    x3 = biased.reshape(N_EXPERT_GROUPS, per, tm)
    m1, hit = _first_argmax(x3, 1, per)
    m2 = jnp.max(jnp.where(hit, -jnp.inf, x3), axis=1, keepdims=True)
    gscore = (m1 + m2).reshape(N_EXPERT_GROUPS, tm)
    gsel = jnp.zeros(gscore.shape, F32)
    for _ in range(TOP_GROUPS):
        _, hit = _first_argmax(gscore, 0, N_EXPERT_GROUPS)
        gsel = jnp.where(hit, 1.0, gsel)
        gscore = jnp.where(hit, -jnp.inf, gscore)
    masked = jnp.where(gsel.reshape(N_EXPERT_GROUPS, 1, tm) > 0.0, x3, -jnp.inf).reshape(N_EXPERTS, tm)
    sel = jnp.zeros(masked.shape, F32)
    for _ in range(TOP_K):
        _, hit = _first_argmax(masked, 0, N_EXPERTS)
        sel = jnp.where(hit, 1.0, sel)
        masked = jnp.where(hit, -jnp.inf, masked)
    w = sel * scores
    o_ref[...] = w / jnp.sum(w, axis=0, keepdims=True) * ROUTED_SCALE


def moe_route(logits_t, b_router):
    b, e, l = logits_t.shape
    tm = 256
    blk = pl.BlockSpec((None, e, tm), lambda bi, i: (bi, 0, i))
    return pl.pallas_call(
        _route_kernel, grid=(b, l // tm),
        in_specs=[blk, pl.BlockSpec((e, tm), lambda bi, i: (0, 0))],
        out_specs=blk, out_shape=jax.ShapeDtypeStruct((b, e, l), F32),
        compiler_params=_cp(("parallel", "parallel")), name="moe_route",
    )(logits_t, jnp.broadcast_to(b_router.astype(F32)[:, None], (e, tm)))


MOE_EB = 2


def _moe_kernel(x_ref, comb_ref, wg_ref, wu_ref, wd_ref, sg_ref, su_ref, sd_ref, h_ref, mod_ref, nmod_ref,
                lng_ref, lnb_ref, h2_ref, u_ref, acc):
    j = pl.program_id(2)
    nj = pl.num_programs(2)
    x = x_ref[...]

    @pl.when(j == 0)
    def _():
        hid = _silu(_dot(x, sg_ref[...])) * _dot(x, su_ref[...])
        acc[...] = _dot(hid, sd_ref[...])

    comb = comb_ref[...]
    lane = lax.broadcasted_iota(jnp.int32, (1, comb.shape[1]), 1)
    for e in range(MOE_EB):
        wcol = jnp.sum(jnp.where(lane == j * MOE_EB + e, comb, 0.0), axis=-1, keepdims=True)
        hid = _silu(_dot(x, wg_ref[e])) * _dot(x, wu_ref[e]) * wcol
        acc[...] += _dot(hid, wd_ref[e])

    @pl.when(j == nj - 1)
    def _():
        h2 = _ln(DEEPNORM_ALPHA * h_ref[...] + mod_ref[5:6, :] * acc[...]) * lng_ref[...] + lnb_ref[...]
        h2_ref[...] = h2
        u_ref[...] = (_ln(h2) * (1.0 + nmod_ref[1:2, :]) + nmod_ref[0:1, :]).astype(u_ref.dtype)


def moe_dense(v, comb, w_gate, w_up, w_down, ws_gate, ws_up, ws_down, h1, mod, next_mod, ln_g, ln_b):
    b, l, d = h1.shape
    tm = min(l, 1024)
    ne = w_gate.shape[0]
    xblk = pl.BlockSpec((None, tm, d), lambda bi, i, j: (bi, i, 0))
    vec = pl.BlockSpec((1, d), lambda bi, i, j: (0, 0))
    modblk = pl.BlockSpec((None, 8, d), lambda bi, i, j: (bi, 0, 0))
    return pl.pallas_call(
        _moe_kernel,
        grid=(b, l // tm, ne // MOE_EB),
        in_specs=[xblk, pl.BlockSpec((None, tm, 128), lambda bi, i, j: (bi, i, 0)),
                  pl.BlockSpec((MOE_EB, d, EXPERT_DIM), lambda bi, i, j: (j, 0, 0)),
                  pl.BlockSpec((MOE_EB, d, EXPERT_DIM), lambda bi, i, j: (j, 0, 0)),
                  pl.BlockSpec((MOE_EB, EXPERT_DIM, d), lambda bi, i, j: (j, 0, 0)),
                  pl.BlockSpec((d, EXPERT_DIM), lambda bi, i, j: (0, 0)),
                  pl.BlockSpec((d, EXPERT_DIM), lambda bi, i, j: (0, 0)),
                  pl.BlockSpec((EXPERT_DIM, d), lambda bi, i, j: (0, 0)),
                  xblk, modblk, modblk, vec, vec],
        out_specs=[xblk, xblk],
        out_shape=[jax.ShapeDtypeStruct((b, l, d), F32), jax.ShapeDtypeStruct((b, l, d), BF16)],
        scratch_shapes=[pltpu.VMEM((tm, d), F32)],
        compiler_params=_cp(("parallel", "parallel", "arbitrary")),
        name="moe_dense",
    )(v, comb, w_gate, w_up, w_down, ws_gate, ws_up, ws_down, h1, mod, next_mod,
      ln_g.reshape(1, d), ln_b.reshape(1, d))


def _to_col_major(t):
    b, l, ch = t.shape
    return t.reshape(b, l // GRID_W, GRID_W, ch).transpose(0, 2, 1, 3).reshape(b, l, ch)


def _from_col_major(t):
    b, l, ch = t.shape
    return t.reshape(b, GRID_W, l // GRID_W, ch).transpose(0, 2, 1, 3).reshape(b, l, ch)


def _layer_weights(l, w_in, b_in, dn_a_log, dn_dt_bias, gla_w2, gla_b2):
    w, bvec = w_in[l], b_in[l]
    cols = lambda a, n: (w[:, a:a + n], bvec[a:a + n])
    parts = (cols(_GATE0, 4096), cols(_DN0, 2048), cols(_CONV0, 1024), cols(_FN0, 512))
    w_main = jnp.concatenate([q[0] for q in parts], axis=1).astype(BF16)
    b_main = jnp.concatenate([q[1] for q in parts])
    pad = lambda wb, n: (jnp.pad(wb[0], ((0, 0), (0, 128 - n))).astype(BF16), jnp.pad(wb[1], (0, 128 - n)))
    w_bg, b_bg = pad(cols(2048, 16), 16)
    w_gla, b_gla = cols(_GLA0, 1536)
    w_lr, b_lr = pad(cols(_GLA0 + 1536, 32), 32)
    par = jnp.zeros((8, 128), F32)
    par = par.at[0, 8:16].set(dn_a_log[l].reshape(-1)).at[1, 8:16].set(dn_dt_bias[l].reshape(-1))
    w2p = jnp.zeros((2, 128, GLA_KDIM), F32)
    w2p = w2p.at[0, 0:16].set(gla_w2[l, 0]).at[1, 16:32].set(gla_w2[l, 1])
    return dict(w_main=w_main, b_main=b_main, w_bg=w_bg, b_bg=b_bg, w_gla=w_gla.astype(BF16), b_gla=b_gla,
                w_lr=w_lr, b_lr=b_lr, par=par, w2p=w2p, b2=gla_b2[l].reshape(2, 1, GLA_KDIM).astype(F32))


def _project(u, lw, col_major):
    b, l, d = u.shape
    flat = u.reshape(b * l, d)
    ug = _to_col_major(u).reshape(b * l, d) if col_major else flat
    p_main = matmul_bias(flat, lw["w_main"], lw["b_main"], BF16, 512).reshape(b, l, MAIN_COLS)
    bg = matmul_bias(flat, lw["w_bg"], lw["b_bg"], F32, 128).reshape(b, l, 128)
    p_gla = matmul_bias(ug, lw["w_gla"], lw["b_gla"], BF16, 512).reshape(b, l, 1536)
    lr = matmul_bias(ug, lw["w_lr"], lw["b_lr"], F32, 128).reshape(b, l, 128)
    return p_main, bg, p_gla, lr


def kernel(x, c, ctx, c_ctx, w_mod, b_mod, w_in, b_in, dn_conv_w, dn_a_log, dn_dt_bias, dn_norm_g, gla_w2, gla_b2, gla_norm_g, conv_w, conv_b, conv_ln_g, conv_ln_b, w_branch, w_o, ln_g, ln_b, w_router, b_router, w_gate, w_up, w_down, ws_gate, ws_up, ws_down):
    batch, seq, d = x.shape
    c8 = jnp.zeros((8, d), F32).at[:batch].set(c).at[batch].set(c_ctx)
    mods = adaln_vectors(c8, w_mod, b_mod).reshape(DEPTH, 8, 6, d)
    zrow = jnp.zeros((DEPTH, 2, d), F32)
    mod_x = [jnp.concatenate([mods[l, :batch], jnp.broadcast_to(zrow[l][None], (batch, 2, d))], axis=1)
             for l in range(DEPTH)]
    mod_c = [jnp.broadcast_to(jnp.concatenate([mods[l, batch], zrow[l]], axis=0)[None], (batch, 8, d))
             for l in range(DEPTH)]

    h, hc = x, ctx
    u_x, u_c = ln_modulate(h, mod_x[0]), ln_modulate(hc, mod_c[0])
    for l in range(DEPTH):
        lw = _layer_weights(l, w_in, b_in, dn_a_log, dn_dt_bias, gla_w2, gla_b2)
        px = _project(u_x, lw, True)
        pc = _project(u_c, lw, False)
        qkv_c, qkv_x = deltanet_shortconv(pc[0], dn_conv_w[l]), deltanet_shortconv(px[0], dn_conv_w[l])
        s0 = jnp.zeros((batch, 2, DN_HEADS, DN_HEAD_DIM, DN_HEAD_DIM), F32)
        ocf, ocb, s_c = deltanet_scan(qkv_c, pc[1], lw["par"], s0)
        oxf, oxb, _ = deltanet_scan(qkv_x, px[1], lw["par"], s_c)
        dn_x = gated_head_norm(oxf, oxb, px[0], MAIN_DN // 512 + 3, dn_norm_g[l])
        dn_c = gated_head_norm(ocf, ocb, pc[0], MAIN_DN // 512 + 3, dn_norm_g[l])
        g0 = jnp.zeros((batch, 2, 2, 2 * GLA_DV, 2 * GLA_DK), F32)
        gcf, gcb, gs_c = gla_scan(pc[2], pc[3], lw["w2p"], lw["b2"], g0)
        gxf, gxb, _ = gla_scan(px[2], px[3], lw["w2p"], lw["b2"], gs_c)
        gla_x = _from_col_major(gated_head_norm(gxf, gxb, px[2], 2, gla_norm_g[l]))
        gla_c = gated_head_norm(gcf, gcb, pc[2], 2, gla_norm_g[l])
        wb = w_branch[l].astype(BF16)
        wo = w_o[l].astype(BF16)
        wrt = w_router[l].T
        nxt = min(l + 1, DEPTH - 1)
        outs = []
        for (p, dn_f, gla_f, hh, mod, nmod) in ((px, dn_x, gla_x, h, mod_x[l], mod_x[nxt]),
                                               (pc, dn_c, gla_c, hc, mod_c[l], mod_c[nxt])):
            feats = (conformer_conv(p[0], conv_w[l], conv_b[l], conv_ln_g[l], conv_ln_b[l]), dn_f,
                     fourier_mix(p[0]), gla_f)
            h1, v, lg = merge_branches(feats, p[0], wb, wo, hh, mod, ln_g[l, 0], ln_b[l, 0], wrt)
            comb_t = moe_route(lg, b_router[l])
            comb = jnp.pad(comb_t.transpose(0, 2, 1), ((0, 0), (0, 0), (0, 128 - N_EXPERTS)))
            outs.append(moe_dense(v, comb, w_gate[l], w_up[l], w_down[l], ws_gate[l], ws_up[l], ws_down[l],
                                  h1, mod, nmod, ln_g[l, 1], ln_b[l, 1]))
        (h, u_x), (hc, u_c) = outs
    return h
```

```python
import functools
import math

import jax
import jax.numpy as jnp
import numpy as np
from jax import lax
from jax.experimental import pallas as pl
from jax.experimental.pallas import tpu as pltpu

F32 = jnp.float32
BF16 = jnp.bfloat16
HI = lax.Precision.HIGHEST

D_MODEL = 1024
DEPTH = 4
GRID_W = 64
CHUNK = 64
EPS = 1e-6
CONV_DIM = 512
CONV_WIDTH = 31
DN_HEADS = 4
DN_HEAD_DIM = 128
DN_DIM = 512
DN_CONV_WIDTH = 5
FN_GROUPS = 4
FN_GROUP_DIM = 128
FN_DIM = 512
GLA_HEADS = 4
GLA_DK = 64
GLA_DV = 128
GLA_KDIM = 256
GLA_VDIM = 512
GLA_GATE_RANK = 16
GLA_TAU = 16.0
N_BRANCH = 4
BRANCH_DIM = 512
N_EXPERTS = 64
N_EXPERT_GROUPS = 8
TOP_GROUPS = 4
TOP_K = 6
EXPERT_DIM = 256
ROUTED_SCALE = 2.5
DEEPNORM_ALPHA = (2 * DEPTH) ** 0.25

_DN0 = 0
_GLA0 = 4 * DN_DIM + 4 * DN_HEADS
_CONV0 = _GLA0 + 2 * GLA_KDIM + 2 * GLA_VDIM + 2 * GLA_GATE_RANK
_FN0 = _CONV0 + 2 * CONV_DIM
_GATE0 = _FN0 + FN_DIM
IN_DIM = _GATE0 + N_BRANCH * D_MODEL

MAIN_GATE, MAIN_DN, MAIN_CONV, MAIN_FN = 0, 4096, 6144, 7168
MAIN_COLS = 7680
HALO = 16
VMEM_LIMIT = 56 * 1024 * 1024


def _cp(sem, vmem=None):
    return pltpu.CompilerParams(dimension_semantics=sem, vmem_limit_bytes=vmem or VMEM_LIMIT)


def _sigmoid(x):
    return 1.0 / (1.0 + jnp.exp(-x))


def _silu(x):
    return x * _sigmoid(x)


def _softplus(x):
    return jnp.maximum(x, 0.0) + jnp.log(1.0 + jnp.exp(-jnp.abs(x)))


def _ln(x):
    mu = jnp.mean(x, axis=-1, keepdims=True)
    xc = x - mu
    var = jnp.mean(xc * xc, axis=-1, keepdims=True)
    return xc * lax.rsqrt(var + EPS)


def _dot(a, b):
    return jnp.dot(a.astype(BF16), b.astype(BF16), preferred_element_type=F32)


def _dot_nt(a, b):
    return lax.dot_general(a.astype(BF16), b.astype(BF16), (((1,), (1,)), ((), ())), preferred_element_type=F32)


def _dot_tn(a, b):
    return lax.dot_general(a.astype(BF16), b.astype(BF16), (((0,), (0,)), ((), ())), preferred_element_type=F32)


def _dot_hi(a, b):
    return jnp.dot(a, b, precision=HI, preferred_element_type=F32)


def _mod_kernel(c_ref, w_ref, b_ref, o_ref):
    o_ref[...] = _dot_hi(_silu(c_ref[...]), w_ref[...]) + b_ref[...]


def adaln_vectors(c8, w_mod, b_mod):
    depth, d, n = w_mod.shape
    tn = 1536
    return pl.pallas_call(
        _mod_kernel,
        grid=(depth, n // tn),
        in_specs=[pl.BlockSpec((8, d), lambda l, j: (0, 0)),
                  pl.BlockSpec((None, d, tn), lambda l, j: (l, 0, j)),
                  pl.BlockSpec((None, 1, tn), lambda l, j: (l, 0, j))],
        out_specs=pl.BlockSpec((None, 8, tn), lambda l, j: (l, 0, j)),
        out_shape=jax.ShapeDtypeStruct((depth, 8, n), F32),
        compiler_params=_cp(("parallel", "parallel")),
        name="adaln_vectors",
    )(c8, w_mod, b_mod.reshape(depth, 1, n))


def _lnmod_kernel(h_ref, mod_ref, o_ref):
    u = _ln(h_ref[...]) * (1.0 + mod_ref[1:2, :]) + mod_ref[0:1, :]
    o_ref[...] = u.astype(o_ref.dtype)


def ln_modulate(h, mod):
    b, l, d = h.shape
    tm = min(l, 512)
    return pl.pallas_call(
        _lnmod_kernel,
        grid=(b, l // tm),
        in_specs=[pl.BlockSpec((None, tm, d), lambda i, j: (i, j, 0)),
                  pl.BlockSpec((None, 8, d), lambda i, j: (i, 0, 0))],
        out_specs=pl.BlockSpec((None, tm, d), lambda i, j: (i, j, 0)),
        out_shape=jax.ShapeDtypeStruct((b, l, d), BF16),
        compiler_params=_cp(("parallel", "parallel")),
        name="ln_modulate",
    )(h, mod)


def _mm_kernel(x_ref, w_ref, b_ref, o_ref):
    o_ref[...] = (jnp.dot(x_ref[...], w_ref[...], preferred_element_type=F32) + b_ref[...]).astype(o_ref.dtype)


def matmul_bias(x, w, b, out_dtype, tn):
    m, k = x.shape
    n = w.shape[1]
    tm = 1024 if m % 1024 == 0 else 512
    return pl.pallas_call(
        _mm_kernel,
        grid=(m // tm, n // tn),
        in_specs=[pl.BlockSpec((tm, k), lambda i, j: (i, 0)),
                  pl.BlockSpec((k, tn), lambda i, j: (0, j)),
                  pl.BlockSpec((1, tn), lambda i, j: (0, j))],
        out_specs=pl.BlockSpec((tm, tn), lambda i, j: (i, j)),
        out_shape=jax.ShapeDtypeStruct((m, n), out_dtype),
        compiler_params=_cp(("parallel", "parallel")),
        name="matmul_bias",
    )(x, w, b.reshape(1, n).astype(F32))


def _conv_rows(g_scr, w_ref, width, r0, rs):
    pad = (width - 1) // 2
    acc = None
    for k in range(width):
        term = w_ref[k:k + 1, :] * g_scr[pl.ds(r0 + HALO + k - pad, rs), :]
        acc = term if acc is None else acc + term
    return acc


def _conformer_kernel(vc, vp, vn, gc, gp, gn, w_ref, cb_ref, lg_ref, lb_ref, o_ref, g_scr, *, tl, rs):
    i = pl.program_id(1)
    nt = pl.num_programs(1)
    glu = lambda v, g: v[...].astype(F32) * _sigmoid(g[...].astype(F32))
    g_scr[HALO:HALO + tl, :] = glu(vc, gc)
    g_scr[0:HALO, :] = jnp.where(i > 0, glu(vp, gp), 0.0)
    g_scr[HALO + tl:2 * HALO + tl, :] = jnp.where(i < nt - 1, glu(vn, gn), 0.0)
    for s in range(tl // rs):
        y = _conv_rows(g_scr, w_ref, CONV_WIDTH, s * rs, rs) + cb_ref[...]
        y = _silu(_ln(y) * lg_ref[...] + lb_ref[...])
        o_ref[s * rs:(s + 1) * rs, :] = y.astype(o_ref.dtype)


def conformer_conv(p_main, conv_w, conv_b, ln_g, ln_b):
    b, l, _ = p_main.shape
    c = CONV_DIM
    tl = min(l, 512)
    rs = 64
    hb = tl // HALO
    nhb = l // HALO
    vblk, gblk = MAIN_CONV // c, MAIN_CONV // c + 1
    cur = lambda cb: pl.BlockSpec((None, tl, c), lambda bi, i: (bi, i, cb))
    prv = lambda cb: pl.BlockSpec((None, HALO, c), lambda bi, i: (bi, jnp.maximum(i * hb - 1, 0), cb))
    nxt = lambda cb: pl.BlockSpec((None, HALO, c), lambda bi, i: (bi, jnp.minimum((i + 1) * hb, nhb - 1), cb))
    vec = pl.BlockSpec((1, c), lambda bi, i: (0, 0))
    return pl.pallas_call(
        functools.partial(_conformer_kernel, tl=tl, rs=rs),
        grid=(b, l // tl),
        in_specs=[cur(vblk), prv(vblk), nxt(vblk), cur(gblk), prv(gblk), nxt(gblk),
                  pl.BlockSpec((CONV_WIDTH, c), lambda bi, i: (0, 0)), vec, vec, vec],
        out_specs=pl.BlockSpec((None, tl, c), lambda bi, i: (bi, i, 0)),
        out_shape=jax.ShapeDtypeStruct((b, l, c), BF16),
        scratch_shapes=[pltpu.VMEM((tl + 2 * HALO, c), F32)],
        compiler_params=_cp(("parallel", "parallel")),
        name="conformer_conv",
    )(p_main, p_main, p_main, p_main, p_main, p_main, conv_w,
      conv_b.reshape(1, c), ln_g.reshape(1, c), ln_b.reshape(1, c))


def _shortconv_kernel(xc, xp, xn, w_ref, o_ref, g_scr, *, tl, rs):
    i = pl.program_id(1)
    nt = pl.num_programs(1)
    g_scr[HALO:HALO + tl, :] = xc[...].astype(F32)
    g_scr[0:HALO, :] = jnp.where(i > 0, xp[...].astype(F32), 0.0)
    g_scr[HALO + tl:2 * HALO + tl, :] = jnp.where(i < nt - 1, xn[...].astype(F32), 0.0)
    is_qk = pl.program_id(2) < 2
    for s in range(tl // rs):
        y = _silu(_conv_rows(g_scr, w_ref, DN_CONV_WIDTH, s * rs, rs))
        for h in range(DN_HEADS):
            yh = y[:, h * DN_HEAD_DIM:(h + 1) * DN_HEAD_DIM]
            yh = jnp.where(is_qk, _l2n(yh), yh)
            o_ref[s * rs:(s + 1) * rs, h * DN_HEAD_DIM:(h + 1) * DN_HEAD_DIM] = yh.astype(o_ref.dtype)


def deltanet_shortconv(p_main, dn_conv_w):
    b, l, _ = p_main.shape
    c = 512
    tl = min(l, 512)
    rs = 64
    hb = tl // HALO
    nhb = l // HALO
    cb0 = MAIN_DN // c
    return pl.pallas_call(
        functools.partial(_shortconv_kernel, tl=tl, rs=rs),
        grid=(b, l // tl, 3),
        in_specs=[pl.BlockSpec((None, tl, c), lambda bi, i, j: (bi, i, cb0 + j)),
                  pl.BlockSpec((None, HALO, c), lambda bi, i, j: (bi, jnp.maximum(i * hb - 1, 0), cb0 + j)),
                  pl.BlockSpec((None, HALO, c), lambda bi, i, j: (bi, jnp.minimum((i + 1) * hb, nhb - 1), cb0 + j)),
                  pl.BlockSpec((DN_CONV_WIDTH, c), lambda bi, i, j: (0, j))],
        out_specs=pl.BlockSpec((None, tl, c), lambda bi, i, j: (bi, i, j)),
        out_shape=jax.ShapeDtypeStruct((b, l, 3 * c), BF16),
        scratch_shapes=[pltpu.VMEM((tl + 2 * HALO, c), F32)],
        compiler_params=_cp(("parallel", "parallel", "parallel")),
        name="deltanet_shortconv",
    )(p_main, p_main, p_main, dn_conv_w)


def _chunk_masks(rev):
    r = lax.broadcasted_iota(jnp.int32, (CHUNK, CHUNK), 0)
    c = lax.broadcasted_iota(jnp.int32, (CHUNK, CHUNK), 1)
    if rev:
        cum, sx, incl, strict = c >= r, r < c, c >= r, c > r
    else:
        cum, sx, incl, strict = c <= r, r > c, c <= r, c < r
    return cum.astype(F32), sx.astype(F32), incl, strict, (r == c).astype(F32)


def _dn_chunks(chains):
    each = lambda f: [f(c) for c in chains]
    scale = DN_HEAD_DIM ** -0.5
    decay = each(lambda c: jnp.where(c["incl"], jnp.exp(c["gs"] - c["gs_row"]), 0.0))
    kb = each(lambda c: c["k"] * c["beta"])
    m = [jnp.where(c["strict"], _dot_nt(kbi, c["k"]) * di, 0.0) for c, kbi, di in zip(chains, kb, decay)]
    inv = [c["eye"] - mi for c, mi in zip(chains, m)]
    p = m
    for _ in range(int(math.log2(CHUNK)) - 1):
        p = [_dot(pi, pi) for pi in p]
        inv = [ii + _dot(ii, pi) for ii, pi in zip(inv, p)]
    u = [_dot(ii, c["v"] * c["beta"]) for c, ii in zip(chains, inv)]
    w = [_dot(ii, kbi * c["eg"]) for c, ii, kbi in zip(chains, inv, kb)]
    a = [_dot_nt(c["q"] * scale, c["k"]) * di for c, di in zip(chains, decay)]
    v_new = [ui - _dot(wi, c["s"]) for c, ui, wi in zip(chains, u, w)]
    o = [_dot(c["q"] * scale * c["eg"], c["s"]) + _dot(ai, vi) for c, ai, vi in zip(chains, a, v_new)]
    s_new = [c["s"] * c["etot"] + _dot_tn(c["k"] * c["egt"], vi) for c, vi in zip(chains, v_new)]
    return list(zip(o, s_new))


def _l2n(t):
    return t * lax.rsqrt(jnp.sum(t * t, axis=-1, keepdims=True) + EPS)


def _dn_scan_kernel(xf_ref, bgf_ref, xb_ref, bgb_ref, par_ref, s0_ref, of_ref, ob_ref, sfin_ref, *s_scr, nb):
    i = pl.program_id(0)
    n = pl.num_programs(0)
    hd = DN_HEAD_DIM
    chain = lambda b, d, h: s_scr[(b * 2 + d) * DN_HEADS + h]

    @pl.when(i == 0)
    def _():
        for b in range(nb):
            for d in range(2):
                for h in range(DN_HEADS):
                    chain(b, d, h)[...] = s0_ref[b, d, h]

    chains, sinks = [], []
    for d, (x_ref, bg_ref, o_ref) in enumerate(((xf_ref, bgf_ref, of_ref), (xb_ref, bgb_ref, ob_ref))):
        cum, _, incl, strict, eye = _chunk_masks(rev=bool(d))
        for b in range(nb):
            bg = bg_ref[b]
            beta_all = _sigmoid(bg)
            g_all = -jnp.exp(par_ref[0:1, :]) * _softplus(bg + par_ref[1:2, :])
            gs_all = _dot_hi(cum, g_all)
            tot = gs_all[0:1, :] if d else gs_all[CHUNK - 1:CHUNK, :]
            eg_all, egt_all, etot_all = jnp.exp(gs_all), jnp.exp(tot - gs_all), jnp.exp(tot)
            gs_t = jnp.concatenate([gs_all, jnp.zeros_like(gs_all)], axis=0).T
            for h in range(DN_HEADS):
                cb = d * DN_HEADS + h
                col = 2 * DN_HEADS + cb
                chains.append(dict(
                    q=x_ref[b, :, h * hd:(h + 1) * hd].astype(F32),
                    k=x_ref[b, :, DN_DIM + h * hd:DN_DIM + (h + 1) * hd].astype(F32),
                    v=x_ref[b, :, 2 * DN_DIM + h * hd:2 * DN_DIM + (h + 1) * hd].astype(F32),
                    beta=beta_all[:, cb:cb + 1], gs=gs_all[:, col:col + 1], eg=eg_all[:, col:col + 1],
                    egt=egt_all[:, col:col + 1], etot=etot_all[:, col:col + 1], gs_row=gs_t[col:col + 1, :CHUNK],
                    s=chain(b, d, h)[...], incl=incl, strict=strict, eye=eye))
                sinks.append((o_ref, b, h, chain(b, d, h)))
    for (o_ref, b, h, s_ref), (o, s_new) in zip(sinks, _dn_chunks(chains)):
        o_ref[b, :, h * hd:(h + 1) * hd] = o
        s_ref[...] = s_new

    @pl.when(i == n - 1)
    def _():
        for b in range(nb):
            for d in range(2):
                for h in range(DN_HEADS):
                    sfin_ref[b, d, h] = chain(b, d, h)[...]


def deltanet_scan(qkv, bg, par, s0):
    b, l, _ = qkv.shape
    n = l // CHUNK
    fwd = lambda w: pl.BlockSpec((b, CHUNK, w), lambda i: (0, i, 0))
    bwd = lambda w: pl.BlockSpec((b, CHUNK, w), lambda i: (0, n - 1 - i, 0))
    st = pl.BlockSpec(s0.shape, lambda i: (0, 0, 0, 0, 0))
    return pl.pallas_call(
        functools.partial(_dn_scan_kernel, nb=b),
        grid=(n,),
        in_specs=[fwd(3 * DN_DIM), fwd(128), bwd(3 * DN_DIM), bwd(128),
                  pl.BlockSpec((8, 128), lambda i: (0, 0)), st],
        out_specs=[fwd(DN_DIM), bwd(DN_DIM), st],
        out_shape=[jax.ShapeDtypeStruct((b, l, DN_DIM), F32), jax.ShapeDtypeStruct((b, l, DN_DIM), F32),
                   jax.ShapeDtypeStruct(s0.shape, F32)],
        scratch_shapes=[pltpu.VMEM((DN_HEAD_DIM, DN_HEAD_DIM), F32)] * (b * 2 * DN_HEADS),
        compiler_params=_cp(("arbitrary",)),
        name="deltanet_scan",
    )(qkv, bg, qkv, bg, par, s0)


GLA_SUB = 16


def _gla_chunks(chains):
    row = lax.broadcasted_iota(jnp.int32, (CHUNK, 1), 0)
    lane = lax.broadcasted_iota(jnp.int32, (1, 2 * GLA_DK), 1)
    o_inter = [_dot_nt(c["qp"] * jnp.exp(c["bp"]), c["st"]) for c in chains]
    blocks = [([], []) for _ in chains]
    for blk in range(CHUNK // GLA_SUB):
        i0 = blk * GLA_SUB
        mid = i0 + GLA_SUB // 2
        ri = lax.broadcasted_iota(jnp.int32, (GLA_SUB, CHUNK), 0) + i0
        ci = lax.broadcasted_iota(jnp.int32, (GLA_SUB, CHUNK), 1)
        for c, blks in zip(chains, blocks):
            bp = c["bp"]
            ref = bp[mid:mid + 1, :]
            qt = c["qp"][i0:i0 + GLA_SUB, :] * jnp.exp(bp[i0:i0 + GLA_SUB, :] - ref)
            valid = (row >= i0) if c["rev"] else (row < i0 + GLA_SUB)
            kt = c["kp"] * jnp.exp(jnp.where(valid, ref - bp, 0.0))
            causal = (ci >= ri) if c["rev"] else (ci <= ri)
            for hh in range(2):
                qh = jnp.where((lane >= hh * GLA_DK) & (lane < (hh + 1) * GLA_DK), qt, 0.0)
                blks[hh].append(jnp.where(causal, _dot_nt(qh, kt), 0.0))
    o_intra = [[_dot(jnp.concatenate(blks[hh], axis=0), c["vp"][:, hh * GLA_DV:(hh + 1) * GLA_DV]) for hh in range(2)]
               for c, blks in zip(chains, blocks)]
    srow = lax.broadcasted_iota(jnp.int32, (2 * GLA_DV, 2 * GLA_DK), 0)
    scol = lax.broadcasted_iota(jnp.int32, (2 * GLA_DV, 2 * GLA_DK), 1)
    bd = (srow < GLA_DV) == (scol < GLA_DK)
    out = []
    for c, oi, ox in zip(chains, o_inter, o_intra):
        bp = c["bp"]
        b_last = bp[0:1, :] if c["rev"] else bp[CHUNK - 1:CHUNK, :]
        st_new = jnp.where(bd, c["st"] * jnp.exp(b_last) + _dot_tn(c["vp"], c["kp"] * jnp.exp(b_last - bp)), 0.0)
        out.append((oi + jnp.concatenate(ox, axis=1), st_new))
    return out


def _gla_scan_kernel(qkf, vf, lrf, qkb, vb, lrb, w2_ref, b2_ref, s0_ref, of_ref, ob_ref, sfin_ref, *s_scr, nb):
    i = pl.program_id(0)
    n = pl.num_programs(0)
    npair = GLA_HEADS // 2
    chain = lambda b, d, p: s_scr[(b * 2 + d) * npair + p]

    @pl.when(i == 0)
    def _():
        for b in range(nb):
            for d in range(2):
                for p in range(npair):
                    chain(b, d, p)[...] = s0_ref[b, d, p]

    chains, sinks = [], []
    zs = [[_dot_hi(lr_ref[b], w2_ref[d]) + b2_ref[d] for b in range(nb)]
          for d, lr_ref in enumerate((lrf, lrb))]
    for d, (qk_ref, v_ref, o_ref) in enumerate(((qkf, vf, of_ref), (qkb, vb, ob_ref))):
        cum = _chunk_masks(rev=bool(d))[0]
        for b in range(nb):
            bs = _dot_hi(cum, -_softplus(-zs[d][b]) * (1.0 / GLA_TAU))
            for p in range(npair):
                lo = p * 2 * GLA_DK
                chains.append(dict(
                    qp=qk_ref[b, :, lo:lo + 2 * GLA_DK].astype(F32) * (GLA_DK ** -0.5),
                    kp=qk_ref[b, :, GLA_KDIM + lo:GLA_KDIM + lo + 2 * GLA_DK].astype(F32),
                    vp=v_ref[b, :, p * 2 * GLA_DV:(p + 1) * 2 * GLA_DV].astype(F32),
                    bp=bs[:, lo:lo + 2 * GLA_DK], st=chain(b, d, p)[...], rev=bool(d)))
                sinks.append((o_ref, b, p, chain(b, d, p)))
    for (o_ref, b, p, s_ref), (o, st_new) in zip(sinks, _gla_chunks(chains)):
        o_ref[b, :, p * 2 * GLA_DV:(p + 1) * 2 * GLA_DV] = o
        s_ref[...] = st_new

    @pl.when(i == n - 1)
    def _():
        for b in range(nb):
            for d in range(2):
                for p in range(npair):
                    sfin_ref[b, d, p] = chain(b, d, p)[...]


def gla_scan(p_gla, lr, w2p, b2, s0):
    b, l, _ = p_gla.shape
    n = l // CHUNK
    fwd = lambda w, cb: pl.BlockSpec((b, CHUNK, w), lambda i: (0, i, cb))
    bwd = lambda w, cb: pl.BlockSpec((b, CHUNK, w), lambda i: (0, n - 1 - i, cb))
    st = pl.BlockSpec(s0.shape, lambda i: (0, 0, 0, 0, 0))
    return pl.pallas_call(
        functools.partial(_gla_scan_kernel, nb=b),
        grid=(n,),
        in_specs=[fwd(512, 0), fwd(512, 1), fwd(128, 0), bwd(512, 0), bwd(512, 1), bwd(128, 0),
                  pl.BlockSpec((2, 128, GLA_KDIM), lambda i: (0, 0, 0)),
                  pl.BlockSpec((2, 1, GLA_KDIM), lambda i: (0, 0, 0)), st],
        out_specs=[fwd(GLA_VDIM, 0), bwd(GLA_VDIM, 0), st],
        out_shape=[jax.ShapeDtypeStruct((b, l, GLA_VDIM), F32), jax.ShapeDtypeStruct((b, l, GLA_VDIM), F32),
                   jax.ShapeDtypeStruct(s0.shape, F32)],
        scratch_shapes=[pltpu.VMEM((2 * GLA_DV, 2 * GLA_DK), F32)] * (b * 2 * (GLA_HEADS // 2)),
        compiler_params=_cp(("arbitrary",)),
        name="gla_scan",
    )(p_gla, p_gla, lr, p_gla, p_gla, lr, w2p, b2, s0)


def _headnorm_kernel(of_ref, ob_ref, z_ref, g_ref, o_ref):
    o = of_ref[...] + ob_ref[...]
    z = z_ref[...].astype(F32)
    for h in range(4):
        sl = slice(h * 128, (h + 1) * 128)
        oh = o[:, sl]
        oh = oh * lax.rsqrt(jnp.mean(oh * oh, axis=-1, keepdims=True) + EPS) * g_ref[...]
        o_ref[:, sl] = (oh * _silu(z[:, sl])).astype(o_ref.dtype)


def gated_head_norm(o_f, o_b, z_arr, z_blk, g):
    b, l, c = o_f.shape
    tl = min(l, 512)
    blk = pl.BlockSpec((None, tl, c), lambda bi, i: (bi, i, 0))
    return pl.pallas_call(
        _headnorm_kernel,
        grid=(b, l // tl),
        in_specs=[blk, blk, pl.BlockSpec((None, tl, c), lambda bi, i: (bi, i, z_blk)),
                  pl.BlockSpec((1, 128), lambda bi, i: (0, 0))],
        out_specs=blk,
        out_shape=jax.ShapeDtypeStruct((b, l, c), BF16),
        compiler_params=_cp(("parallel", "parallel")),
        name="gated_head_norm",
    )(o_f, o_b, z_arr, g.reshape(1, 128).astype(F32))


def _dft_tables(n):
    ang = 2.0 * np.pi * (np.outer(np.arange(n), np.arange(n)) % n) / n
    return np.cos(ang), np.sin(ang)


def _fnet_small_kernel(x_ref, wc_ref, cl_ref, sl_ref, o_ref):
    y = jnp.dot(x_ref[...], wc_ref[...], preferred_element_type=F32)
    out = _dot(cl_ref[...], y[:, :FN_GROUP_DIM]) + _dot(sl_ref[...], y[:, FN_GROUP_DIM:])
    o_ref[...] = out.astype(o_ref.dtype)


def _fnet_big_kernel(x_ref, wc_ref, f1_ref, twc_ref, tws_ref, f2_ref, o_ref, y_scr, yi_scr, b_scr, *, n1):
    n2 = FN_GROUP_DIM
    l = n1 * n2
    rb = 512

    def step0(r, c):
        r0 = pl.multiple_of(r * rb, rb)
        y = jnp.dot(x_ref[pl.ds(r0, rb), :], wc_ref[...], preferred_element_type=F32)
        y_scr[pl.ds(r0, rb), :] = y[:, :n2]
        yi_scr[pl.ds(r0, rb), :] = y[:, n2:]
        return c

    lax.fori_loop(0, l // rb, step0, 0)

    def step1(j, c):
        mr = _dot(f1_ref[...], y_scr[pl.ds(j, n1, stride=n2), :])
        mi = _dot(f1_ref[...], yi_scr[pl.ds(j, n1, stride=n2), :])
        ar = mr[:n1] + mi[n1:]
        ai = mi[:n1] - mr[n1:]
        tc, ts = twc_ref[j], tws_ref[j]
        b_scr[pl.ds(j, n1, stride=2 * n2), :] = ar * tc + ai * ts
        b_scr[pl.ds(n2 + j, n1, stride=2 * n2), :] = ai * tc - ar * ts
        return c

    lax.fori_loop(0, n2, step1, 0, unroll=8)

    def step2(k1, c):
        bk = b_scr[pl.ds(pl.multiple_of(k1 * 2 * n2, 2 * n2), 2 * n2), :]
        y_scr[pl.ds(k1, n2, stride=n1), :] = _dot(f2_ref[...], bk)
        return c

    lax.fori_loop(0, n1, step2, 0, unroll=8)

    def step3(r, c):
        r0 = pl.multiple_of(r * rb, rb)
        o_ref[pl.ds(r0, rb), :] = y_scr[pl.ds(r0, rb), :].astype(o_ref.dtype)
        return c

    lax.fori_loop(0, l // rb, step3, 0)


def fourier_mix(p_main):
    b, l, _ = p_main.shape
    gd = FN_GROUP_DIM
    cc, sc = _dft_tables(gd)
    wc = jnp.asarray(np.concatenate([cc, -sc], axis=1) / math.sqrt(gd), BF16)
    x_spec = pl.BlockSpec((None, l, gd), lambda bi, g: (bi, 0, MAIN_FN // gd + g))
    o_spec = pl.BlockSpec((None, l, gd), lambda bi, g: (bi, 0, g))
    full = lambda shape: pl.BlockSpec(shape, lambda bi, g: (0,) * len(shape))
    out_shape = jax.ShapeDtypeStruct((b, l, FN_DIM), BF16)
    if l <= 512:
        cl, sl = _dft_tables(l)
        scale = 1.0 / math.sqrt(l)
        return pl.pallas_call(
            _fnet_small_kernel, grid=(b, FN_GROUPS),
            in_specs=[x_spec, full((gd, 2 * gd)), full((l, l)), full((l, l))],
            out_specs=o_spec, out_shape=out_shape,
            compiler_params=_cp(("parallel", "parallel")), name="fourier_mix_small",
        )(p_main, wc, jnp.asarray(cl * scale, BF16), jnp.asarray(sl * scale, BF16))
    n1, n2 = l // gd, gd
    c1, s1 = _dft_tables(n1)
    c2, s2 = _dft_tables(n2)
    f1 = jnp.asarray(np.concatenate([c1, s1], axis=0), BF16)
    f2 = jnp.asarray(np.concatenate([c2, s2], axis=1) / math.sqrt(l), BF16)
    ang = 2.0 * np.pi * np.outer(np.arange(n2), np.arange(n1)) / l
    twc = jnp.asarray(np.broadcast_to(np.cos(ang)[:, :, None], (n2, n1, gd)), F32)
    tws = jnp.asarray(np.broadcast_to(np.sin(ang)[:, :, None], (n2, n1, gd)), F32)
    return pl.pallas_call(
        functools.partial(_fnet_big_kernel, n1=n1), grid=(b, FN_GROUPS),
        in_specs=[x_spec, full((gd, 2 * gd)), full((2 * n1, n1)), full((n2, n1, gd)), full((n2, n1, gd)),
                  full((n2, 2 * n2))],
        out_specs=o_spec, out_shape=out_shape,
        scratch_shapes=[pltpu.VMEM((l, gd), F32), pltpu.VMEM((l, gd), F32), pltpu.VMEM((n1 * 2 * n2, gd), F32)],
        compiler_params=_cp(("parallel", "parallel")), name="fourier_mix_big",
    )(p_main, wc, f1, twc, tws, f2)


def _merge_kernel(fa, fb, fc, fd, gt_ref, wb_ref, wo_ref, h_ref, mod_ref, lng_ref, lnb_ref, wr_ref,
                  h1_ref, v_ref, lg_ref):
    acc = None
    for n, f_ref in enumerate((fa, fb, fc, fd)):
        proj = jnp.dot(f_ref[...], wb_ref[n], preferred_element_type=F32)
        term = _sigmoid(gt_ref[:, n * D_MODEL:(n + 1) * D_MODEL].astype(F32)) * proj
        acc = term if acc is None else acc + term
    y = _dot(acc, wo_ref[...])
    h1 = _ln(DEEPNORM_ALPHA * h_ref[...] + mod_ref[2:3, :] * y) * lng_ref[...] + lnb_ref[...]
    h1_ref[...] = h1
    v = _ln(h1) * (1.0 + mod_ref[4:5, :]) + mod_ref[3:4, :]
    v_ref[...] = v.astype(v_ref.dtype)
    lg_ref[...] = lax.dot_general(wr_ref[...], v, (((1,), (1,)), ((), ())), precision=HI,
                                  preferred_element_type=F32)


def merge_branches(feats, p_main, w_branch, w_o, h, mod, ln_g, ln_b, w_router_t):
    b, l, d = h.shape
    tm = 256
    fblk = pl.BlockSpec((None, tm, BRANCH_DIM), lambda bi, i: (bi, i, 0))
    hblk = pl.BlockSpec((None, tm, d), lambda bi, i: (bi, i, 0))
    vec = pl.BlockSpec((1, d), lambda bi, i: (0, 0))
    return pl.pallas_call(
        _merge_kernel,
        grid=(b, l // tm),
        in_specs=[fblk, fblk, fblk, fblk,
                  pl.BlockSpec((None, tm, N_BRANCH * d), lambda bi, i: (bi, i, MAIN_GATE // (N_BRANCH * d))),
                  pl.BlockSpec((N_BRANCH, BRANCH_DIM, d), lambda bi, i: (0, 0, 0)),
                  pl.BlockSpec((d, d), lambda bi, i: (0, 0)),
                  hblk, pl.BlockSpec((None, 8, d), lambda bi, i: (bi, 0, 0)), vec, vec,
                  pl.BlockSpec((N_EXPERTS, d), lambda bi, i: (0, 0))],
        out_specs=[hblk, hblk, pl.BlockSpec((None, N_EXPERTS, tm), lambda bi, i: (bi, 0, i))],
        out_shape=[jax.ShapeDtypeStruct((b, l, d), F32), jax.ShapeDtypeStruct((b, l, d), BF16),
                   jax.ShapeDtypeStruct((b, N_EXPERTS, l), F32)],
        compiler_params=_cp(("parallel", "parallel")),
        name="merge_branches",
    )(*feats, p_main, w_branch, w_o, h, mod, ln_g.reshape(1, d), ln_b.reshape(1, d), w_router_t)


def _first_argmax(x, axis, size):
    m = jnp.max(x, axis=axis, keepdims=True)
    idx = lax.broadcasted_iota(jnp.int32, x.shape, axis)
    first = jnp.min(jnp.where(x == m, idx, size), axis=axis, keepdims=True)
    return m, idx == first


def _route_kernel(lg_ref, rb_ref, o_ref):
    tm = lg_ref.shape[-1]
    per = N_EXPERTS // N_EXPERT_GROUPS
    scores = _sigmoid(lg_ref[...])
    biased = scores + rb_ref[...]
    x3 = biased.reshape(N_EXPERT_GROUPS, per, tm)
    m1, hit = _first_argmax(x3, 1, per)
    m2 = jnp.max(jnp.where(hit, -jnp.inf, x3), axis=1, keepdims=True)
    gscore = (m1 + m2).reshape(N_EXPERT_GROUPS, tm)
    gsel = jnp.zeros(gscore.shape, F32)
    for _ in range(TOP_GROUPS):
        _, hit = _first_argmax(gscore, 0, N_EXPERT_GROUPS)
        gsel = jnp.where(hit, 1.0, gsel)
        gscore = jnp.where(hit, -jnp.inf, gscore)
    masked = jnp.where(gsel.reshape(N_EXPERT_GROUPS, 1, tm) > 0.0, x3, -jnp.inf).reshape(N_EXPERTS, tm)
    sel = jnp.zeros(masked.shape, F32)
    for _ in range(TOP_K):
        _, hit = _first_argmax(masked, 0, N_EXPERTS)
        sel = jnp.where(hit, 1.0, sel)
        masked = jnp.where(hit, -jnp.inf, masked)
    w = sel * scores
    o_ref[...] = w / jnp.sum(w, axis=0, keepdims=True) * ROUTED_SCALE


def moe_route(logits_t, b_router):
    b, e, l = logits_t.shape
    tm = 256
    blk = pl.BlockSpec((None, e, tm), lambda bi, i: (bi, 0, i))
    return pl.pallas_call(
        _route_kernel, grid=(b, l // tm),
        in_specs=[blk, pl.BlockSpec((e, tm), lambda bi, i: (0, 0))],
        out_specs=blk, out_shape=jax.ShapeDtypeStruct((b, e, l), F32),
        compiler_params=_cp(("parallel", "parallel")), name="moe_route",
    )(logits_t, jnp.broadcast_to(b_router.astype(F32)[:, None], (e, tm)))


MOE_EB = 2


def _moe_kernel(x_ref, comb_ref, wg_ref, wu_ref, wd_ref, sg_ref, su_ref, sd_ref, h_ref, mod_ref, nmod_ref,
                lng_ref, lnb_ref, h2_ref, u_ref, acc):
    j = pl.program_id(2)
    nj = pl.num_programs(2)
    x = x_ref[...]

    @pl.when(j == 0)
    def _():
        hid = _silu(_dot(x, sg_ref[...])) * _dot(x, su_ref[...])
        acc[...] = _dot(hid, sd_ref[...])

    comb = comb_ref[...]
    lane = lax.broadcasted_iota(jnp.int32, (1, comb.shape[1]), 1)
    for e in range(MOE_EB):
        wcol = jnp.sum(jnp.where(lane == j * MOE_EB + e, comb, 0.0), axis=-1, keepdims=True)
        hid = _silu(_dot(x, wg_ref[e])) * _dot(x, wu_ref[e]) * wcol
        acc[...] += _dot(hid, wd_ref[e])

    @pl.when(j == nj - 1)
    def _():
        h2 = _ln(DEEPNORM_ALPHA * h_ref[...] + mod_ref[5:6, :] * acc[...]) * lng_ref[...] + lnb_ref[...]
        h2_ref[...] = h2
        u_ref[...] = (_ln(h2) * (1.0 + nmod_ref[1:2, :]) + nmod_ref[0:1, :]).astype(u_ref.dtype)


def moe_dense(v, comb, w_gate, w_up, w_down, ws_gate, ws_up, ws_down, h1, mod, next_mod, ln_g, ln_b):
    b, l, d = h1.shape
    tm = min(l, 1024)
    ne = w_gate.shape[0]
    xblk = pl.BlockSpec((None, tm, d), lambda bi, i, j: (bi, i, 0))
    vec = pl.BlockSpec((1, d), lambda bi, i, j: (0, 0))
    modblk = pl.BlockSpec((None, 8, d), lambda bi, i, j: (bi, 0, 0))
    return pl.pallas_call(
        _moe_kernel,
        grid=(b, l // tm, ne // MOE_EB),
        in_specs=[xblk, pl.BlockSpec((None, tm, 128), lambda bi, i, j: (bi, i, 0)),
                  pl.BlockSpec((MOE_EB, d, EXPERT_DIM), lambda bi, i, j: (j, 0, 0)),
                  pl.BlockSpec((MOE_EB, d, EXPERT_DIM), lambda bi, i, j: (j, 0, 0)),
                  pl.BlockSpec((MOE_EB, EXPERT_DIM, d), lambda bi, i, j: (j, 0, 0)),
                  pl.BlockSpec((d, EXPERT_DIM), lambda bi, i, j: (0, 0)),
                  pl.BlockSpec((d, EXPERT_DIM), lambda bi, i, j: (0, 0)),
                  pl.BlockSpec((EXPERT_DIM, d), lambda bi, i, j: (0, 0)),
                  xblk, modblk, modblk, vec, vec],
        out_specs=[xblk, xblk],
        out_shape=[jax.ShapeDtypeStruct((b, l, d), F32), jax.ShapeDtypeStruct((b, l, d), BF16)],
        scratch_shapes=[pltpu.VMEM((tm, d), F32)],
        compiler_params=_cp(("parallel", "parallel", "arbitrary")),
        name="moe_dense",
    )(v, comb, w_gate, w_up, w_down, ws_gate, ws_up, ws_down, h1, mod, next_mod,
      ln_g.reshape(1, d), ln_b.reshape(1, d))


def _to_col_major(t):
    b, l, ch = t.shape
    return t.reshape(b, l // GRID_W, GRID_W, ch).transpose(0, 2, 1, 3).reshape(b, l, ch)


def _from_col_major(t):
    b, l, ch = t.shape
    return t.reshape(b, GRID_W, l // GRID_W, ch).transpose(0, 2, 1, 3).reshape(b, l, ch)


def _layer_weights(l, w_in, b_in, dn_a_log, dn_dt_bias, gla_w2, gla_b2):
    w, bvec = w_in[l], b_in[l]
    cols = lambda a, n: (w[:, a:a + n], bvec[a:a + n])
    parts = (cols(_GATE0, 4096), cols(_DN0, 2048), cols(_CONV0, 1024), cols(_FN0, 512))
    w_main = jnp.concatenate([q[0] for q in parts], axis=1).astype(BF16)
    b_main = jnp.concatenate([q[1] for q in parts])
    pad = lambda wb, n: (jnp.pad(wb[0], ((0, 0), (0, 128 - n))).astype(BF16), jnp.pad(wb[1], (0, 128 - n)))
    w_bg, b_bg = pad(cols(2048, 16), 16)
    w_gla, b_gla = cols(_GLA0, 1536)
    w_lr, b_lr = pad(cols(_GLA0 + 1536, 32), 32)
    par = jnp.zeros((8, 128), F32)
    par = par.at[0, 8:16].set(dn_a_log[l].reshape(-1)).at[1, 8:16].set(dn_dt_bias[l].reshape(-1))
    w2p = jnp.zeros((2, 128, GLA_KDIM), F32)
    w2p = w2p.at[0, 0:16].set(gla_w2[l, 0]).at[1, 16:32].set(gla_w2[l, 1])
    return dict(w_main=w_main, b_main=b_main, w_bg=w_bg, b_bg=b_bg, w_gla=w_gla.astype(BF16), b_gla=b_gla,
                w_lr=w_lr, b_lr=b_lr, par=par, w2p=w2p, b2=gla_b2[l].reshape(2, 1, GLA_KDIM).astype(F32))


def _project(u, lw, col_major):
    b, l, d = u.shape
    flat = u.reshape(b * l, d)
    ug = _to_col_major(u).reshape(b * l, d) if col_major else flat
    p_main = matmul_bias(flat, lw["w_main"], lw["b_main"], BF16, 512).reshape(b, l, MAIN_COLS)
    bg = matmul_bias(flat, lw["w_bg"], lw["b_bg"], F32, 128).reshape(b, l, 128)
    p_gla = matmul_bias(ug, lw["w_gla"], lw["b_gla"], BF16, 512).reshape(b, l, 1536)
    lr = matmul_bias(ug, lw["w_lr"], lw["b_lr"], F32, 128).reshape(b, l, 128)
    return p_main, bg, p_gla, lr


def kernel(x, c, ctx, c_ctx, w_mod, b_mod, w_in, b_in, dn_conv_w, dn_a_log, dn_dt_bias, dn_norm_g, gla_w2, gla_b2, gla_norm_g, conv_w, conv_b, conv_ln_g, conv_ln_b, w_branch, w_o, ln_g, ln_b, w_router, b_router, w_gate, w_up, w_down, ws_gate, ws_up, ws_down):
    batch, seq, d = x.shape
    c8 = jnp.zeros((8, d), F32).at[:batch].set(c).at[batch].set(c_ctx)
    mods = adaln_vectors(c8, w_mod, b_mod).reshape(DEPTH, 8, 6, d)
    zrow = jnp.zeros((DEPTH, 2, d), F32)
    mod_x = [jnp.concatenate([mods[l, :batch], jnp.broadcast_to(zrow[l][None], (batch, 2, d))], axis=1)
             for l in range(DEPTH)]
    mod_c = [jnp.broadcast_to(jnp.concatenate([mods[l, batch], zrow[l]], axis=0)[None], (batch, 8, d))
             for l in range(DEPTH)]

    h, hc = x, ctx
    u_x, u_c = ln_modulate(h, mod_x[0]), ln_modulate(hc, mod_c[0])
    for l in range(DEPTH):
        lw = _layer_weights(l, w_in, b_in, dn_a_log, dn_dt_bias, gla_w2, gla_b2)
        px = _project(u_x, lw, True)
        pc = _project(u_c, lw, False)
        qkv_c, qkv_x = deltanet_shortconv(pc[0], dn_conv_w[l]), deltanet_shortconv(px[0], dn_conv_w[l])
        s0 = jnp.zeros((batch, 2, DN_HEADS, DN_HEAD_DIM, DN_HEAD_DIM), F32)
        ocf, ocb, s_c = deltanet_scan(qkv_c, pc[1], lw["par"], s0)
        oxf, oxb, _ = deltanet_scan(qkv_x, px[1], lw["par"], s_c)
        dn_x = gated_head_norm(oxf, oxb, px[0], MAIN_DN // 512 + 3, dn_norm_g[l])
        dn_c = gated_head_norm(ocf, ocb, pc[0], MAIN_DN // 512 + 3, dn_norm_g[l])
        g0 = jnp.zeros((batch, 2, 2, 2 * GLA_DV, 2 * GLA_DK), F32)
        gcf, gcb, gs_c = gla_scan(pc[2], pc[3], lw["w2p"], lw["b2"], g0)
        gxf, gxb, _ = gla_scan(px[2], px[3], lw["w2p"], lw["b2"], gs_c)
        gla_x = _from_col_major(gated_head_norm(gxf, gxb, px[2], 2, gla_norm_g[l]))
        gla_c = gated_head_norm(gcf, gcb, pc[2], 2, gla_norm_g[l])
        wb = w_branch[l].astype(BF16)
        wo = w_o[l].astype(BF16)
        wrt = w_router[l].T
        nxt = min(l + 1, DEPTH - 1)
        outs = []
        for (p, dn_f, gla_f, hh, mod, nmod) in ((px, dn_x, gla_x, h, mod_x[l], mod_x[nxt]),
                                               (pc, dn_c, gla_c, hc, mod_c[l], mod_c[nxt])):
            feats = (conformer_conv(p[0], conv_w[l], conv_b[l], conv_ln_g[l], conv_ln_b[l]), dn_f,
                     fourier_mix(p[0]), gla_f)
            h1, v, lg = merge_branches(feats, p[0], wb, wo, hh, mod, ln_g[l, 0], ln_b[l, 0], wrt)
            comb_t = moe_route(lg, b_router[l])
            comb = jnp.pad(comb_t.transpose(0, 2, 1), ((0, 0), (0, 0), (0, 128 - N_EXPERTS)))
            outs.append(moe_dense(v, comb, w_gate[l], w_up[l], w_down[l], ws_gate[l], ws_up[l], ws_down[l],
                                  h1, mod, nmod, ln_g[l, 1], ln_b[l, 1]))
        (h, u_x), (hc, u_c) = outs
    return h
```

```python
import functools
import math

import jax
import jax.numpy as jnp
import numpy as np
from jax import lax
from jax.experimental import pallas as pl
from jax.experimental.pallas import tpu as pltpu
from jax.experimental.pallas import tpu_sc as plsc

F32 = jnp.float32
BF16 = jnp.bfloat16
HI = lax.Precision.HIGHEST

D_MODEL = 1024
DEPTH = 4
GRID_W = 64
CHUNK = 64
EPS = 1e-6
CONV_DIM = 512
CONV_WIDTH = 31
DN_HEADS = 4
DN_HEAD_DIM = 128
DN_DIM = 512
DN_CONV_WIDTH = 5
FN_GROUPS = 4
FN_GROUP_DIM = 128
FN_DIM = 512
GLA_HEADS = 4
GLA_DK = 64
GLA_DV = 128
GLA_KDIM = 256
GLA_VDIM = 512
GLA_GATE_RANK = 16
GLA_TAU = 16.0
N_BRANCH = 4
BRANCH_DIM = 512
N_EXPERTS = 64
N_EXPERT_GROUPS = 8
TOP_GROUPS = 4
TOP_K = 6
EXPERT_DIM = 256
ROUTED_SCALE = 2.5
DEEPNORM_ALPHA = (2 * DEPTH) ** 0.25

_DN0 = 0
_GLA0 = 4 * DN_DIM + 4 * DN_HEADS
_CONV0 = _GLA0 + 2 * GLA_KDIM + 2 * GLA_VDIM + 2 * GLA_GATE_RANK
_FN0 = _CONV0 + 2 * CONV_DIM
_GATE0 = _FN0 + FN_DIM
IN_DIM = _GATE0 + N_BRANCH * D_MODEL

MAIN_GATE, MAIN_DN, MAIN_CONV, MAIN_FN = 0, 4096, 6144, 7168
MAIN_COLS = 7680
HALO = 16
VMEM_LIMIT = 56 * 1024 * 1024


def _cp(sem, vmem=None):
    return pltpu.CompilerParams(dimension_semantics=sem, vmem_limit_bytes=vmem or VMEM_LIMIT)


def _sigmoid(x):
    return 1.0 / (1.0 + jnp.exp(-x))


def _silu(x):
    return x * _sigmoid(x)


def _softplus(x):
    return jnp.maximum(x, 0.0) + jnp.log(1.0 + jnp.exp(-jnp.abs(x)))


def _ln(x):
    mu = jnp.mean(x, axis=-1, keepdims=True)
    xc = x - mu
    var = jnp.mean(xc * xc, axis=-1, keepdims=True)
    return xc * lax.rsqrt(var + EPS)


def _dot(a, b):
    return jnp.dot(a.astype(BF16), b.astype(BF16), preferred_element_type=F32)


def _dot_nt(a, b):
    return lax.dot_general(a.astype(BF16), b.astype(BF16), (((1,), (1,)), ((), ())), preferred_element_type=F32)


def _dot_tn(a, b):
    return lax.dot_general(a.astype(BF16), b.astype(BF16), (((0,), (0,)), ((), ())), preferred_element_type=F32)


def _dot_hi(a, b):
    return jnp.dot(a, b, precision=HI, preferred_element_type=F32)


_HI16 = -65536


def _pack_bf16_pair(x):
    n = x.shape[-1] // 2
    bits = lambda t: lax.bitcast_convert_type(t.astype(BF16).astype(F32), jnp.int32)
    return (bits(x[:, n:]) & _HI16) | ((bits(x[:, :n]) >> 16) & 0xFFFF)


def _unpack_bf16_pair(w):
    lo = lax.bitcast_convert_type(w << 16, F32)
    hi = lax.bitcast_convert_type(w & _HI16, F32)
    return lo, hi


def _mod_kernel(c_ref, w_ref, b_ref, o_ref):
    o_ref[...] = _dot_hi(_silu(c_ref[...]), w_ref[...]) + b_ref[...]


def adaln_vectors(c8, w_mod, b_mod):
    depth, d, n = w_mod.shape
    tn = 1536
    return pl.pallas_call(
        _mod_kernel,
        grid=(depth, n // tn),
        in_specs=[pl.BlockSpec((8, d), lambda l, j: (0, 0)),
                  pl.BlockSpec((None, d, tn), lambda l, j: (l, 0, j)),
                  pl.BlockSpec((None, 1, tn), lambda l, j: (l, 0, j))],
        out_specs=pl.BlockSpec((None, 8, tn), lambda l, j: (l, 0, j)),
        out_shape=jax.ShapeDtypeStruct((depth, 8, n), F32),
        compiler_params=_cp(("parallel", "parallel")),
        name="adaln_vectors",
    )(c8, w_mod, b_mod.reshape(depth, 1, n))


def _lnmod_kernel(h_ref, mod_ref, o_ref):
    u = _ln(h_ref[...]) * (1.0 + mod_ref[1:2, :]) + mod_ref[0:1, :]
    o_ref[...] = u.astype(o_ref.dtype)


def ln_modulate(h, mod):
    b, l, d = h.shape
    tm = min(l, 512)
    return pl.pallas_call(
        _lnmod_kernel,
        grid=(b, l // tm),
        in_specs=[pl.BlockSpec((None, tm, d), lambda i, j: (i, j, 0)),
                  pl.BlockSpec((None, 8, d), lambda i, j: (i, 0, 0))],
        out_specs=pl.BlockSpec((None, tm, d), lambda i, j: (i, j, 0)),
        out_shape=jax.ShapeDtypeStruct((b, l, d), BF16),
        compiler_params=_cp(("parallel", "parallel")),
        name="ln_modulate",
    )(h, mod)


def _mm_kernel(x_ref, w_ref, b_ref, o_ref):
    o_ref[...] = (jnp.dot(x_ref[...], w_ref[...], preferred_element_type=F32) + b_ref[...]).astype(o_ref.dtype)


def matmul_bias(x, w, b, out_dtype, tn):
    m, k = x.shape
    n = w.shape[1]
    tm = 1024 if m % 1024 == 0 else 512
    return pl.pallas_call(
        _mm_kernel,
        grid=(m // tm, n // tn),
        in_specs=[pl.BlockSpec((tm, k), lambda i, j: (i, 0)),
                  pl.BlockSpec((k, tn), lambda i, j: (0, j)),
                  pl.BlockSpec((1, tn), lambda i, j: (0, j))],
        out_specs=pl.BlockSpec((tm, tn), lambda i, j: (i, j)),
        out_shape=jax.ShapeDtypeStruct((m, n), out_dtype),
        compiler_params=_cp(("parallel", "parallel")),
        name="matmul_bias",
    )(x, w, b.reshape(1, n).astype(F32))


def _conv_rows(g_scr, w_ref, width, r0, rs):
    pad = (width - 1) // 2
    acc = None
    for k in range(width):
        term = w_ref[k:k + 1, :] * g_scr[pl.ds(r0 + HALO + k - pad, rs), :]
        acc = term if acc is None else acc + term
    return acc


def _conformer_kernel(vc, vp, vn, gc, gp, gn, w_ref, cb_ref, lg_ref, lb_ref, o_ref, g_scr, *, tl, rs):
    i = pl.program_id(1)
    nt = pl.num_programs(1)
    glu = lambda v, g: v[...].astype(F32) * _sigmoid(g[...].astype(F32))
    g_scr[HALO:HALO + tl, :] = glu(vc, gc)
    g_scr[0:HALO, :] = jnp.where(i > 0, glu(vp, gp), 0.0)
    g_scr[HALO + tl:2 * HALO + tl, :] = jnp.where(i < nt - 1, glu(vn, gn), 0.0)
    for s in range(tl // rs):
        y = _conv_rows(g_scr, w_ref, CONV_WIDTH, s * rs, rs) + cb_ref[...]
        y = _silu(_ln(y) * lg_ref[...] + lb_ref[...])
        o_ref[s * rs:(s + 1) * rs, :] = y.astype(o_ref.dtype)


def conformer_conv(p_main, conv_w, conv_b, ln_g, ln_b):
    b, l, _ = p_main.shape
    c = CONV_DIM
    tl = min(l, 512)
    rs = 64
    hb = tl // HALO
    nhb = l // HALO
    vblk, gblk = MAIN_CONV // c, MAIN_CONV // c + 1
    cur = lambda cb: pl.BlockSpec((None, tl, c), lambda bi, i: (bi, i, cb))
    prv = lambda cb: pl.BlockSpec((None, HALO, c), lambda bi, i: (bi, jnp.maximum(i * hb - 1, 0), cb))
    nxt = lambda cb: pl.BlockSpec((None, HALO, c), lambda bi, i: (bi, jnp.minimum((i + 1) * hb, nhb - 1), cb))
    vec = pl.BlockSpec((1, c), lambda bi, i: (0, 0))
    return pl.pallas_call(
        functools.partial(_conformer_kernel, tl=tl, rs=rs),
        grid=(b, l // tl),
        in_specs=[cur(vblk), prv(vblk), nxt(vblk), cur(gblk), prv(gblk), nxt(gblk),
                  pl.BlockSpec((CONV_WIDTH, c), lambda bi, i: (0, 0)), vec, vec, vec],
        out_specs=pl.BlockSpec((None, tl, c), lambda bi, i: (bi, i, 0)),
        out_shape=jax.ShapeDtypeStruct((b, l, c), BF16),
        scratch_shapes=[pltpu.VMEM((tl + 2 * HALO, c), F32)],
        compiler_params=_cp(("parallel", "parallel")),
        name="conformer_conv",
    )(p_main, p_main, p_main, p_main, p_main, p_main, conv_w,
      conv_b.reshape(1, c), ln_g.reshape(1, c), ln_b.reshape(1, c))


def _shortconv_kernel(xc, xp, xn, w_ref, o_ref, g_scr, *, tl, rs):
    i = pl.program_id(1)
    nt = pl.num_programs(1)
    g_scr[HALO:HALO + tl, :] = xc[...].astype(F32)
    g_scr[0:HALO, :] = jnp.where(i > 0, xp[...].astype(F32), 0.0)
    g_scr[HALO + tl:2 * HALO + tl, :] = jnp.where(i < nt - 1, xn[...].astype(F32), 0.0)
    is_qk = pl.program_id(2) < 2
    for s in range(tl // rs):
        y = _silu(_conv_rows(g_scr, w_ref, DN_CONV_WIDTH, s * rs, rs))
        for h in range(DN_HEADS):
            yh = y[:, h * DN_HEAD_DIM:(h + 1) * DN_HEAD_DIM]
            yh = jnp.where(is_qk, _l2n(yh), yh)
            o_ref[s * rs:(s + 1) * rs, h * DN_HEAD_DIM:(h + 1) * DN_HEAD_DIM] = yh.astype(o_ref.dtype)


def deltanet_shortconv(p_main, dn_conv_w):
    b, l, _ = p_main.shape
    c = 512
    tl = min(l, 512)
    rs = 64
    hb = tl // HALO
    nhb = l // HALO
    cb0 = MAIN_DN // c
    return pl.pallas_call(
        functools.partial(_shortconv_kernel, tl=tl, rs=rs),
        grid=(b, l // tl, 3),
        in_specs=[pl.BlockSpec((None, tl, c), lambda bi, i, j: (bi, i, cb0 + j)),
                  pl.BlockSpec((None, HALO, c), lambda bi, i, j: (bi, jnp.maximum(i * hb - 1, 0), cb0 + j)),
                  pl.BlockSpec((None, HALO, c), lambda bi, i, j: (bi, jnp.minimum((i + 1) * hb, nhb - 1), cb0 + j)),
                  pl.BlockSpec((DN_CONV_WIDTH, c), lambda bi, i, j: (0, j))],
        out_specs=pl.BlockSpec((None, tl, c), lambda bi, i, j: (bi, i, j)),
        out_shape=jax.ShapeDtypeStruct((b, l, 3 * c), BF16),
        scratch_shapes=[pltpu.VMEM((tl + 2 * HALO, c), F32)],
        compiler_params=_cp(("parallel", "parallel", "parallel")),
        name="deltanet_shortconv",
    )(p_main, p_main, p_main, dn_conv_w)


def _chunk_masks(rev):
    r = lax.broadcasted_iota(jnp.int32, (CHUNK, CHUNK), 0)
    c = lax.broadcasted_iota(jnp.int32, (CHUNK, CHUNK), 1)
    if rev:
        cum, sx, incl, strict = c >= r, r < c, c >= r, c > r
    else:
        cum, sx, incl, strict = c <= r, r > c, c <= r, c < r
    return cum.astype(F32), sx.astype(F32), incl, strict, (r == c).astype(F32)


def _dn_chunks(chains):
    each = lambda f: [f(c) for c in chains]
    scale = DN_HEAD_DIM ** -0.5
    decay = each(lambda c: jnp.where(c["incl"], jnp.exp(c["gs"] - c["gs_row"]), 0.0))
    kb = each(lambda c: c["k"] * c["beta"])
    m = [jnp.where(c["strict"], _dot_nt(kbi, c["k"]) * di, 0.0) for c, kbi, di in zip(chains, kb, decay)]
    inv = [c["eye"] - mi for c, mi in zip(chains, m)]
    p = m
    for _ in range(int(math.log2(CHUNK)) - 1):
        p = [_dot(pi, pi) for pi in p]
        inv = [ii + _dot(ii, pi) for ii, pi in zip(inv, p)]
    u = [_dot(ii, c["v"] * c["beta"]) for c, ii in zip(chains, inv)]
    w = [_dot(ii, kbi * c["eg"]) for c, ii, kbi in zip(chains, inv, kb)]
    a = [_dot_nt(c["q"] * scale, c["k"]) * di for c, di in zip(chains, decay)]
    v_new = [ui - _dot(wi, c["s"]) for c, ui, wi in zip(chains, u, w)]
    o = [_dot(c["q"] * scale * c["eg"], c["s"]) + _dot(ai, vi) for c, ai, vi in zip(chains, a, v_new)]
    s_new = [c["s"] * c["etot"] + _dot_tn(c["k"] * c["egt"], vi) for c, vi in zip(chains, v_new)]
    return list(zip(o, s_new))


def _l2n(t):
    return t * lax.rsqrt(jnp.sum(t * t, axis=-1, keepdims=True) + EPS)


def _dn_scan_kernel(xf_ref, bgf_ref, xb_ref, bgb_ref, par_ref, s0_ref, of_ref, ob_ref, sfin_ref, *s_scr, nb):
    i = pl.program_id(0)
    n = pl.num_programs(0)
    hd = DN_HEAD_DIM
    chain = lambda b, d, h: s_scr[(b * 2 + d) * DN_HEADS + h]

    @pl.when(i == 0)
    def _():
        for b in range(nb):
            for d in range(2):
                for h in range(DN_HEADS):
                    chain(b, d, h)[...] = s0_ref[b, d, h]

    chains, sinks = [], []
    for d, (x_ref, bg_ref, o_ref) in enumerate(((xf_ref, bgf_ref, of_ref), (xb_ref, bgb_ref, ob_ref))):
        cum, _, incl, strict, eye = _chunk_masks(rev=bool(d))
        for b in range(nb):
            bg = bg_ref[b]
            beta_all = _sigmoid(bg)
            g_all = -jnp.exp(par_ref[0:1, :]) * _softplus(bg + par_ref[1:2, :])
            gs_all = _dot_hi(cum, g_all)
            tot = gs_all[0:1, :] if d else gs_all[CHUNK - 1:CHUNK, :]
            eg_all, egt_all, etot_all = jnp.exp(gs_all), jnp.exp(tot - gs_all), jnp.exp(tot)
            gs_t = jnp.concatenate([gs_all, jnp.zeros_like(gs_all)], axis=0).T
            for h in range(DN_HEADS):
                cb = d * DN_HEADS + h
                col = 2 * DN_HEADS + cb
                chains.append(dict(
                    q=x_ref[b, :, h * hd:(h + 1) * hd].astype(F32),
                    k=x_ref[b, :, DN_DIM + h * hd:DN_DIM + (h + 1) * hd].astype(F32),
                    v=x_ref[b, :, 2 * DN_DIM + h * hd:2 * DN_DIM + (h + 1) * hd].astype(F32),
                    beta=beta_all[:, cb:cb + 1], gs=gs_all[:, col:col + 1], eg=eg_all[:, col:col + 1],
                    egt=egt_all[:, col:col + 1], etot=etot_all[:, col:col + 1], gs_row=gs_t[col:col + 1, :CHUNK],
                    s=chain(b, d, h)[...], incl=incl, strict=strict, eye=eye))
                sinks.append((o_ref, b, h, chain(b, d, h)))
    for (o_ref, b, h, s_ref), (o, s_new) in zip(sinks, _dn_chunks(chains)):
        o_ref[b, :, h * hd:(h + 1) * hd] = o
        s_ref[...] = s_new

    @pl.when(i == n - 1)
    def _():
        for b in range(nb):
            for d in range(2):
                for h in range(DN_HEADS):
                    sfin_ref[b, d, h] = chain(b, d, h)[...]


def deltanet_scan(qkv, bg, par, s0):
    b, l, _ = qkv.shape
    n = l // CHUNK
    fwd = lambda w: pl.BlockSpec((b, CHUNK, w), lambda i: (0, i, 0))
    bwd = lambda w: pl.BlockSpec((b, CHUNK, w), lambda i: (0, n - 1 - i, 0))
    st = pl.BlockSpec(s0.shape, lambda i: (0, 0, 0, 0, 0))
    return pl.pallas_call(
        functools.partial(_dn_scan_kernel, nb=b),
        grid=(n,),
        in_specs=[fwd(3 * DN_DIM), fwd(128), bwd(3 * DN_DIM), bwd(128),
                  pl.BlockSpec((8, 128), lambda i: (0, 0)), st],
        out_specs=[fwd(DN_DIM), bwd(DN_DIM), st],
        out_shape=[jax.ShapeDtypeStruct((b, l, DN_DIM), F32), jax.ShapeDtypeStruct((b, l, DN_DIM), F32),
                   jax.ShapeDtypeStruct(s0.shape, F32)],
        scratch_shapes=[pltpu.VMEM((DN_HEAD_DIM, DN_HEAD_DIM), F32)] * (b * 2 * DN_HEADS),
        compiler_params=_cp(("arbitrary",)),
        name="deltanet_scan",
    )(qkv, bg, qkv, bg, par, s0)


GLA_SUB = 16


def _gla_chunks(chains):
    row = lax.broadcasted_iota(jnp.int32, (CHUNK, 1), 0)
    lane = lax.broadcasted_iota(jnp.int32, (1, 2 * GLA_DK), 1)
    o_inter = [_dot_nt(c["qp"] * jnp.exp(c["bp"]), c["st"]) for c in chains]
    blocks = [([], []) for _ in chains]
    for blk in range(CHUNK // GLA_SUB):
        i0 = blk * GLA_SUB
        mid = i0 + GLA_SUB // 2
        ri = lax.broadcasted_iota(jnp.int32, (GLA_SUB, CHUNK), 0) + i0
        ci = lax.broadcasted_iota(jnp.int32, (GLA_SUB, CHUNK), 1)
        for c, blks in zip(chains, blocks):
            bp = c["bp"]
            ref = bp[mid:mid + 1, :]
            qt = c["qp"][i0:i0 + GLA_SUB, :] * jnp.exp(bp[i0:i0 + GLA_SUB, :] - ref)
            valid = (row >= i0) if c["rev"] else (row < i0 + GLA_SUB)
            kt = c["kp"] * jnp.exp(jnp.where(valid, ref - bp, 0.0))
            causal = (ci >= ri) if c["rev"] else (ci <= ri)
            for hh in range(2):
                qh = jnp.where((lane >= hh * GLA_DK) & (lane < (hh + 1) * GLA_DK), qt, 0.0)
                blks[hh].append(jnp.where(causal, _dot_nt(qh, kt), 0.0))
    o_intra = [[_dot(jnp.concatenate(blks[hh], axis=0), c["vp"][:, hh * GLA_DV:(hh + 1) * GLA_DV]) for hh in range(2)]
               for c, blks in zip(chains, blocks)]
    srow = lax.broadcasted_iota(jnp.int32, (2 * GLA_DV, 2 * GLA_DK), 0)
    scol = lax.broadcasted_iota(jnp.int32, (2 * GLA_DV, 2 * GLA_DK), 1)
    bd = (srow < GLA_DV) == (scol < GLA_DK)
    out = []
    for c, oi, ox in zip(chains, o_inter, o_intra):
        bp = c["bp"]
        b_last = bp[0:1, :] if c["rev"] else bp[CHUNK - 1:CHUNK, :]
        st_new = jnp.where(bd, c["st"] * jnp.exp(b_last) + _dot_tn(c["vp"], c["kp"] * jnp.exp(b_last - bp)), 0.0)
        out.append((oi + jnp.concatenate(ox, axis=1), st_new))
    return out


def _gla_scan_kernel(qkf, vf, lrf, qkb, vb, lrb, w2_ref, b2_ref, s0_ref, of_ref, ob_ref, sfin_ref, *s_scr, nb):
    i = pl.program_id(0)
    n = pl.num_programs(0)
    npair = GLA_HEADS // 2
    chain = lambda b, d, p: s_scr[(b * 2 + d) * npair + p]

    @pl.when(i == 0)
    def _():
        for b in range(nb):
            for d in range(2):
                for p in range(npair):
                    chain(b, d, p)[...] = s0_ref[b, d, p]

    chains, sinks = [], []
    zs = [[_dot_hi(lr_ref[b], w2_ref[d]) + b2_ref[d] for b in range(nb)]
          for d, lr_ref in enumerate((lrf, lrb))]
    for d, (qk_ref, v_ref, o_ref) in enumerate(((qkf, vf, of_ref), (qkb, vb, ob_ref))):
        cum = _chunk_masks(rev=bool(d))[0]
        for b in range(nb):
            bs = _dot_hi(cum, -_softplus(-zs[d][b]) * (1.0 / GLA_TAU))
            for p in range(npair):
                lo = p * 2 * GLA_DK
                chains.append(dict(
                    qp=qk_ref[b, :, lo:lo + 2 * GLA_DK].astype(F32) * (GLA_DK ** -0.5),
                    kp=qk_ref[b, :, GLA_KDIM + lo:GLA_KDIM + lo + 2 * GLA_DK].astype(F32),
                    vp=v_ref[b, :, p * 2 * GLA_DV:(p + 1) * 2 * GLA_DV].astype(F32),
                    bp=bs[:, lo:lo + 2 * GLA_DK], st=chain(b, d, p)[...], rev=bool(d)))
                sinks.append((o_ref, b, p, chain(b, d, p)))
    for (o_ref, b, p, s_ref), (o, st_new) in zip(sinks, _gla_chunks(chains)):
        o_ref[b, :, p * 2 * GLA_DV:(p + 1) * 2 * GLA_DV] = o
        s_ref[...] = st_new

    @pl.when(i == n - 1)
    def _():
        for b in range(nb):
            for d in range(2):
                for p in range(npair):
                    sfin_ref[b, d, p] = chain(b, d, p)[...]


def gla_scan(p_gla, lr, w2p, b2, s0):
    b, l, _ = p_gla.shape
    n = l // CHUNK
    fwd = lambda w, cb: pl.BlockSpec((b, CHUNK, w), lambda i: (0, i, cb))
    bwd = lambda w, cb: pl.BlockSpec((b, CHUNK, w), lambda i: (0, n - 1 - i, cb))
    st = pl.BlockSpec(s0.shape, lambda i: (0, 0, 0, 0, 0))
    return pl.pallas_call(
        functools.partial(_gla_scan_kernel, nb=b),
        grid=(n,),
        in_specs=[fwd(512, 0), fwd(512, 1), fwd(128, 0), bwd(512, 0), bwd(512, 1), bwd(128, 0),
                  pl.BlockSpec((2, 128, GLA_KDIM), lambda i: (0, 0, 0)),
                  pl.BlockSpec((2, 1, GLA_KDIM), lambda i: (0, 0, 0)), st],
        out_specs=[fwd(GLA_VDIM, 0), bwd(GLA_VDIM, 0), st],
        out_shape=[jax.ShapeDtypeStruct((b, l, GLA_VDIM), F32), jax.ShapeDtypeStruct((b, l, GLA_VDIM), F32),
                   jax.ShapeDtypeStruct(s0.shape, F32)],
        scratch_shapes=[pltpu.VMEM((2 * GLA_DV, 2 * GLA_DK), F32)] * (b * 2 * (GLA_HEADS // 2)),
        compiler_params=_cp(("arbitrary",)),
        name="gla_scan",
    )(p_gla, p_gla, lr, p_gla, p_gla, lr, w2p, b2, s0)


def _headnorm_kernel(of_ref, ob_ref, z_ref, g_ref, o_ref):
    o = of_ref[...] + ob_ref[...]
    z = z_ref[...].astype(F32)
    for h in range(4):
        sl = slice(h * 128, (h + 1) * 128)
        oh = o[:, sl]
        oh = oh * lax.rsqrt(jnp.mean(oh * oh, axis=-1, keepdims=True) + EPS) * g_ref[...]
        o_ref[:, sl] = (oh * _silu(z[:, sl])).astype(o_ref.dtype)


def gated_head_norm(o_f, o_b, z_arr, z_blk, g):
    b, l, c = o_f.shape
    tl = min(l, 512)
    blk = pl.BlockSpec((None, tl, c), lambda bi, i: (bi, i, 0))
    return pl.pallas_call(
        _headnorm_kernel,
        grid=(b, l // tl),
        in_specs=[blk, blk, pl.BlockSpec((None, tl, c), lambda bi, i: (bi, i, z_blk)),
                  pl.BlockSpec((1, 128), lambda bi, i: (0, 0))],
        out_specs=blk,
        out_shape=jax.ShapeDtypeStruct((b, l, c), BF16),
        compiler_params=_cp(("parallel", "parallel")),
        name="gated_head_norm",
    )(o_f, o_b, z_arr, g.reshape(1, 128).astype(F32))


def _dft_tables(n):
    ang = 2.0 * np.pi * (np.outer(np.arange(n), np.arange(n)) % n) / n
    return np.cos(ang), np.sin(ang)


def _fnet_small_kernel(x_ref, wc_ref, cl_ref, sl_ref, o_ref):
    y = jnp.dot(x_ref[...], wc_ref[...], preferred_element_type=F32)
    out = _dot(cl_ref[...], y[:, :FN_GROUP_DIM]) + _dot(sl_ref[...], y[:, FN_GROUP_DIM:])
    o_ref[...] = out.astype(o_ref.dtype)


def _fnet_big_kernel(x_ref, wc_ref, f1_ref, twc_ref, tws_ref, f2_ref, o_ref, y_scr, yi_scr, b_scr, *, n1):
    n2 = FN_GROUP_DIM
    l = n1 * n2
    rb = 512

    def step0(r, c):
        r0 = pl.multiple_of(r * rb, rb)
        y = jnp.dot(x_ref[pl.ds(r0, rb), :], wc_ref[...], preferred_element_type=F32)
        y_scr[pl.ds(r0, rb), :] = y[:, :n2]
        yi_scr[pl.ds(r0, rb), :] = y[:, n2:]
        return c

    lax.fori_loop(0, l // rb, step0, 0)

    def step1(j, c):
        mr = _dot(f1_ref[...], y_scr[pl.ds(j, n1, stride=n2), :])
        mi = _dot(f1_ref[...], yi_scr[pl.ds(j, n1, stride=n2), :])
        ar = mr[:n1] + mi[n1:]
        ai = mi[:n1] - mr[n1:]
        tc, ts = twc_ref[j], tws_ref[j]
        b_scr[pl.ds(j, n1, stride=2 * n2), :] = ar * tc + ai * ts
        b_scr[pl.ds(n2 + j, n1, stride=2 * n2), :] = ai * tc - ar * ts
        return c

    lax.fori_loop(0, n2, step1, 0, unroll=8)

    def step2(k1, c):
        bk = b_scr[pl.ds(pl.multiple_of(k1 * 2 * n2, 2 * n2), 2 * n2), :]
        y_scr[pl.ds(k1, n2, stride=n1), :] = _dot(f2_ref[...], bk)
        return c

    lax.fori_loop(0, n1, step2, 0, unroll=8)

    def step3(r, c):
        r0 = pl.multiple_of(r * rb, rb)
        o_ref[pl.ds(r0, rb), :] = y_scr[pl.ds(r0, rb), :].astype(o_ref.dtype)
        return c

    lax.fori_loop(0, l // rb, step3, 0)


def fourier_mix(p_main):
    b, l, _ = p_main.shape
    gd = FN_GROUP_DIM
    cc, sc = _dft_tables(gd)
    wc = jnp.asarray(np.concatenate([cc, -sc], axis=1) / math.sqrt(gd), BF16)
    x_spec = pl.BlockSpec((None, l, gd), lambda bi, g: (bi, 0, MAIN_FN // gd + g))
    o_spec = pl.BlockSpec((None, l, gd), lambda bi, g: (bi, 0, g))
    full = lambda shape: pl.BlockSpec(shape, lambda bi, g: (0,) * len(shape))
    out_shape = jax.ShapeDtypeStruct((b, l, FN_DIM), BF16)
    if l <= 512:
        cl, sl = _dft_tables(l)
        scale = 1.0 / math.sqrt(l)
        return pl.pallas_call(
            _fnet_small_kernel, grid=(b, FN_GROUPS),
            in_specs=[x_spec, full((gd, 2 * gd)), full((l, l)), full((l, l))],
            out_specs=o_spec, out_shape=out_shape,
            compiler_params=_cp(("parallel", "parallel")), name="fourier_mix_small",
        )(p_main, wc, jnp.asarray(cl * scale, BF16), jnp.asarray(sl * scale, BF16))
    n1, n2 = l // gd, gd
    c1, s1 = _dft_tables(n1)
    c2, s2 = _dft_tables(n2)
    f1 = jnp.asarray(np.concatenate([c1, s1], axis=0), BF16)
    f2 = jnp.asarray(np.concatenate([c2, s2], axis=1) / math.sqrt(l), BF16)
    ang = 2.0 * np.pi * np.outer(np.arange(n2), np.arange(n1)) / l
    twc = jnp.asarray(np.broadcast_to(np.cos(ang)[:, :, None], (n2, n1, gd)), F32)
    tws = jnp.asarray(np.broadcast_to(np.sin(ang)[:, :, None], (n2, n1, gd)), F32)
    return pl.pallas_call(
        functools.partial(_fnet_big_kernel, n1=n1), grid=(b, FN_GROUPS),
        in_specs=[x_spec, full((gd, 2 * gd)), full((2 * n1, n1)), full((n2, n1, gd)), full((n2, n1, gd)),
                  full((n2, 2 * n2))],
        out_specs=o_spec, out_shape=out_shape,
        scratch_shapes=[pltpu.VMEM((l, gd), F32), pltpu.VMEM((l, gd), F32), pltpu.VMEM((n1 * 2 * n2, gd), F32)],
        compiler_params=_cp(("parallel", "parallel")), name="fourier_mix_big",
    )(p_main, wc, f1, twc, tws, f2)


def _merge_kernel(fa, fb, fc, fd, gt_ref, wb_ref, wo_ref, h_ref, mod_ref, lng_ref, lnb_ref, wr_ref,
                  h1_ref, v_ref, lg_ref):
    acc = None
    for n, f_ref in enumerate((fa, fb, fc, fd)):
        proj = jnp.dot(f_ref[...], wb_ref[n], preferred_element_type=F32)
        term = _sigmoid(gt_ref[:, n * D_MODEL:(n + 1) * D_MODEL].astype(F32)) * proj
        acc = term if acc is None else acc + term
    y = _dot(acc, wo_ref[...])
    h1 = _ln(DEEPNORM_ALPHA * h_ref[...] + mod_ref[2:3, :] * y) * lng_ref[...] + lnb_ref[...]
    h1_ref[...] = h1
    v = _ln(h1) * (1.0 + mod_ref[4:5, :]) + mod_ref[3:4, :]
    v_ref[...] = _pack_bf16_pair(v)
    lg_ref[...] = lax.dot_general(wr_ref[...], v, (((1,), (1,)), ((), ())), precision=HI,
                                  preferred_element_type=F32)


def merge_branches(feats, p_main, w_branch, w_o, h, mod, ln_g, ln_b, w_router_t):
    b, l, d = h.shape
    tm = 256
    nt = l // tm
    fblk = pl.BlockSpec((None, tm, BRANCH_DIM), lambda bi, i: (bi, i, 0))
    hblk = pl.BlockSpec((None, tm, d), lambda bi, i: (bi, i, 0))
    vec = pl.BlockSpec((1, d), lambda bi, i: (0, 0))
    return pl.pallas_call(
        _merge_kernel,
        grid=(b, l // tm),
        in_specs=[fblk, fblk, fblk, fblk,
                  pl.BlockSpec((None, tm, N_BRANCH * d), lambda bi, i: (bi, i, MAIN_GATE // (N_BRANCH * d))),
                  pl.BlockSpec((N_BRANCH, BRANCH_DIM, d), lambda bi, i: (0, 0, 0)),
                  pl.BlockSpec((d, d), lambda bi, i: (0, 0)),
                  hblk, pl.BlockSpec((None, 8, d), lambda bi, i: (bi, 0, 0)), vec, vec,
                  pl.BlockSpec((N_EXPERTS, d), lambda bi, i: (0, 0))],
        out_specs=[hblk, pl.BlockSpec((None, tm, d // 2), lambda bi, i: (bi, i, 0)),
                   pl.BlockSpec((N_EXPERTS, tm), lambda bi, i: (0, bi * nt + i))],
        out_shape=[jax.ShapeDtypeStruct((b, l, d), F32), jax.ShapeDtypeStruct((b, l, d // 2), jnp.int32),
                   jax.ShapeDtypeStruct((N_EXPERTS, b * l), F32)],
        compiler_params=_cp(("parallel", "parallel")),
        name="merge_branches",
    )(*feats, p_main, w_branch, w_o, h, mod, ln_g.reshape(1, d), ln_b.reshape(1, d), w_router_t)


def _first_argmax(x, axis, size):
    m = jnp.max(x, axis=axis, keepdims=True)
    idx = lax.broadcasted_iota(jnp.int32, x.shape, axis)
    first = jnp.min(jnp.where(x == m, idx, size), axis=axis, keepdims=True)
    return m, idx == first


def _route_kernel(lg_ref, rb_ref, pos_ref, w_ref, cnt_ref, cnt_scr, *, capacity):
    i = pl.program_id(0)

    @pl.when(i == 0)
    def _():
        cnt_scr[...] = jnp.zeros_like(cnt_scr)

    tm = lg_ref.shape[-1]
    per = N_EXPERTS // N_EXPERT_GROUPS
    scores = _sigmoid(lg_ref[...])
    biased = scores + rb_ref[...]
    x3 = biased.reshape(N_EXPERT_GROUPS, per, tm)
    m1, hit = _first_argmax(x3, 1, per)
    m2 = jnp.max(jnp.where(hit, -jnp.inf, x3), axis=1, keepdims=True)
    gscore = (m1 + m2).reshape(N_EXPERT_GROUPS, tm)
    gsel = jnp.zeros(gscore.shape, F32)
    for _ in range(TOP_GROUPS):
        _, hit = _first_argmax(gscore, 0, N_EXPERT_GROUPS)
        gsel = jnp.where(hit, 1.0, gsel)
        gscore = jnp.where(hit, -jnp.inf, gscore)
    masked = jnp.where(gsel.reshape(N_EXPERT_GROUPS, 1, tm) > 0.0, x3, -jnp.inf).reshape(N_EXPERTS, tm)
    sel = jnp.zeros(masked.shape, F32)
    for _ in range(TOP_K):
        _, hit = _first_argmax(masked, 0, N_EXPERTS)
        sel = jnp.where(hit, 1.0, sel)
        masked = jnp.where(hit, -jnp.inf, masked)
    w = sel * scores
    comb = w / jnp.sum(w, axis=0, keepdims=True) * ROUTED_SCALE
    tr = lax.broadcasted_iota(jnp.int32, (tm, tm), 0)
    tc = lax.broadcasted_iota(jnp.int32, (tm, tm), 1)
    rank = cnt_scr[:, 0:1] + _dot(sel, (tr < tc).astype(F32))
    slot = lax.broadcasted_iota(jnp.int32, sel.shape, 0).astype(F32) * float(capacity) + rank
    pos_rows, w_rows = [], []
    remaining = sel
    for _ in range(TOP_K):
        _, hit = _first_argmax(remaining, 0, N_EXPERTS)
        pos_rows.append(jnp.sum(jnp.where(hit, slot, 0.0), axis=0, keepdims=True))
        w_rows.append(jnp.sum(jnp.where(hit, comb, 0.0), axis=0, keepdims=True))
        remaining = jnp.where(hit, 0.0, remaining)
    zero = jnp.zeros((8 - TOP_K, tm), F32)
    pos_ref[...] = jnp.concatenate(pos_rows + [zero], axis=0).astype(jnp.int32)
    w_ref[...] = jnp.concatenate(w_rows + [zero], axis=0)
    cnt_scr[...] = cnt_scr[...] + jnp.sum(sel, axis=1, keepdims=True)

    @pl.when(i == pl.num_programs(0) - 1)
    def _():
        cnt_ref[...] = cnt_scr[...]


def moe_route(logits_t, b_router):
    e, t = logits_t.shape
    tm = 256
    blk = pl.BlockSpec((e, tm), lambda i: (0, i))
    oblk = pl.BlockSpec((8, tm), lambda i: (0, i))
    return pl.pallas_call(
        functools.partial(_route_kernel, capacity=t), grid=(t // tm,),
        in_specs=[blk, pl.BlockSpec((e, tm), lambda i: (0, 0))],
        out_specs=[oblk, oblk, pl.BlockSpec((e, 128), lambda i: (0, 0))],
        out_shape=[jax.ShapeDtypeStruct((8, t), jnp.int32), jax.ShapeDtypeStruct((8, t), F32),
                   jax.ShapeDtypeStruct((e, 128), F32)],
        scratch_shapes=[pltpu.VMEM((e, 128), F32)],
        compiler_params=_cp(("arbitrary",)), name="moe_route",
    )(logits_t, jnp.broadcast_to(b_router.astype(F32)[:, None], (e, tm)))


def _sc_workers():
    info = plsc.get_sparse_core_info()
    return info.num_cores, info.num_subcores


def _sc_chunk(per_worker):
    return max(c for c in range(8, 129, 8) if per_worker % c == 0)


def sc_scatter_rows(rows, pos, n_out):
    t, w = rows.shape
    nc, ns = _sc_workers()
    nw = nc * ns
    per_w = t // nw
    ch = _sc_chunk(per_w)
    nch = per_w // ch
    pos_w = pos[:TOP_K].reshape(TOP_K, nw, nch, ch).transpose(1, 2, 0, 3)
    mesh = plsc.VectorSubcoreMesh(core_axis_name="c", subcore_axis_name="s")

    @functools.partial(
        pl.kernel, mesh=mesh, out_type=jax.ShapeDtypeStruct((n_out, w), jnp.int32),
        scratch_types=[pltpu.VMEM((TOP_K, ch), jnp.int32), pltpu.VMEM((ch, w), jnp.int32), pltpu.SemaphoreType.DMA])
    def scatter(rows_hbm, pos_hbm, out_hbm, idx_v, rows_v, sem):
        wid = lax.axis_index("s") * nc + lax.axis_index("c")

        @pl.loop(0, nch)
        def _(j):
            pltpu.sync_copy(pos_hbm.at[wid, j], idx_v)
            pltpu.sync_copy(rows_hbm.at[pl.ds(wid * per_w + j * ch, ch)], rows_v)
            copies = [pltpu.async_copy(rows_v, out_hbm.at[idx_v.at[k]], sem) for k in range(TOP_K)]
            for cp in copies:
                cp.wait()

    return scatter(rows, pos_w)


def sc_gather_rows(table, idx):
    m = idx.shape[0]
    w = table.shape[1]
    nc, ns = _sc_workers()
    nw = nc * ns
    per_w = m // nw
    ch = _sc_chunk(per_w)
    nch = per_w // ch
    mesh = plsc.VectorSubcoreMesh(core_axis_name="c", subcore_axis_name="s")

    @functools.partial(
        pl.kernel, mesh=mesh, out_type=jax.ShapeDtypeStruct((m, w), jnp.int32),
        scratch_types=[pltpu.VMEM((ch,), jnp.int32), pltpu.VMEM((ch, w), jnp.int32), pltpu.SemaphoreType.DMA])
    def gather(table_hbm, idx_hbm, out_hbm, idx_v, rows_v, sem):
        wid = lax.axis_index("s") * nc + lax.axis_index("c")

        @pl.loop(0, nch)
        def _(j):
            off = wid * per_w + j * ch
            pltpu.sync_copy(idx_hbm.at[pl.ds(off, ch)], idx_v)
            pltpu.async_copy(table_hbm.at[idx_v], rows_v, sem).wait()
            pltpu.sync_copy(rows_v, out_hbm.at[pl.ds(off, ch)])

    return gather(table, idx)


MOE_TM = 256


def _gmm_kernel(te_ref, tj_ref, na_ref, x_ref, wg_ref, wu_ref, wd_ref, y_ref, wg_s, wu_s, wd_s):
    i = pl.program_id(0)

    @pl.when(i < na_ref[0])
    def _():
        @pl.when(tj_ref[i] == 0)
        def _():
            wg_s[...] = wg_ref[...].astype(BF16)
            wu_s[...] = wu_ref[...].astype(BF16)
            wd_s[...] = wd_ref[...].astype(BF16)

        half = D_MODEL // 2
        lo, hi = _unpack_bf16_pair(x_ref[...])
        gate = _dot(lo, wg_s[:half, :]) + _dot(hi, wg_s[half:, :])
        up = _dot(lo, wu_s[:half, :]) + _dot(hi, wu_s[half:, :])
        y_ref[...] = _pack_bf16_pair(_dot(_silu(gate) * up, wd_s[...]))


def moe_grouped_experts(xs, counts, w_gate, w_up, w_down, capacity):
    n_e, d, hdim = w_gate.shape
    blocks_per_e = capacity // MOE_TM
    n_tiles = capacity * TOP_K // MOE_TM + n_e
    tiles_e = (counts.astype(jnp.int32) + MOE_TM - 1) // MOE_TM
    ends = jnp.cumsum(tiles_e)
    n_active = ends[-1]
    step = jnp.minimum(jnp.arange(n_tiles, dtype=jnp.int32), n_active - 1)
    te = jnp.searchsorted(ends, step, side="right").astype(jnp.int32)
    tj = step - (ends[te] - tiles_e[te])
    row_blk = lambda i, te_r, tj_r, na_r: (te_r[i] * blocks_per_e + tj_r[i], 0)
    wmap = lambda i, te_r, tj_r, na_r: (te_r[i], 0, 0)
    return pl.pallas_call(
        _gmm_kernel,
        grid_spec=pltpu.PrefetchScalarGridSpec(
            num_scalar_prefetch=3, grid=(n_tiles,),
            in_specs=[pl.BlockSpec((MOE_TM, d // 2), row_blk),
                      pl.BlockSpec((None, d, hdim), wmap), pl.BlockSpec((None, d, hdim), wmap),
                      pl.BlockSpec((None, hdim, d), wmap)],
            out_specs=pl.BlockSpec((MOE_TM, d // 2), row_blk),
            scratch_shapes=[pltpu.VMEM((d, hdim), BF16), pltpu.VMEM((d, hdim), BF16), pltpu.VMEM((hdim, d), BF16)]),
        out_shape=jax.ShapeDtypeStruct(xs.shape, jnp.int32),
        compiler_params=_cp(("arbitrary",)),
        name="moe_grouped_experts",
    )(te, tj, n_active.reshape(1).astype(jnp.int32), xs, w_gate, w_up, w_down)


def _moe_out_kernel(v_ref, g_ref, w_ref, sg_ref, su_ref, sd_ref, h_ref, mod_ref, nmod_ref, lng_ref, lnb_ref,
                    h2_ref, u_ref):
    half = D_MODEL // 2
    lo, hi = _unpack_bf16_pair(v_ref[...])
    gate = _dot(lo, sg_ref[:half, :]) + _dot(hi, sg_ref[half:, :])
    up = _dot(lo, su_ref[:half, :]) + _dot(hi, su_ref[half:, :])
    f = _dot(_silu(gate) * up, sd_ref[...])
    acc_lo = acc_hi = None
    for k in range(TOP_K):
        ylo, yhi = _unpack_bf16_pair(g_ref[k])
        wk = w_ref[:, k:k + 1]
        acc_lo = ylo * wk if acc_lo is None else acc_lo + ylo * wk
        acc_hi = yhi * wk if acc_hi is None else acc_hi + yhi * wk
    f = f + jnp.concatenate([acc_lo, acc_hi], axis=1)
    h2 = _ln(DEEPNORM_ALPHA * h_ref[...] + mod_ref[5:6, :] * f) * lng_ref[...] + lnb_ref[...]
    h2_ref[...] = h2
    u_ref[...] = (_ln(h2) * (1.0 + nmod_ref[1:2, :]) + nmod_ref[0:1, :]).astype(u_ref.dtype)


def moe_output(vp, gathered, w_tok, ws_gate, ws_up, ws_down, h1, mod, next_mod, ln_g, ln_b):
    b, l, d = h1.shape
    tm = 256
    xblk = pl.BlockSpec((None, tm, d), lambda bi, i: (bi, i, 0))
    pblk = pl.BlockSpec((None, tm, d // 2), lambda bi, i: (bi, i, 0))
    vec = pl.BlockSpec((1, d), lambda bi, i: (0, 0))
    modblk = pl.BlockSpec((None, 8, d), lambda bi, i: (bi, 0, 0))
    return pl.pallas_call(
        _moe_out_kernel,
        grid=(b, l // tm),
        in_specs=[pblk, pl.BlockSpec((TOP_K, None, tm, d // 2), lambda bi, i: (0, bi, i, 0)),
                  pl.BlockSpec((None, tm, 8), lambda bi, i: (bi, i, 0)),
                  pl.BlockSpec((d, EXPERT_DIM), lambda bi, i: (0, 0)),
                  pl.BlockSpec((d, EXPERT_DIM), lambda bi, i: (0, 0)),
                  pl.BlockSpec((EXPERT_DIM, d), lambda bi, i: (0, 0)),
                  xblk, modblk, modblk, vec, vec],
        out_specs=[xblk, xblk],
        out_shape=[jax.ShapeDtypeStruct((b, l, d), F32), jax.ShapeDtypeStruct((b, l, d), BF16)],
        compiler_params=_cp(("parallel", "parallel")),
        name="moe_output",
    )(vp, gathered, w_tok, ws_gate, ws_up, ws_down, h1, mod, next_mod, ln_g.reshape(1, d), ln_b.reshape(1, d))


def moe_layer(vp, logits_t, b_router, w_gate, w_up, w_down, ws_gate, ws_up, ws_down, h1, mod, next_mod, ln_g, ln_b):
    b, l, d = h1.shape
    t = b * l
    pos, w_rows, counts = moe_route(logits_t, b_router)
    xs = sc_scatter_rows(vp.reshape(t, d // 2), pos, N_EXPERTS * t)
    ys = moe_grouped_experts(xs, counts[:, 0], w_gate, w_up, w_down, t)
    gathered = sc_gather_rows(ys, pos[:TOP_K].reshape(-1)).reshape(TOP_K, b, l, d // 2)
    return moe_output(vp, gathered, w_rows.T.reshape(b, l, 8), ws_gate, ws_up, ws_down, h1, mod, next_mod, ln_g, ln_b)


def _to_col_major(t):
    b, l, ch = t.shape
    return t.reshape(b, l // GRID_W, GRID_W, ch).transpose(0, 2, 1, 3).reshape(b, l, ch)


def _from_col_major(t):
    b, l, ch = t.shape
    return t.reshape(b, GRID_W, l // GRID_W, ch).transpose(0, 2, 1, 3).reshape(b, l, ch)


def _layer_weights(l, w_in, b_in, dn_a_log, dn_dt_bias, gla_w2, gla_b2):
    w, bvec = w_in[l], b_in[l]
    cols = lambda a, n: (w[:, a:a + n], bvec[a:a + n])
    parts = (cols(_GATE0, 4096), cols(_DN0, 2048), cols(_CONV0, 1024), cols(_FN0, 512))
    w_main = jnp.concatenate([q[0] for q in parts], axis=1).astype(BF16)
    b_main = jnp.concatenate([q[1] for q in parts])
    pad = lambda wb, n: (jnp.pad(wb[0], ((0, 0), (0, 128 - n))).astype(BF16), jnp.pad(wb[1], (0, 128 - n)))
    w_bg, b_bg = pad(cols(2048, 16), 16)
    w_gla, b_gla = cols(_GLA0, 1536)
    w_lr, b_lr = pad(cols(_GLA0 + 1536, 32), 32)
    par = jnp.zeros((8, 128), F32)
    par = par.at[0, 8:16].set(dn_a_log[l].reshape(-1)).at[1, 8:16].set(dn_dt_bias[l].reshape(-1))
    w2p = jnp.zeros((2, 128, GLA_KDIM), F32)
    w2p = w2p.at[0, 0:16].set(gla_w2[l, 0]).at[1, 16:32].set(gla_w2[l, 1])
    return dict(w_main=w_main, b_main=b_main, w_bg=w_bg, b_bg=b_bg, w_gla=w_gla.astype(BF16), b_gla=b_gla,
                w_lr=w_lr, b_lr=b_lr, par=par, w2p=w2p, b2=gla_b2[l].reshape(2, 1, GLA_KDIM).astype(F32))


def _project(u, lw, col_major):
    b, l, d = u.shape
    flat = u.reshape(b * l, d)
    ug = _to_col_major(u).reshape(b * l, d) if col_major else flat
    p_main = matmul_bias(flat, lw["w_main"], lw["b_main"], BF16, 512).reshape(b, l, MAIN_COLS)
    bg = matmul_bias(flat, lw["w_bg"], lw["b_bg"], F32, 128).reshape(b, l, 128)
    p_gla = matmul_bias(ug, lw["w_gla"], lw["b_gla"], BF16, 512).reshape(b, l, 1536)
    lr = matmul_bias(ug, lw["w_lr"], lw["b_lr"], F32, 128).reshape(b, l, 128)
    return p_main, bg, p_gla, lr


def kernel(x, c, ctx, c_ctx, w_mod, b_mod, w_in, b_in, dn_conv_w, dn_a_log, dn_dt_bias, dn_norm_g, gla_w2, gla_b2, gla_norm_g, conv_w, conv_b, conv_ln_g, conv_ln_b, w_branch, w_o, ln_g, ln_b, w_router, b_router, w_gate, w_up, w_down, ws_gate, ws_up, ws_down):
    batch, seq, d = x.shape
    c8 = jnp.zeros((8, d), F32).at[:batch].set(c).at[batch].set(c_ctx)
    mods = adaln_vectors(c8, w_mod, b_mod).reshape(DEPTH, 8, 6, d)
    zrow = jnp.zeros((DEPTH, 2, d), F32)
    mod_x = [jnp.concatenate([mods[l, :batch], jnp.broadcast_to(zrow[l][None], (batch, 2, d))], axis=1)
             for l in range(DEPTH)]
    mod_c = [jnp.broadcast_to(jnp.concatenate([mods[l, batch], zrow[l]], axis=0)[None], (batch, 8, d))
             for l in range(DEPTH)]

    h, hc = x, ctx
    u_x, u_c = ln_modulate(h, mod_x[0]), ln_modulate(hc, mod_c[0])
    for l in range(DEPTH):
        lw = _layer_weights(l, w_in, b_in, dn_a_log, dn_dt_bias, gla_w2, gla_b2)
        px = _project(u_x, lw, True)
        pc = _project(u_c, lw, False)
        qkv_c, qkv_x = deltanet_shortconv(pc[0], dn_conv_w[l]), deltanet_shortconv(px[0], dn_conv_w[l])
        s0 = jnp.zeros((batch, 2, DN_HEADS, DN_HEAD_DIM, DN_HEAD_DIM), F32)
        ocf, ocb, s_c = deltanet_scan(qkv_c, pc[1], lw["par"], s0)
        oxf, oxb, _ = deltanet_scan(qkv_x, px[1], lw["par"], s_c)
        dn_x = gated_head_norm(oxf, oxb, px[0], MAIN_DN // 512 + 3, dn_norm_g[l])
        dn_c = gated_head_norm(ocf, ocb, pc[0], MAIN_DN // 512 + 3, dn_norm_g[l])
        g0 = jnp.zeros((batch, 2, 2, 2 * GLA_DV, 2 * GLA_DK), F32)
        gcf, gcb, gs_c = gla_scan(pc[2], pc[3], lw["w2p"], lw["b2"], g0)
        gxf, gxb, _ = gla_scan(px[2], px[3], lw["w2p"], lw["b2"], gs_c)
        gla_x = _from_col_major(gated_head_norm(gxf, gxb, px[2], 2, gla_norm_g[l]))
        gla_c = gated_head_norm(gcf, gcb, pc[2], 2, gla_norm_g[l])
        wb = w_branch[l].astype(BF16)
        wo = w_o[l].astype(BF16)
        wrt = w_router[l].T
        nxt = min(l + 1, DEPTH - 1)
        outs = []
        for (p, dn_f, gla_f, hh, mod, nmod) in ((px, dn_x, gla_x, h, mod_x[l], mod_x[nxt]),
                                               (pc, dn_c, gla_c, hc, mod_c[l], mod_c[nxt])):
            feats = (conformer_conv(p[0], conv_w[l], conv_b[l], conv_ln_g[l], conv_ln_b[l]), dn_f,
                     fourier_mix(p[0]), gla_f)
            h1, vp, lg = merge_branches(feats, p[0], wb, wo, hh, mod, ln_g[l, 0], ln_b[l, 0], wrt)
            outs.append(moe_layer(vp, lg, b_router[l], w_gate[l], w_up[l], w_down[l], ws_gate[l], ws_up[l],
                                  ws_down[l], h1, mod, nmod, ln_g[l, 1], ln_b[l, 1]))
        (h, u_x), (hc, u_c) = outs
    return h
```

```python
import functools
import math

import jax
import jax.numpy as jnp
import numpy as np
from jax import lax
from jax.experimental import pallas as pl
from jax.experimental.pallas import tpu as pltpu
from jax.experimental.pallas import tpu_sc as plsc

F32 = jnp.float32
BF16 = jnp.bfloat16
HI = lax.Precision.HIGHEST

D_MODEL = 1024
DEPTH = 4
GRID_W = 64
CHUNK = 64
EPS = 1e-6
CONV_DIM = 512
CONV_WIDTH = 31
DN_HEADS = 4
DN_HEAD_DIM = 128
DN_DIM = 512
DN_CONV_WIDTH = 5
FN_GROUPS = 4
FN_GROUP_DIM = 128
FN_DIM = 512
GLA_HEADS = 4
GLA_DK = 64
GLA_DV = 128
GLA_KDIM = 256
GLA_VDIM = 512
GLA_GATE_RANK = 16
GLA_TAU = 16.0
N_BRANCH = 4
BRANCH_DIM = 512
N_EXPERTS = 64
N_EXPERT_GROUPS = 8
TOP_GROUPS = 4
TOP_K = 6
EXPERT_DIM = 256
ROUTED_SCALE = 2.5
DEEPNORM_ALPHA = (2 * DEPTH) ** 0.25

_DN0 = 0
_GLA0 = 4 * DN_DIM + 4 * DN_HEADS
_CONV0 = _GLA0 + 2 * GLA_KDIM + 2 * GLA_VDIM + 2 * GLA_GATE_RANK
_FN0 = _CONV0 + 2 * CONV_DIM
_GATE0 = _FN0 + FN_DIM
IN_DIM = _GATE0 + N_BRANCH * D_MODEL

MAIN_GATE, MAIN_DN, MAIN_CONV, MAIN_FN = 0, 4096, 6144, 7168
MAIN_COLS = 7680
HALO = 16
VMEM_LIMIT = 56 * 1024 * 1024


def _cp(sem, vmem=None):
    return pltpu.CompilerParams(dimension_semantics=sem, vmem_limit_bytes=vmem or VMEM_LIMIT)


def _sigmoid(x):
    return 0.5 * jnp.tanh(0.5 * x) + 0.5


def _silu(x):
    return x * _sigmoid(x)


def _softplus(x):
    return jnp.maximum(x, 0.0) + jnp.log(1.0 + jnp.exp(-jnp.abs(x)))


def _ln(x):
    mu = jnp.mean(x, axis=-1, keepdims=True)
    xc = x - mu
    var = jnp.mean(xc * xc, axis=-1, keepdims=True)
    return xc * lax.rsqrt(var + EPS)


def _dot(a, b):
    return jnp.dot(a.astype(BF16), b.astype(BF16), preferred_element_type=F32)


def _dot_nt(a, b):
    return lax.dot_general(a.astype(BF16), b.astype(BF16), (((1,), (1,)), ((), ())), preferred_element_type=F32)


def _dot_tn(a, b):
    return lax.dot_general(a.astype(BF16), b.astype(BF16), (((0,), (0,)), ((), ())), preferred_element_type=F32)


def _dot_hi(a, b):
    return jnp.dot(a, b, precision=HI, preferred_element_type=F32)


_HI16 = -65536


def _pack_bf16_pair(x):
    n = x.shape[-1] // 2
    bits = lambda t: lax.bitcast_convert_type(t.astype(BF16).astype(F32), jnp.int32)
    return (bits(x[:, n:]) & _HI16) | ((bits(x[:, :n]) >> 16) & 0xFFFF)


def _unpack_bf16_pair(w):
    lo = lax.bitcast_convert_type(w << 16, F32)
    hi = lax.bitcast_convert_type(w & _HI16, F32)
    return lo, hi


def _mod_kernel(c_ref, w_ref, b_ref, o_ref):
    o_ref[...] = _dot_hi(_silu(c_ref[...]), w_ref[...]) + b_ref[...]


def adaln_vectors(c8, w_mod, b_mod):
    depth, d, n = w_mod.shape
    tn = 1536
    return pl.pallas_call(
        _mod_kernel,
        grid=(depth, n // tn),
        in_specs=[pl.BlockSpec((8, d), lambda l, j: (0, 0)),
                  pl.BlockSpec((None, d, tn), lambda l, j: (l, 0, j)),
                  pl.BlockSpec((None, 1, tn), lambda l, j: (l, 0, j))],
        out_specs=pl.BlockSpec((None, 8, tn), lambda l, j: (l, 0, j)),
        out_shape=jax.ShapeDtypeStruct((depth, 8, n), F32),
        compiler_params=_cp(("parallel", "parallel")),
        name="adaln_vectors",
    )(c8, w_mod, b_mod.reshape(depth, 1, n))


def _lnmod_kernel(h_ref, mod_ref, o_ref):
    u = _ln(h_ref[...]) * (1.0 + mod_ref[1:2, :]) + mod_ref[0:1, :]
    o_ref[...] = u.astype(o_ref.dtype)


def ln_modulate(h, mod):
    b, l, d = h.shape
    tm = min(l, 512)
    return pl.pallas_call(
        _lnmod_kernel,
        grid=(b, l // tm),
        in_specs=[pl.BlockSpec((None, tm, d), lambda i, j: (i, j, 0)),
                  pl.BlockSpec((None, 8, d), lambda i, j: (i, 0, 0))],
        out_specs=pl.BlockSpec((None, tm, d), lambda i, j: (i, j, 0)),
        out_shape=jax.ShapeDtypeStruct((b, l, d), BF16),
        compiler_params=_cp(("parallel", "parallel")),
        name="ln_modulate",
    )(h, mod)


def _mm_kernel(x_ref, w_ref, b_ref, o_ref):
    o_ref[...] = (jnp.dot(x_ref[...], w_ref[...], preferred_element_type=F32) + b_ref[...]).astype(o_ref.dtype)


def matmul_bias(x, w, b, out_dtype, tn):
    m, k = x.shape
    n = w.shape[1]
    tm = 1024 if m % 1024 == 0 else 512
    return pl.pallas_call(
        _mm_kernel,
        grid=(m // tm, n // tn),
        in_specs=[pl.BlockSpec((tm, k), lambda i, j: (i, 0)),
                  pl.BlockSpec((k, tn), lambda i, j: (0, j)),
                  pl.BlockSpec((1, tn), lambda i, j: (0, j))],
        out_specs=pl.BlockSpec((tm, tn), lambda i, j: (i, j)),
        out_shape=jax.ShapeDtypeStruct((m, n), out_dtype),
        compiler_params=_cp(("parallel", "parallel")),
        name="matmul_bias",
    )(x, w, b.reshape(1, n).astype(F32))


SUBLANES = 8


def _conv_shifted_copies(g_scr, width, tl):
    pad = (width - 1) // 2
    offs = [HALO + k - pad for k in range(width)]
    n = tl + (max(offs) // SUBLANES) * SUBLANES
    step = 64
    for b in sorted({o % SUBLANES for o in offs} - {0}):
        for s in range(0, n, step):
            m = min(step, n - s)
            g_scr[b, s:s + m, :] = g_scr[0, pl.ds(s + b, m), :]


def _conv_rows(g_scr, w_ref, width, r0, rs):
    pad = (width - 1) // 2
    acc = None
    for k in range(width):
        o = HALO + k - pad
        term = w_ref[k:k + 1, :] * g_scr[o % SUBLANES, pl.ds(r0 + (o // SUBLANES) * SUBLANES, rs), :]
        acc = term if acc is None else acc + term
    return acc


def _conformer_kernel(vc, vp, vn, gc, gp, gn, w_ref, cb_ref, lg_ref, lb_ref, o_ref, g_scr, *, tl, rs):
    i = pl.program_id(1)
    nt = pl.num_programs(1)
    glu = lambda v, g: v[...].astype(F32) * _sigmoid(g[...].astype(F32))
    g_scr[0, HALO:HALO + tl, :] = glu(vc, gc)
    g_scr[0, 0:HALO, :] = jnp.where(i > 0, glu(vp, gp), 0.0)
    g_scr[0, HALO + tl:2 * HALO + tl, :] = jnp.where(i < nt - 1, glu(vn, gn), 0.0)
    _conv_shifted_copies(g_scr, CONV_WIDTH, tl)
    for s in range(tl // rs):
        y = _conv_rows(g_scr, w_ref, CONV_WIDTH, s * rs, rs) + cb_ref[...]
        y = _silu(_ln(y) * lg_ref[...] + lb_ref[...])
        o_ref[s * rs:(s + 1) * rs, :] = y.astype(o_ref.dtype)


def conformer_conv(p_main, conv_w, conv_b, ln_g, ln_b):
    b, l, _ = p_main.shape
    c = CONV_DIM
    tl = min(l, 512)
    rs = 64
    hb = tl // HALO
    nhb = l // HALO
    vblk, gblk = MAIN_CONV // c, MAIN_CONV // c + 1
    cur = lambda cb: pl.BlockSpec((None, tl, c), lambda bi, i: (bi, i, cb))
    prv = lambda cb: pl.BlockSpec((None, HALO, c), lambda bi, i: (bi, jnp.maximum(i * hb - 1, 0), cb))
    nxt = lambda cb: pl.BlockSpec((None, HALO, c), lambda bi, i: (bi, jnp.minimum((i + 1) * hb, nhb - 1), cb))
    vec = pl.BlockSpec((1, c), lambda bi, i: (0, 0))
    return pl.pallas_call(
        functools.partial(_conformer_kernel, tl=tl, rs=rs),
        grid=(b, l // tl),
        in_specs=[cur(vblk), prv(vblk), nxt(vblk), cur(gblk), prv(gblk), nxt(gblk),
                  pl.BlockSpec((CONV_WIDTH, c), lambda bi, i: (0, 0)), vec, vec, vec],
        out_specs=pl.BlockSpec((None, tl, c), lambda bi, i: (bi, i, 0)),
        out_shape=jax.ShapeDtypeStruct((b, l, c), BF16),
        scratch_shapes=[pltpu.VMEM((SUBLANES, tl + 2 * HALO, c), F32)],
        compiler_params=_cp(("parallel", "parallel")),
        name="conformer_conv",
    )(p_main, p_main, p_main, p_main, p_main, p_main, conv_w,
      conv_b.reshape(1, c), ln_g.reshape(1, c), ln_b.reshape(1, c))


def _shortconv_kernel(xc, xp, xn, w_ref, o_ref, g_scr, *, tl, rs):
    i = pl.program_id(1)
    nt = pl.num_programs(1)
    g_scr[0, HALO:HALO + tl, :] = xc[...].astype(F32)
    g_scr[0, 0:HALO, :] = jnp.where(i > 0, xp[...].astype(F32), 0.0)
    g_scr[0, HALO + tl:2 * HALO + tl, :] = jnp.where(i < nt - 1, xn[...].astype(F32), 0.0)
    _conv_shifted_copies(g_scr, DN_CONV_WIDTH, tl)
    is_qk = pl.program_id(2) < 2
    for s in range(tl // rs):
        y = _silu(_conv_rows(g_scr, w_ref, DN_CONV_WIDTH, s * rs, rs))
        for h in range(DN_HEADS):
            yh = y[:, h * DN_HEAD_DIM:(h + 1) * DN_HEAD_DIM]
            yh = jnp.where(is_qk, _l2n(yh), yh)
            o_ref[s * rs:(s + 1) * rs, h * DN_HEAD_DIM:(h + 1) * DN_HEAD_DIM] = yh.astype(o_ref.dtype)


def deltanet_shortconv(p_main, dn_conv_w):
    b, l, _ = p_main.shape
    c = 512
    tl = min(l, 512)
    rs = 64
    hb = tl // HALO
    nhb = l // HALO
    cb0 = MAIN_DN // c
    return pl.pallas_call(
        functools.partial(_shortconv_kernel, tl=tl, rs=rs),
        grid=(b, l // tl, 3),
        in_specs=[pl.BlockSpec((None, tl, c), lambda bi, i, j: (bi, i, cb0 + j)),
                  pl.BlockSpec((None, HALO, c), lambda bi, i, j: (bi, jnp.maximum(i * hb - 1, 0), cb0 + j)),
                  pl.BlockSpec((None, HALO, c), lambda bi, i, j: (bi, jnp.minimum((i + 1) * hb, nhb - 1), cb0 + j)),
                  pl.BlockSpec((DN_CONV_WIDTH, c), lambda bi, i, j: (0, j))],
        out_specs=pl.BlockSpec((None, tl, c), lambda bi, i, j: (bi, i, j)),
        out_shape=jax.ShapeDtypeStruct((b, l, 3 * c), BF16),
        scratch_shapes=[pltpu.VMEM((SUBLANES, tl + 2 * HALO, c), F32)],
        compiler_params=_cp(("parallel", "parallel", "parallel")),
        name="deltanet_shortconv",
    )(p_main, p_main, p_main, dn_conv_w)


def _chunk_masks(rev):
    r = lax.broadcasted_iota(jnp.int32, (CHUNK, CHUNK), 0)
    c = lax.broadcasted_iota(jnp.int32, (CHUNK, CHUNK), 1)
    if rev:
        cum, sx, incl, strict = c >= r, r < c, c >= r, c > r
    else:
        cum, sx, incl, strict = c <= r, r > c, c <= r, c < r
    return cum.astype(F32), sx.astype(F32), incl, strict, (r == c).astype(F32)


def _dn_chunks(chains):
    each = lambda f: [f(c) for c in chains]
    scale = DN_HEAD_DIM ** -0.5
    decay = each(lambda c: jnp.where(c["incl"], jnp.exp(c["gs"] - c["gs_row"]), 0.0))
    kb = each(lambda c: c["k"] * c["beta"])
    m = [jnp.where(c["strict"], _dot_nt(kbi, c["k"]) * di, 0.0) for c, kbi, di in zip(chains, kb, decay)]
    inv = [c["eye"] - mi for c, mi in zip(chains, m)]
    p = m
    for _ in range(int(math.log2(CHUNK)) - 1):
        p = [_dot(pi, pi) for pi in p]
        inv = [ii + _dot(ii, pi) for ii, pi in zip(inv, p)]
    u = [_dot(ii, c["v"] * c["beta"]) for c, ii in zip(chains, inv)]
    w = [_dot(ii, kbi * c["eg"]) for c, ii, kbi in zip(chains, inv, kb)]
    a = [_dot_nt(c["q"] * scale, c["k"]) * di for c, di in zip(chains, decay)]
    v_new = [ui - _dot(wi, c["s"]) for c, ui, wi in zip(chains, u, w)]
    o = [_dot(c["q"] * scale * c["eg"], c["s"]) + _dot(ai, vi) for c, ai, vi in zip(chains, a, v_new)]
    s_new = [c["s"] * c["etot"] + _dot_tn(c["k"] * c["egt"], vi) for c, vi in zip(chains, v_new)]
    return list(zip(o, s_new))


def _l2n(t):
    return t * lax.rsqrt(jnp.sum(t * t, axis=-1, keepdims=True) + EPS)


def _dn_scan_kernel(xf_ref, bgf_ref, xb_ref, bgb_ref, par_ref, s0_ref, of_ref, ob_ref, sfin_ref, *s_scr, nb):
    i = pl.program_id(0)
    n = pl.num_programs(0)
    hd = DN_HEAD_DIM
    chain = lambda b, d, h: s_scr[(b * 2 + d) * DN_HEADS + h]

    @pl.when(i == 0)
    def _():
        for b in range(nb):
            for d in range(2):
                for h in range(DN_HEADS):
                    chain(b, d, h)[...] = s0_ref[b, d, h]

    chains, sinks = [], []
    for d, (x_ref, bg_ref, o_ref) in enumerate(((xf_ref, bgf_ref, of_ref), (xb_ref, bgb_ref, ob_ref))):
        cum, _, incl, strict, eye = _chunk_masks(rev=bool(d))
        for b in range(nb):
            bg = bg_ref[b]
            beta_all = _sigmoid(bg)
            g_all = -jnp.exp(par_ref[0:1, :]) * _softplus(bg + par_ref[1:2, :])
            gs_all = _dot_hi(cum, g_all)
            tot = gs_all[0:1, :] if d else gs_all[CHUNK - 1:CHUNK, :]
            eg_all, egt_all, etot_all = jnp.exp(gs_all), jnp.exp(tot - gs_all), jnp.exp(tot)
            gs_t = jnp.concatenate([gs_all, jnp.zeros_like(gs_all)], axis=0).T
            for h in range(DN_HEADS):
                cb = d * DN_HEADS + h
                col = 2 * DN_HEADS + cb
                chains.append(dict(
                    q=x_ref[b, :, h * hd:(h + 1) * hd].astype(F32),
                    k=x_ref[b, :, DN_DIM + h * hd:DN_DIM + (h + 1) * hd].astype(F32),
                    v=x_ref[b, :, 2 * DN_DIM + h * hd:2 * DN_DIM + (h + 1) * hd].astype(F32),
                    beta=beta_all[:, cb:cb + 1], gs=gs_all[:, col:col + 1], eg=eg_all[:, col:col + 1],
                    egt=egt_all[:, col:col + 1], etot=etot_all[:, col:col + 1], gs_row=gs_t[col:col + 1, :CHUNK],
                    s=chain(b, d, h)[...], incl=incl, strict=strict, eye=eye))
                sinks.append((o_ref, b, h, chain(b, d, h)))
    for (o_ref, b, h, s_ref), (o, s_new) in zip(sinks, _dn_chunks(chains)):
        o_ref[b, :, h * hd:(h + 1) * hd] = o
        s_ref[...] = s_new

    @pl.when(i == n - 1)
    def _():
        for b in range(nb):
            for d in range(2):
                for h in range(DN_HEADS):
                    sfin_ref[b, d, h] = chain(b, d, h)[...]


def deltanet_scan(qkv, bg, par, s0):
    b, l, _ = qkv.shape
    n = l // CHUNK
    fwd = lambda w: pl.BlockSpec((b, CHUNK, w), lambda i: (0, i, 0))
    bwd = lambda w: pl.BlockSpec((b, CHUNK, w), lambda i: (0, n - 1 - i, 0))
    st = pl.BlockSpec(s0.shape, lambda i: (0, 0, 0, 0, 0))
    return pl.pallas_call(
        functools.partial(_dn_scan_kernel, nb=b),
        grid=(n,),
        in_specs=[fwd(3 * DN_DIM), fwd(128), bwd(3 * DN_DIM), bwd(128),
                  pl.BlockSpec((8, 128), lambda i: (0, 0)), st],
        out_specs=[fwd(DN_DIM), bwd(DN_DIM), st],
        out_shape=[jax.ShapeDtypeStruct((b, l, DN_DIM), F32), jax.ShapeDtypeStruct((b, l, DN_DIM), F32),
                   jax.ShapeDtypeStruct(s0.shape, F32)],
        scratch_shapes=[pltpu.VMEM((DN_HEAD_DIM, DN_HEAD_DIM), F32)] * (b * 2 * DN_HEADS),
        compiler_params=_cp(("arbitrary",)),
        name="deltanet_scan",
    )(qkv, bg, qkv, bg, par, s0)


GLA_SUB = 16


def _gla_chunks(chains):
    row = lax.broadcasted_iota(jnp.int32, (CHUNK, 1), 0)
    lane = lax.broadcasted_iota(jnp.int32, (1, 2 * GLA_DK), 1)
    o_inter = [_dot_nt(c["qp"] * jnp.exp(c["bp"]), c["st"]) for c in chains]
    blocks = [([], []) for _ in chains]
    for blk in range(CHUNK // GLA_SUB):
        i0 = blk * GLA_SUB
        mid = i0 + GLA_SUB // 2
        ri = lax.broadcasted_iota(jnp.int32, (GLA_SUB, CHUNK), 0) + i0
        ci = lax.broadcasted_iota(jnp.int32, (GLA_SUB, CHUNK), 1)
        for c, blks in zip(chains, blocks):
            bp = c["bp"]
            ref = bp[mid:mid + 1, :]
            qt = c["qp"][i0:i0 + GLA_SUB, :] * jnp.exp(bp[i0:i0 + GLA_SUB, :] - ref)
            valid = (row >= i0) if c["rev"] else (row < i0 + GLA_SUB)
            kt = c["kp"] * jnp.exp(jnp.where(valid, ref - bp, 0.0))
            causal = (ci >= ri) if c["rev"] else (ci <= ri)
            for hh in range(2):
                qh = jnp.where((lane >= hh * GLA_DK) & (lane < (hh + 1) * GLA_DK), qt, 0.0)
                blks[hh].append(jnp.where(causal, _dot_nt(qh, kt), 0.0))
    o_intra = [[_dot(jnp.concatenate(blks[hh], axis=0), c["vp"][:, hh * GLA_DV:(hh + 1) * GLA_DV]) for hh in range(2)]
               for c, blks in zip(chains, blocks)]
    srow = lax.broadcasted_iota(jnp.int32, (2 * GLA_DV, 2 * GLA_DK), 0)
    scol = lax.broadcasted_iota(jnp.int32, (2 * GLA_DV, 2 * GLA_DK), 1)
    bd = (srow < GLA_DV) == (scol < GLA_DK)
    out = []
    for c, oi, ox in zip(chains, o_inter, o_intra):
        bp = c["bp"]
        b_last = bp[0:1, :] if c["rev"] else bp[CHUNK - 1:CHUNK, :]
        st_new = jnp.where(bd, c["st"] * jnp.exp(b_last) + _dot_tn(c["vp"], c["kp"] * jnp.exp(b_last - bp)), 0.0)
        out.append((oi + jnp.concatenate(ox, axis=1), st_new))
    return out


def _gla_scan_kernel(qkf, vf, lrf, qkb, vb, lrb, w2_ref, b2_ref, s0_ref, of_ref, ob_ref, sfin_ref, *s_scr, nb):
    i = pl.program_id(0)
    n = pl.num_programs(0)
    npair = GLA_HEADS // 2
    chain = lambda b, d, p: s_scr[(b * 2 + d) * npair + p]

    @pl.when(i == 0)
    def _():
        for b in range(nb):
            for d in range(2):
                for p in range(npair):
                    chain(b, d, p)[...] = s0_ref[b, d, p]

    chains, sinks = [], []
    zs = [[_dot_hi(lr_ref[b], w2_ref[d]) + b2_ref[d] for b in range(nb)]
          for d, lr_ref in enumerate((lrf, lrb))]
    for d, (qk_ref, v_ref, o_ref) in enumerate(((qkf, vf, of_ref), (qkb, vb, ob_ref))):
        cum = _chunk_masks(rev=bool(d))[0]
        for b in range(nb):
            bs = _dot_hi(cum, -_softplus(-zs[d][b]) * (1.0 / GLA_TAU))
            for p in range(npair):
                lo = p * 2 * GLA_DK
                chains.append(dict(
                    qp=qk_ref[b, :, lo:lo + 2 * GLA_DK].astype(F32) * (GLA_DK ** -0.5),
                    kp=qk_ref[b, :, GLA_KDIM + lo:GLA_KDIM + lo + 2 * GLA_DK].astype(F32),
                    vp=v_ref[b, :, p * 2 * GLA_DV:(p + 1) * 2 * GLA_DV].astype(F32),
                    bp=bs[:, lo:lo + 2 * GLA_DK], st=chain(b, d, p)[...], rev=bool(d)))
                sinks.append((o_ref, b, p, chain(b, d, p)))
    for (o_ref, b, p, s_ref), (o, st_new) in zip(sinks, _gla_chunks(chains)):
        o_ref[b, :, p * 2 * GLA_DV:(p + 1) * 2 * GLA_DV] = o
        s_ref[...] = st_new

    @pl.when(i == n - 1)
    def _():
        for b in range(nb):
            for d in range(2):
                for p in range(npair):
                    sfin_ref[b, d, p] = chain(b, d, p)[...]


def gla_scan(p_gla, lr, w2p, b2, s0):
    b, l, _ = p_gla.shape
    n = l // CHUNK
    fwd = lambda w, cb: pl.BlockSpec((b, CHUNK, w), lambda i: (0, i, cb))
    bwd = lambda w, cb: pl.BlockSpec((b, CHUNK, w), lambda i: (0, n - 1 - i, cb))
    st = pl.BlockSpec(s0.shape, lambda i: (0, 0, 0, 0, 0))
    return pl.pallas_call(
        functools.partial(_gla_scan_kernel, nb=b),
        grid=(n,),
        in_specs=[fwd(512, 0), fwd(512, 1), fwd(128, 0), bwd(512, 0), bwd(512, 1), bwd(128, 0),
                  pl.BlockSpec((2, 128, GLA_KDIM), lambda i: (0, 0, 0)),
                  pl.BlockSpec((2, 1, GLA_KDIM), lambda i: (0, 0, 0)), st],
        out_specs=[fwd(GLA_VDIM, 0), bwd(GLA_VDIM, 0), st],
        out_shape=[jax.ShapeDtypeStruct((b, l, GLA_VDIM), F32), jax.ShapeDtypeStruct((b, l, GLA_VDIM), F32),
                   jax.ShapeDtypeStruct(s0.shape, F32)],
        scratch_shapes=[pltpu.VMEM((2 * GLA_DV, 2 * GLA_DK), F32)] * (b * 2 * (GLA_HEADS // 2)),
        compiler_params=_cp(("arbitrary",)),
        name="gla_scan",
    )(p_gla, p_gla, lr, p_gla, p_gla, lr, w2p, b2, s0)


def _headnorm_kernel(of_ref, ob_ref, z_ref, g_ref, o_ref):
    o = of_ref[...] + ob_ref[...]
    z = z_ref[...].astype(F32)
    for h in range(4):
        sl = slice(h * 128, (h + 1) * 128)
        oh = o[:, sl]
        oh = oh * lax.rsqrt(jnp.mean(oh * oh, axis=-1, keepdims=True) + EPS) * g_ref[...]
        o_ref[:, sl] = (oh * _silu(z[:, sl])).astype(o_ref.dtype)


def gated_head_norm(o_f, o_b, z_arr, z_blk, g):
    b, l, c = o_f.shape
    tl = min(l, 512)
    blk = pl.BlockSpec((None, tl, c), lambda bi, i: (bi, i, 0))
    return pl.pallas_call(
        _headnorm_kernel,
        grid=(b, l // tl),
        in_specs=[blk, blk, pl.BlockSpec((None, tl, c), lambda bi, i: (bi, i, z_blk)),
                  pl.BlockSpec((1, 128), lambda bi, i: (0, 0))],
        out_specs=blk,
        out_shape=jax.ShapeDtypeStruct((b, l, c), BF16),
        compiler_params=_cp(("parallel", "parallel")),
        name="gated_head_norm",
    )(o_f, o_b, z_arr, g.reshape(1, 128).astype(F32))


def _dft_tables(n):
    ang = 2.0 * np.pi * (np.outer(np.arange(n), np.arange(n)) % n) / n
    return np.cos(ang), np.sin(ang)


def _fnet_small_kernel(x_ref, wc_ref, cl_ref, sl_ref, o_ref):
    y = jnp.dot(x_ref[...], wc_ref[...], preferred_element_type=F32)
    out = _dot(cl_ref[...], y[:, :FN_GROUP_DIM]) + _dot(sl_ref[...], y[:, FN_GROUP_DIM:])
    o_ref[...] = out.astype(o_ref.dtype)


def _fnet_big_kernel(x_ref, wc_ref, f1_ref, twc_ref, tws_ref, f2_ref, o_ref, y_scr, yi_scr, b_scr, *, n1):
    n2 = FN_GROUP_DIM
    l = n1 * n2
    rb = 512

    def step0(r, c):
        r0 = pl.multiple_of(r * rb, rb)
        y = jnp.dot(x_ref[pl.ds(r0, rb), :], wc_ref[...], preferred_element_type=F32)
        y_scr[pl.ds(r0, rb), :] = y[:, :n2]
        yi_scr[pl.ds(r0, rb), :] = y[:, n2:]
        return c

    lax.fori_loop(0, l // rb, step0, 0)

    def step1(j, c):
        mr = _dot(f1_ref[...], y_scr[pl.ds(j, n1, stride=n2), :])
        mi = _dot(f1_ref[...], yi_scr[pl.ds(j, n1, stride=n2), :])
        ar = mr[:n1] + mi[n1:]
        ai = mi[:n1] - mr[n1:]
        tc, ts = twc_ref[j], tws_ref[j]
        b_scr[pl.ds(j, n1, stride=2 * n2), :] = ar * tc + ai * ts
        b_scr[pl.ds(n2 + j, n1, stride=2 * n2), :] = ai * tc - ar * ts
        return c

    lax.fori_loop(0, n2, step1, 0, unroll=8)

    def step2(k1, c):
        bk = b_scr[pl.ds(pl.multiple_of(k1 * 2 * n2, 2 * n2), 2 * n2), :]
        y_scr[pl.ds(k1, n2, stride=n1), :] = _dot(f2_ref[...], bk)
        return c

    lax.fori_loop(0, n1, step2, 0, unroll=8)

    def step3(r, c):
        r0 = pl.multiple_of(r * rb, rb)
        o_ref[pl.ds(r0, rb), :] = y_scr[pl.ds(r0, rb), :].astype(o_ref.dtype)
        return c

    lax.fori_loop(0, l // rb, step3, 0)


def fourier_mix(p_main):
    b, l, _ = p_main.shape
    gd = FN_GROUP_DIM
    cc, sc = _dft_tables(gd)
    wc = jnp.asarray(np.concatenate([cc, -sc], axis=1) / math.sqrt(gd), BF16)
    x_spec = pl.BlockSpec((None, l, gd), lambda bi, g: (bi, 0, MAIN_FN // gd + g))
    o_spec = pl.BlockSpec((None, l, gd), lambda bi, g: (bi, 0, g))
    full = lambda shape: pl.BlockSpec(shape, lambda bi, g: (0,) * len(shape))
    out_shape = jax.ShapeDtypeStruct((b, l, FN_DIM), BF16)
    if l <= 512:
        cl, sl = _dft_tables(l)
        scale = 1.0 / math.sqrt(l)
        return pl.pallas_call(
            _fnet_small_kernel, grid=(b, FN_GROUPS),
            in_specs=[x_spec, full((gd, 2 * gd)), full((l, l)), full((l, l))],
            out_specs=o_spec, out_shape=out_shape,
            compiler_params=_cp(("parallel", "parallel")), name="fourier_mix_small",
        )(p_main, wc, jnp.asarray(cl * scale, BF16), jnp.asarray(sl * scale, BF16))
    n1, n2 = l // gd, gd
    c1, s1 = _dft_tables(n1)
    c2, s2 = _dft_tables(n2)
    f1 = jnp.asarray(np.concatenate([c1, s1], axis=0), BF16)
    f2 = jnp.asarray(np.concatenate([c2, s2], axis=1) / math.sqrt(l), BF16)
    ang = 2.0 * np.pi * np.outer(np.arange(n2), np.arange(n1)) / l
    twc = jnp.asarray(np.broadcast_to(np.cos(ang)[:, :, None], (n2, n1, gd)), F32)
    tws = jnp.asarray(np.broadcast_to(np.sin(ang)[:, :, None], (n2, n1, gd)), F32)
    return pl.pallas_call(
        functools.partial(_fnet_big_kernel, n1=n1), grid=(b, FN_GROUPS),
        in_specs=[x_spec, full((gd, 2 * gd)), full((2 * n1, n1)), full((n2, n1, gd)), full((n2, n1, gd)),
                  full((n2, 2 * n2))],
        out_specs=o_spec, out_shape=out_shape,
        scratch_shapes=[pltpu.VMEM((l, gd), F32), pltpu.VMEM((l, gd), F32), pltpu.VMEM((n1 * 2 * n2, gd), F32)],
        compiler_params=_cp(("parallel", "parallel")), name="fourier_mix_big",
    )(p_main, wc, f1, twc, tws, f2)


def _merge_kernel(fa, fb, fc, fd, gt_ref, wb_ref, wo_ref, h_ref, mod_ref, lng_ref, lnb_ref, wr_ref,
                  h1_ref, v_ref, lg_ref):
    acc = None
    for n, f_ref in enumerate((fa, fb, fc, fd)):
        proj = jnp.dot(f_ref[...], wb_ref[n], preferred_element_type=F32)
        term = _sigmoid(gt_ref[:, n * D_MODEL:(n + 1) * D_MODEL].astype(F32)) * proj
        acc = term if acc is None else acc + term
    y = _dot(acc, wo_ref[...])
    h1 = _ln(DEEPNORM_ALPHA * h_ref[...] + mod_ref[2:3, :] * y) * lng_ref[...] + lnb_ref[...]
    h1_ref[...] = h1
    v = _ln(h1) * (1.0 + mod_ref[4:5, :]) + mod_ref[3:4, :]
    v_ref[...] = _pack_bf16_pair(v)
    lg_ref[...] = lax.dot_general(wr_ref[...], v, (((1,), (1,)), ((), ())), precision=HI,
                                  preferred_element_type=F32)


def merge_branches(feats, p_main, w_branch, w_o, h, mod, ln_g, ln_b, w_router_t):
    b, l, d = h.shape
    tm = 256
    nt = l // tm
    fblk = pl.BlockSpec((None, tm, BRANCH_DIM), lambda bi, i: (bi, i, 0))
    hblk = pl.BlockSpec((None, tm, d), lambda bi, i: (bi, i, 0))
    vec = pl.BlockSpec((1, d), lambda bi, i: (0, 0))
    return pl.pallas_call(
        _merge_kernel,
        grid=(b, l // tm),
        in_specs=[fblk, fblk, fblk, fblk,
                  pl.BlockSpec((None, tm, N_BRANCH * d), lambda bi, i: (bi, i, MAIN_GATE // (N_BRANCH * d))),
                  pl.BlockSpec((N_BRANCH, BRANCH_DIM, d), lambda bi, i: (0, 0, 0)),
                  pl.BlockSpec((d, d), lambda bi, i: (0, 0)),
                  hblk, pl.BlockSpec((None, 8, d), lambda bi, i: (bi, 0, 0)), vec, vec,
                  pl.BlockSpec((N_EXPERTS, d), lambda bi, i: (0, 0))],
        out_specs=[hblk, pl.BlockSpec((None, tm, d // 2), lambda bi, i: (bi, i, 0)),
                   pl.BlockSpec((N_EXPERTS, tm), lambda bi, i: (0, bi * nt + i))],
        out_shape=[jax.ShapeDtypeStruct((b, l, d), F32), jax.ShapeDtypeStruct((b, l, d // 2), jnp.int32),
                   jax.ShapeDtypeStruct((N_EXPERTS, b * l), F32)],
        compiler_params=_cp(("parallel", "parallel")),
        name="merge_branches",
    )(*feats, p_main, w_branch, w_o, h, mod, ln_g.reshape(1, d), ln_b.reshape(1, d), w_router_t)


def _first_argmax(x, axis, size):
    m = jnp.max(x, axis=axis, keepdims=True)
    idx = lax.broadcasted_iota(jnp.int32, x.shape, axis)
    first = jnp.min(jnp.where(x == m, idx, size), axis=axis, keepdims=True)
    return m, idx == first


def _route_kernel(lg_ref, rb_ref, pos_ref, w_ref, cnt_ref, cnt_scr, *, capacity):
    i = pl.program_id(0)

    @pl.when(i == 0)
    def _():
        cnt_scr[...] = jnp.zeros_like(cnt_scr)

    tm = lg_ref.shape[-1]
    per = N_EXPERTS // N_EXPERT_GROUPS
    scores = _sigmoid(lg_ref[...])
    biased = scores + rb_ref[...]
    x3 = biased.reshape(N_EXPERT_GROUPS, per, tm)
    m1, hit = _first_argmax(x3, 1, per)
    m2 = jnp.max(jnp.where(hit, -jnp.inf, x3), axis=1, keepdims=True)
    gscore = (m1 + m2).reshape(N_EXPERT_GROUPS, tm)
    gsel = jnp.zeros(gscore.shape, F32)
    for _ in range(TOP_GROUPS):
        _, hit = _first_argmax(gscore, 0, N_EXPERT_GROUPS)
        gsel = jnp.where(hit, 1.0, gsel)
        gscore = jnp.where(hit, -jnp.inf, gscore)
    masked = jnp.where(gsel.reshape(N_EXPERT_GROUPS, 1, tm) > 0.0, x3, -jnp.inf).reshape(N_EXPERTS, tm)
    sel = jnp.zeros(masked.shape, F32)
    for _ in range(TOP_K):
        _, hit = _first_argmax(masked, 0, N_EXPERTS)
        sel = jnp.where(hit, 1.0, sel)
        masked = jnp.where(hit, -jnp.inf, masked)
    w = sel * scores
    comb = w / jnp.sum(w, axis=0, keepdims=True) * ROUTED_SCALE
    tr = lax.broadcasted_iota(jnp.int32, (tm, tm), 0)
    tc = lax.broadcasted_iota(jnp.int32, (tm, tm), 1)
    rank = cnt_scr[:, 0:1] + _dot(sel, (tr < tc).astype(F32))
    slot = lax.broadcasted_iota(jnp.int32, sel.shape, 0).astype(F32) * float(capacity) + rank
    pos_rows, w_rows = [], []
    remaining = sel
    for _ in range(TOP_K):
        _, hit = _first_argmax(remaining, 0, N_EXPERTS)
        pos_rows.append(jnp.sum(jnp.where(hit, slot, 0.0), axis=0, keepdims=True))
        w_rows.append(jnp.sum(jnp.where(hit, comb, 0.0), axis=0, keepdims=True))
        remaining = jnp.where(hit, 0.0, remaining)
    zero = jnp.zeros((8 - TOP_K, tm), F32)
    pos_ref[...] = jnp.concatenate(pos_rows + [zero], axis=0).astype(jnp.int32)
    w_ref[...] = jnp.concatenate(w_rows + [zero], axis=0)
    cnt_scr[...] = cnt_scr[...] + jnp.sum(sel, axis=1, keepdims=True)

    @pl.when(i == pl.num_programs(0) - 1)
    def _():
        cnt_ref[...] = cnt_scr[...]


def moe_route(logits_t, b_router):
    e, t = logits_t.shape
    tm = 256
    blk = pl.BlockSpec((e, tm), lambda i: (0, i))
    oblk = pl.BlockSpec((8, tm), lambda i: (0, i))
    return pl.pallas_call(
        functools.partial(_route_kernel, capacity=t), grid=(t // tm,),
        in_specs=[blk, pl.BlockSpec((e, tm), lambda i: (0, 0))],
        out_specs=[oblk, oblk, pl.BlockSpec((e, 128), lambda i: (0, 0))],
        out_shape=[jax.ShapeDtypeStruct((8, t), jnp.int32), jax.ShapeDtypeStruct((8, t), F32),
                   jax.ShapeDtypeStruct((e, 128), F32)],
        scratch_shapes=[pltpu.VMEM((e, 128), F32)],
        compiler_params=_cp(("arbitrary",)), name="moe_route",
    )(logits_t, jnp.broadcast_to(b_router.astype(F32)[:, None], (e, tm)))


def _sc_workers():
    info = plsc.get_sparse_core_info()
    return info.num_cores, info.num_subcores


def _sc_chunk(per_worker):
    return max(c for c in range(8, 129, 8) if per_worker % c == 0)


def sc_scatter_rows(rows, pos, n_out):
    t, w = rows.shape
    nc, ns = _sc_workers()
    nw = nc * ns
    per_w = t // nw
    ch = _sc_chunk(per_w)
    nch = per_w // ch
    pos_w = pos[:TOP_K].reshape(TOP_K, nw, nch, ch).transpose(1, 2, 0, 3)
    mesh = plsc.VectorSubcoreMesh(core_axis_name="c", subcore_axis_name="s")

    @functools.partial(
        pl.kernel, mesh=mesh, out_type=jax.ShapeDtypeStruct((n_out, w), jnp.int32),
        scratch_types=[pltpu.VMEM((TOP_K, ch), jnp.int32), pltpu.VMEM((ch, w), jnp.int32), pltpu.SemaphoreType.DMA])
    def scatter(rows_hbm, pos_hbm, out_hbm, idx_v, rows_v, sem):
        wid = lax.axis_index("s") * nc + lax.axis_index("c")

        @pl.loop(0, nch)
        def _(j):
            pltpu.sync_copy(pos_hbm.at[wid, j], idx_v)
            pltpu.sync_copy(rows_hbm.at[pl.ds(wid * per_w + j * ch, ch)], rows_v)
            copies = [pltpu.async_copy(rows_v, out_hbm.at[idx_v.at[k]], sem) for k in range(TOP_K)]
            for cp in copies:
                cp.wait()

    return scatter(rows, pos_w)


def sc_gather_rows(table, idx):
    m = idx.shape[0]
    w = table.shape[1]
    nc, ns = _sc_workers()
    nw = nc * ns
    per_w = m // nw
    ch = _sc_chunk(per_w)
    nch = per_w // ch
    mesh = plsc.VectorSubcoreMesh(core_axis_name="c", subcore_axis_name="s")

    @functools.partial(
        pl.kernel, mesh=mesh, out_type=jax.ShapeDtypeStruct((m, w), jnp.int32),
        scratch_types=[pltpu.VMEM((ch,), jnp.int32), pltpu.VMEM((ch, w), jnp.int32), pltpu.SemaphoreType.DMA])
    def gather(table_hbm, idx_hbm, out_hbm, idx_v, rows_v, sem):
        wid = lax.axis_index("s") * nc + lax.axis_index("c")

        @pl.loop(0, nch)
        def _(j):
            off = wid * per_w + j * ch
            pltpu.sync_copy(idx_hbm.at[pl.ds(off, ch)], idx_v)
            pltpu.async_copy(table_hbm.at[idx_v], rows_v, sem).wait()
            pltpu.sync_copy(rows_v, out_hbm.at[pl.ds(off, ch)])

    return gather(table, idx)


MOE_TM = 512


def _gmm_kernel(te_ref, tj_ref, na_ref, x_ref, wg_ref, wu_ref, wd_ref, y_ref, wg_s, wu_s, wd_s):
    i = pl.program_id(0)

    @pl.when(i < na_ref[0])
    def _():
        @pl.when(tj_ref[i] == 0)
        def _():
            wg_s[...] = wg_ref[...].astype(BF16)
            wu_s[...] = wu_ref[...].astype(BF16)
            wd_s[...] = wd_ref[...].astype(BF16)

        half = D_MODEL // 2
        lo, hi = _unpack_bf16_pair(x_ref[...])
        gate = _dot(lo, wg_s[:half, :]) + _dot(hi, wg_s[half:, :])
        up = _dot(lo, wu_s[:half, :]) + _dot(hi, wu_s[half:, :])
        y_ref[...] = _pack_bf16_pair(_dot(_silu(gate) * up, wd_s[...]))


def moe_grouped_experts(xs, counts, w_gate, w_up, w_down, capacity):
    n_e, d, hdim = w_gate.shape
    blocks_per_e = capacity // MOE_TM
    n_tiles = capacity * TOP_K // MOE_TM + n_e
    tiles_e = (counts.astype(jnp.int32) + MOE_TM - 1) // MOE_TM
    ends = jnp.cumsum(tiles_e)
    n_active = ends[-1]
    step = jnp.minimum(jnp.arange(n_tiles, dtype=jnp.int32), n_active - 1)
    owned = step[:, None] >= ends[None, :]
    te = jnp.sum(owned, axis=1).astype(jnp.int32)
    tj = step - jnp.sum(jnp.where(owned, tiles_e[None, :], 0), axis=1).astype(jnp.int32)
    row_blk = lambda i, te_r, tj_r, na_r: (te_r[i] * blocks_per_e + tj_r[i], 0)
    wmap = lambda i, te_r, tj_r, na_r: (te_r[i], 0, 0)
    return pl.pallas_call(
        _gmm_kernel,
        grid_spec=pltpu.PrefetchScalarGridSpec(
            num_scalar_prefetch=3, grid=(n_tiles,),
            in_specs=[pl.BlockSpec((MOE_TM, d // 2), row_blk),
                      pl.BlockSpec((None, d, hdim), wmap), pl.BlockSpec((None, d, hdim), wmap),
                      pl.BlockSpec((None, hdim, d), wmap)],
            out_specs=pl.BlockSpec((MOE_TM, d // 2), row_blk),
            scratch_shapes=[pltpu.VMEM((d, hdim), BF16), pltpu.VMEM((d, hdim), BF16), pltpu.VMEM((hdim, d), BF16)]),
        out_shape=jax.ShapeDtypeStruct(xs.shape, jnp.int32),
        compiler_params=_cp(("arbitrary",)),
        name="moe_grouped_experts",
    )(te, tj, n_active.reshape(1).astype(jnp.int32), xs, w_gate, w_up, w_down)


def _moe_out_kernel(v_ref, g_ref, w_ref, sg_ref, su_ref, sd_ref, h_ref, mod_ref, nmod_ref, lng_ref, lnb_ref,
                    h2_ref, u_ref):
    half = D_MODEL // 2
    lo, hi = _unpack_bf16_pair(v_ref[...])
    gate = _dot(lo, sg_ref[:half, :]) + _dot(hi, sg_ref[half:, :])
    up = _dot(lo, su_ref[:half, :]) + _dot(hi, su_ref[half:, :])
    f = _dot(_silu(gate) * up, sd_ref[...])
    acc_lo = acc_hi = None
    for k in range(TOP_K):
        ylo, yhi = _unpack_bf16_pair(g_ref[k])
        wk = w_ref[:, k:k + 1]
        acc_lo = ylo * wk if acc_lo is None else acc_lo + ylo * wk
        acc_hi = yhi * wk if acc_hi is None else acc_hi + yhi * wk
    f = f + jnp.concatenate([acc_lo, acc_hi], axis=1)
    h2 = _ln(DEEPNORM_ALPHA * h_ref[...] + mod_ref[5:6, :] * f) * lng_ref[...] + lnb_ref[...]
    h2_ref[...] = h2
    u_ref[...] = (_ln(h2) * (1.0 + nmod_ref[1:2, :]) + nmod_ref[0:1, :]).astype(u_ref.dtype)


def moe_output(vp, gathered, w_tok, tok0, ws_gate, ws_up, ws_down, h1, mod, next_mod, ln_g, ln_b):
    b, l, d = h1.shape
    tm = 256
    nt = l // tm
    blk0 = tok0 // tm
    xblk = pl.BlockSpec((None, tm, d), lambda bi, i: (bi, i, 0))
    pblk = pl.BlockSpec((None, tm, d // 2), lambda bi, i: (bi, i, 0))
    vec = pl.BlockSpec((1, d), lambda bi, i: (0, 0))
    modblk = pl.BlockSpec((None, 8, d), lambda bi, i: (bi, 0, 0))
    return pl.pallas_call(
        _moe_out_kernel,
        grid=(b, nt),
        in_specs=[pblk, pl.BlockSpec((TOP_K, tm, d // 2), lambda bi, i: (0, blk0 + bi * nt + i, 0)),
                  pl.BlockSpec((tm, 8), lambda bi, i: (blk0 + bi * nt + i, 0)),
                  pl.BlockSpec((d, EXPERT_DIM), lambda bi, i: (0, 0)),
                  pl.BlockSpec((d, EXPERT_DIM), lambda bi, i: (0, 0)),
                  pl.BlockSpec((EXPERT_DIM, d), lambda bi, i: (0, 0)),
                  xblk, modblk, modblk, vec, vec],
        out_specs=[xblk, xblk],
        out_shape=[jax.ShapeDtypeStruct((b, l, d), F32), jax.ShapeDtypeStruct((b, l, d), BF16)],
        compiler_params=_cp(("parallel", "parallel")),
        name="moe_output",
    )(vp, gathered, w_tok, ws_gate, ws_up, ws_down, h1, mod, next_mod, ln_g.reshape(1, d), ln_b.reshape(1, d))


def moe_layer(streams, b_router, w_gate, w_up, w_down, ws_gate, ws_up, ws_down, ln_g, ln_b):
    half = D_MODEL // 2
    vps = [s[0].reshape(-1, half) for s in streams]
    sizes = [v.shape[0] for v in vps]
    t = sum(sizes)
    vp_all = vps[0] if len(vps) == 1 else jnp.concatenate(vps, axis=0)
    lg_all = streams[0][1] if len(vps) == 1 else jnp.concatenate([s[1] for s in streams], axis=1)
    pos, w_rows, counts = moe_route(lg_all, b_router)
    xs = sc_scatter_rows(vp_all, pos, N_EXPERTS * t)
    ys = moe_grouped_experts(xs, counts[:, 0], w_gate, w_up, w_down, t)
    gathered = sc_gather_rows(ys, pos[:TOP_K].reshape(-1)).reshape(TOP_K, t, half)
    w_tok = w_rows.T
    outs, tok0 = [], 0
    for (vp, _, h1, mod, next_mod), n in zip(streams, sizes):
        outs.append(moe_output(vp, gathered, w_tok, tok0, ws_gate, ws_up, ws_down, h1, mod, next_mod, ln_g, ln_b))
        tok0 += n
    return outs


def _to_col_major(t):
    b, l, ch = t.shape
    return t.reshape(b, l // GRID_W, GRID_W, ch).transpose(0, 2, 1, 3).reshape(b, l, ch)


def _from_col_major(t):
    b, l, ch = t.shape
    return t.reshape(b, GRID_W, l // GRID_W, ch).transpose(0, 2, 1, 3).reshape(b, l, ch)


def _layer_weights(l, w_in, b_in, dn_a_log, dn_dt_bias, gla_w2, gla_b2):
    w, bvec = w_in[l], b_in[l]
    cols = lambda a, n: (w[:, a:a + n], bvec[a:a + n])
    parts = (cols(_GATE0, 4096), cols(_DN0, 2048), cols(_CONV0, 1024), cols(_FN0, 512))
    w_main = jnp.concatenate([q[0] for q in parts], axis=1).astype(BF16)
    b_main = jnp.concatenate([q[1] for q in parts])
    pad = lambda wb, n: (jnp.pad(wb[0], ((0, 0), (0, 128 - n))).astype(BF16), jnp.pad(wb[1], (0, 128 - n)))
    w_bg, b_bg = pad(cols(2048, 16), 16)
    w_gla, b_gla = cols(_GLA0, 1536)
    w_lr, b_lr = pad(cols(_GLA0 + 1536, 32), 32)
    par = jnp.zeros((8, 128), F32)
    par = par.at[0, 8:16].set(dn_a_log[l].reshape(-1)).at[1, 8:16].set(dn_dt_bias[l].reshape(-1))
    w2p = jnp.zeros((2, 128, GLA_KDIM), F32)
    w2p = w2p.at[0, 0:16].set(gla_w2[l, 0]).at[1, 16:32].set(gla_w2[l, 1])
    return dict(w_main=w_main, b_main=b_main, w_bg=w_bg, b_bg=b_bg, w_gla=w_gla.astype(BF16), b_gla=b_gla,
                w_lr=w_lr, b_lr=b_lr, par=par, w2p=w2p, b2=gla_b2[l].reshape(2, 1, GLA_KDIM).astype(F32))


def _project(u, lw, col_major):
    b, l, d = u.shape
    flat = u.reshape(b * l, d)
    ug = _to_col_major(u).reshape(b * l, d) if col_major else flat
    p_main = matmul_bias(flat, lw["w_main"], lw["b_main"], BF16, 1536).reshape(b, l, MAIN_COLS)
    bg = matmul_bias(flat, lw["w_bg"], lw["b_bg"], F32, 128).reshape(b, l, 128)
    p_gla = matmul_bias(ug, lw["w_gla"], lw["b_gla"], BF16, 1536).reshape(b, l, 1536)
    lr = matmul_bias(ug, lw["w_lr"], lw["b_lr"], F32, 128).reshape(b, l, 128)
    return p_main, bg, p_gla, lr


def kernel(x, c, ctx, c_ctx, w_mod, b_mod, w_in, b_in, dn_conv_w, dn_a_log, dn_dt_bias, dn_norm_g, gla_w2, gla_b2, gla_norm_g, conv_w, conv_b, conv_ln_g, conv_ln_b, w_branch, w_o, ln_g, ln_b, w_router, b_router, w_gate, w_up, w_down, ws_gate, ws_up, ws_down):
    batch, seq, d = x.shape
    c8 = jnp.zeros((8, d), F32).at[:batch].set(c).at[batch].set(c_ctx)
    mods = adaln_vectors(c8, w_mod, b_mod).reshape(DEPTH, 8, 6, d)
    zrow = jnp.zeros((DEPTH, 2, d), F32)
    mod_x = [jnp.concatenate([mods[l, :batch], jnp.broadcast_to(zrow[l][None], (batch, 2, d))], axis=1)
             for l in range(DEPTH)]
    mod_c = [jnp.broadcast_to(jnp.concatenate([mods[l, batch], zrow[l]], axis=0)[None], (batch, 8, d))
             for l in range(DEPTH)]

    h, hc = x, ctx
    u_x, u_c = ln_modulate(h, mod_x[0]), ln_modulate(hc, mod_c[0])
    for l in range(DEPTH):
        lw = _layer_weights(l, w_in, b_in, dn_a_log, dn_dt_bias, gla_w2, gla_b2)
        px = _project(u_x, lw, True)
        pc = _project(u_c, lw, False)
        qkv_c, qkv_x = deltanet_shortconv(pc[0], dn_conv_w[l]), deltanet_shortconv(px[0], dn_conv_w[l])
        s0 = jnp.zeros((batch, 2, DN_HEADS, DN_HEAD_DIM, DN_HEAD_DIM), F32)
        ocf, ocb, s_c = deltanet_scan(qkv_c, pc[1], lw["par"], s0)
        oxf, oxb, _ = deltanet_scan(qkv_x, px[1], lw["par"], s_c)
        dn_x = gated_head_norm(oxf, oxb, px[0], MAIN_DN // 512 + 3, dn_norm_g[l])
        dn_c = gated_head_norm(ocf, ocb, pc[0], MAIN_DN // 512 + 3, dn_norm_g[l])
        g0 = jnp.zeros((batch, 2, 2, 2 * GLA_DV, 2 * GLA_DK), F32)
        gcf, gcb, gs_c = gla_scan(pc[2], pc[3], lw["w2p"], lw["b2"], g0)
        gxf, gxb, _ = gla_scan(px[2], px[3], lw["w2p"], lw["b2"], gs_c)
        gla_x = _from_col_major(gated_head_norm(gxf, gxb, px[2], 2, gla_norm_g[l]))
        gla_c = gated_head_norm(gcf, gcb, pc[2], 2, gla_norm_g[l])
        wb = w_branch[l].astype(BF16)
        wo = w_o[l].astype(BF16)
        wrt = w_router[l].T
        last = l == DEPTH - 1
        nxt = min(l + 1, DEPTH - 1)
        streams = []
        for (p, dn_f, gla_f, hh, mod, nmod) in ((px, dn_x, gla_x, h, mod_x[l], mod_x[nxt]),
                                               (pc, dn_c, gla_c, hc, mod_c[l], mod_c[nxt]))[:1 if last else 2]:
            feats = (conformer_conv(p[0], conv_w[l], conv_b[l], conv_ln_g[l], conv_ln_b[l]), dn_f,
                     fourier_mix(p[0]), gla_f)
            h1, vp, lg = merge_branches(feats, p[0], wb, wo, hh, mod, ln_g[l, 0], ln_b[l, 0], wrt)
            streams.append((vp, lg, h1, mod, nmod))
        outs = moe_layer(streams, b_router[l], w_gate[l], w_up[l], w_down[l], ws_gate[l], ws_up[l], ws_down[l],
                         ln_g[l, 1], ln_b[l, 1])
        (h, u_x) = outs[0]
        if not last:
            (hc, u_c) = outs[1]
    return h
```

```python
import functools
import math

import jax
import jax.numpy as jnp
import numpy as np
from jax import lax
from jax.experimental import pallas as pl
from jax.experimental.pallas import tpu as pltpu
from jax.experimental.pallas import tpu_sc as plsc

F32 = jnp.float32
BF16 = jnp.bfloat16
HI = lax.Precision.HIGHEST

D_MODEL = 1024
DEPTH = 4
GRID_W = 64
CHUNK = 64
EPS = 1e-6
CONV_DIM = 512
CONV_WIDTH = 31
DN_HEADS = 4
DN_HEAD_DIM = 128
DN_DIM = 512
DN_CONV_WIDTH = 5
FN_GROUPS = 4
FN_GROUP_DIM = 128
FN_DIM = 512
GLA_HEADS = 4
GLA_DK = 64
GLA_DV = 128
GLA_KDIM = 256
GLA_VDIM = 512
GLA_GATE_RANK = 16
GLA_TAU = 16.0
N_BRANCH = 4
BRANCH_DIM = 512
N_EXPERTS = 64
N_EXPERT_GROUPS = 8
TOP_GROUPS = 4
TOP_K = 6
EXPERT_DIM = 256
ROUTED_SCALE = 2.5
DEEPNORM_ALPHA = (2 * DEPTH) ** 0.25

_DN0 = 0
_GLA0 = 4 * DN_DIM + 4 * DN_HEADS
_CONV0 = _GLA0 + 2 * GLA_KDIM + 2 * GLA_VDIM + 2 * GLA_GATE_RANK
_FN0 = _CONV0 + 2 * CONV_DIM
_GATE0 = _FN0 + FN_DIM
IN_DIM = _GATE0 + N_BRANCH * D_MODEL

MAIN_GATE, MAIN_DN, MAIN_CONV, MAIN_FN, MAIN_GLA = 0, 4096, 6144, 7168, 7680
MAIN_COLS = 9216
HALO = 16
VMEM_LIMIT = 56 * 1024 * 1024


def _cp(sem, vmem=None):
    return pltpu.CompilerParams(dimension_semantics=sem, vmem_limit_bytes=vmem or VMEM_LIMIT)


def _sigmoid(x):
    return 0.5 * jnp.tanh(0.5 * x) + 0.5


def _silu(x):
    return x * _sigmoid(x)


def _softplus(x):
    return jnp.maximum(x, 0.0) + jnp.log(1.0 + jnp.exp(-jnp.abs(x)))


def _ln(x):
    mu = jnp.mean(x, axis=-1, keepdims=True)
    xc = x - mu
    var = jnp.mean(xc * xc, axis=-1, keepdims=True)
    return xc * lax.rsqrt(var + EPS)


def _dot(a, b):
    return jnp.dot(a.astype(BF16), b.astype(BF16), preferred_element_type=F32)


def _dot_nt(a, b):
    return lax.dot_general(a.astype(BF16), b.astype(BF16), (((1,), (1,)), ((), ())), preferred_element_type=F32)


def _dot_tn(a, b):
    return lax.dot_general(a.astype(BF16), b.astype(BF16), (((0,), (0,)), ((), ())), preferred_element_type=F32)


def _dot_hi(a, b):
    return jnp.dot(a, b, precision=HI, preferred_element_type=F32)


_HI16 = -65536


def _pack_bf16_pair(x):
    n = x.shape[-1] // 2
    bits = lambda t: lax.bitcast_convert_type(t.astype(BF16).astype(F32), jnp.int32)
    return (bits(x[:, n:]) & _HI16) | ((bits(x[:, :n]) >> 16) & 0xFFFF)


def _unpack_bf16_pair(w):
    lo = lax.bitcast_convert_type(w << 16, F32)
    hi = lax.bitcast_convert_type(w & _HI16, F32)
    return lo, hi


def _mod_kernel(c_ref, w_ref, b_ref, o_ref):
    o_ref[...] = _dot_hi(_silu(c_ref[...]), w_ref[...]) + b_ref[...]


def adaln_vectors(c8, w_mod, b_mod):
    depth, d, n = w_mod.shape
    tn = 1536
    return pl.pallas_call(
        _mod_kernel,
        grid=(depth, n // tn),
        in_specs=[pl.BlockSpec((8, d), lambda l, j: (0, 0)),
                  pl.BlockSpec((None, d, tn), lambda l, j: (l, 0, j)),
                  pl.BlockSpec((None, 1, tn), lambda l, j: (l, 0, j))],
        out_specs=pl.BlockSpec((None, 8, tn), lambda l, j: (l, 0, j)),
        out_shape=jax.ShapeDtypeStruct((depth, 8, n), F32),
        compiler_params=_cp(("parallel", "parallel")),
        name="adaln_vectors",
    )(c8, w_mod, b_mod.reshape(depth, 1, n))


def _lnmod_kernel(h_ref, mod_ref, o_ref):
    u = _ln(h_ref[...]) * (1.0 + mod_ref[1:2, :]) + mod_ref[0:1, :]
    o_ref[...] = u.astype(o_ref.dtype)


def ln_modulate(h, mod):
    b, l, d = h.shape
    tm = min(l, 512)
    return pl.pallas_call(
        _lnmod_kernel,
        grid=(b, l // tm),
        in_specs=[pl.BlockSpec((None, tm, d), lambda i, j: (i, j, 0)),
                  pl.BlockSpec((None, 8, d), lambda i, j: (i, 0, 0))],
        out_specs=pl.BlockSpec((None, tm, d), lambda i, j: (i, j, 0)),
        out_shape=jax.ShapeDtypeStruct((b, l, d), BF16),
        compiler_params=_cp(("parallel", "parallel")),
        name="ln_modulate",
    )(h, mod)


def _mm_kernel(x_ref, w_ref, b_ref, o_ref):
    o_ref[...] = (jnp.dot(x_ref[...], w_ref[...], preferred_element_type=F32) + b_ref[...]).astype(o_ref.dtype)


def matmul_bias(x, w, b, out_dtype, tn):
    m, k = x.shape
    n = w.shape[1]
    tm = 1024 if m % 1024 == 0 else 512
    return pl.pallas_call(
        _mm_kernel,
        grid=(m // tm, n // tn),
        in_specs=[pl.BlockSpec((tm, k), lambda i, j: (i, 0)),
                  pl.BlockSpec((k, tn), lambda i, j: (0, j)),
                  pl.BlockSpec((1, tn), lambda i, j: (0, j))],
        out_specs=pl.BlockSpec((tm, tn), lambda i, j: (i, j)),
        out_shape=jax.ShapeDtypeStruct((m, n), out_dtype),
        compiler_params=_cp(("parallel", "parallel")),
        name="matmul_bias",
    )(x, w, b.reshape(1, n).astype(F32))


SUBLANES = 8


def _conv_shifted_copies(g_scr, width, tl):
    pad = (width - 1) // 2
    offs = [HALO + k - pad for k in range(width)]
    n = tl + (max(offs) // SUBLANES) * SUBLANES
    step = 64
    for b in sorted({o % SUBLANES for o in offs} - {0}):
        for s in range(0, n, step):
            m = min(step, n - s)
            g_scr[b, s:s + m, :] = g_scr[0, pl.ds(s + b, m), :]


def _conv_rows(g_scr, w_ref, width, r0, rs):
    pad = (width - 1) // 2
    acc = None
    for k in range(width):
        o = HALO + k - pad
        term = w_ref[k:k + 1, :] * g_scr[o % SUBLANES, pl.ds(r0 + (o // SUBLANES) * SUBLANES, rs), :]
        acc = term if acc is None else acc + term
    return acc


def _conformer_kernel(vc, vp, vn, gc, gp, gn, w_ref, cb_ref, lg_ref, lb_ref, o_ref, g_scr, *, tl, rs):
    i = pl.program_id(1)
    nt = pl.num_programs(1)
    glu = lambda v, g: v[...].astype(F32) * _sigmoid(g[...].astype(F32))
    g_scr[0, HALO:HALO + tl, :] = glu(vc, gc)
    g_scr[0, 0:HALO, :] = jnp.where(i > 0, glu(vp, gp), 0.0)
    g_scr[0, HALO + tl:2 * HALO + tl, :] = jnp.where(i < nt - 1, glu(vn, gn), 0.0)
    _conv_shifted_copies(g_scr, CONV_WIDTH, tl)
    for s in range(tl // rs):
        y = _conv_rows(g_scr, w_ref, CONV_WIDTH, s * rs, rs) + cb_ref[...]
        y = _silu(_ln(y) * lg_ref[...] + lb_ref[...])
        o_ref[s * rs:(s + 1) * rs, :] = y.astype(o_ref.dtype)


def conformer_conv(p_main, conv_w, conv_b, ln_g, ln_b):
    b, l, _ = p_main.shape
    c = CONV_DIM
    tl = min(l, 512)
    rs = 64
    hb = tl // HALO
    nhb = l // HALO
    vblk, gblk = MAIN_CONV // c, MAIN_CONV // c + 1
    cur = lambda cb: pl.BlockSpec((None, tl, c), lambda bi, i: (bi, i, cb))
    prv = lambda cb: pl.BlockSpec((None, HALO, c), lambda bi, i: (bi, jnp.maximum(i * hb - 1, 0), cb))
    nxt = lambda cb: pl.BlockSpec((None, HALO, c), lambda bi, i: (bi, jnp.minimum((i + 1) * hb, nhb - 1), cb))
    vec = pl.BlockSpec((1, c), lambda bi, i: (0, 0))
    return pl.pallas_call(
        functools.partial(_conformer_kernel, tl=tl, rs=rs),
        grid=(b, l // tl),
        in_specs=[cur(vblk), prv(vblk), nxt(vblk), cur(gblk), prv(gblk), nxt(gblk),
                  pl.BlockSpec((CONV_WIDTH, c), lambda bi, i: (0, 0)), vec, vec, vec],
        out_specs=pl.BlockSpec((None, tl, c), lambda bi, i: (bi, i, 0)),
        out_shape=jax.ShapeDtypeStruct((b, l, c), BF16),
        scratch_shapes=[pltpu.VMEM((SUBLANES, tl + 2 * HALO, c), F32)],
        compiler_params=_cp(("parallel", "parallel")),
        name="conformer_conv",
    )(p_main, p_main, p_main, p_main, p_main, p_main, conv_w,
      conv_b.reshape(1, c), ln_g.reshape(1, c), ln_b.reshape(1, c))


def _shortconv_kernel(xc, xp, xn, w_ref, o_ref, g_scr, *, tl, rs):
    i = pl.program_id(1)
    nt = pl.num_programs(1)
    g_scr[0, HALO:HALO + tl, :] = xc[...].astype(F32)
    g_scr[0, 0:HALO, :] = jnp.where(i > 0, xp[...].astype(F32), 0.0)
    g_scr[0, HALO + tl:2 * HALO + tl, :] = jnp.where(i < nt - 1, xn[...].astype(F32), 0.0)
    _conv_shifted_copies(g_scr, DN_CONV_WIDTH, tl)
    is_qk = pl.program_id(2) < 2
    for s in range(tl // rs):
        y = _silu(_conv_rows(g_scr, w_ref, DN_CONV_WIDTH, s * rs, rs))
        for h in range(DN_HEADS):
            yh = y[:, h * DN_HEAD_DIM:(h + 1) * DN_HEAD_DIM]
            yh = jnp.where(is_qk, _l2n(yh), yh)
            o_ref[s * rs:(s + 1) * rs, h * DN_HEAD_DIM:(h + 1) * DN_HEAD_DIM] = yh.astype(o_ref.dtype)


def deltanet_shortconv(p_main, dn_conv_w):
    b, l, _ = p_main.shape
    c = 512
    tl = min(l, 512)
    rs = 64
    hb = tl // HALO
    nhb = l // HALO
    cb0 = MAIN_DN // c
    return pl.pallas_call(
        functools.partial(_shortconv_kernel, tl=tl, rs=rs),
        grid=(b, l // tl, 3),
        in_specs=[pl.BlockSpec((None, tl, c), lambda bi, i, j: (bi, i, cb0 + j)),
                  pl.BlockSpec((None, HALO, c), lambda bi, i, j: (bi, jnp.maximum(i * hb - 1, 0), cb0 + j)),
                  pl.BlockSpec((None, HALO, c), lambda bi, i, j: (bi, jnp.minimum((i + 1) * hb, nhb - 1), cb0 + j)),
                  pl.BlockSpec((DN_CONV_WIDTH, c), lambda bi, i, j: (0, j))],
        out_specs=pl.BlockSpec((None, tl, c), lambda bi, i, j: (bi, i, j)),
        out_shape=jax.ShapeDtypeStruct((b, l, 3 * c), BF16),
        scratch_shapes=[pltpu.VMEM((SUBLANES, tl + 2 * HALO, c), F32)],
        compiler_params=_cp(("parallel", "parallel", "parallel")),
        name="deltanet_shortconv",
    )(p_main, p_main, p_main, dn_conv_w)


def _chunk_masks(rev):
    r = lax.broadcasted_iota(jnp.int32, (CHUNK, CHUNK), 0)
    c = lax.broadcasted_iota(jnp.int32, (CHUNK, CHUNK), 1)
    if rev:
        cum, sx, incl, strict = c >= r, r < c, c >= r, c > r
    else:
        cum, sx, incl, strict = c <= r, r > c, c <= r, c < r
    return cum.astype(F32), sx.astype(F32), incl, strict, (r == c).astype(F32)


def _dn_chunks(chains):
    each = lambda f: [f(c) for c in chains]
    scale = DN_HEAD_DIM ** -0.5
    n = CHUNK
    rows = lambda top, bot: jnp.concatenate([top, bot], axis=0)
    decay = each(lambda c: jnp.where(c["incl"], jnp.exp(c["gs"] - c["gs_row"]), 0.0))
    kb = each(lambda c: c["k"] * c["beta"])
    qs = each(lambda c: c["q"] * scale)
    kq = [_dot_nt(rows(kbi, qi), c["k"]) for c, kbi, qi in zip(chains, kb, qs)]
    m = [jnp.where(c["strict"], r[:n] * di, 0.0) for c, r, di in zip(chains, kq, decay)]
    a = [r[n:] * di for r, di in zip(kq, decay)]
    inv = [c["eye"] - mi for c, mi in zip(chains, m)]
    p = [_dot(mi, mi) for mi in m]
    for _ in range(int(math.log2(CHUNK)) - 2):
        r = [_dot(rows(pi, ii), pi) for pi, ii in zip(p, inv)]
        inv = [ii + ri[n:] for ii, ri in zip(inv, r)]
        p = [ri[:n] for ri in r]
    inv = [ii + _dot(ii, pi) for ii, pi in zip(inv, p)]
    uw = [_dot(ii, jnp.concatenate([c["v"] * c["beta"], kbi * c["eg"]], axis=1)) for c, ii, kbi in zip(chains, inv, kb)]
    hd = DN_HEAD_DIM
    ws = [_dot(rows(r[:, hd:], qi * c["eg"]), c["s"]) for c, r, qi in zip(chains, uw, qs)]
    v_new = [r[:, :hd] - wsi[:n] for r, wsi in zip(uw, ws)]
    o = [wsi[n:] + _dot(ai, vi) for wsi, ai, vi in zip(ws, a, v_new)]
    s_new = [c["s"] * c["etot"] + _dot_tn(c["k"] * c["egt"], vi) for c, vi in zip(chains, v_new)]
    return list(zip(o, s_new))


def _l2n(t):
    return t * lax.rsqrt(jnp.sum(t * t, axis=-1, keepdims=True) + EPS)


def _dn_scan_kernel(xf_ref, bgf_ref, xb_ref, bgb_ref, par_ref, s0_ref, of_ref, ob_ref, sfin_ref, *s_scr, nb):
    i = pl.program_id(0)
    n = pl.num_programs(0)
    hd = DN_HEAD_DIM
    chain = lambda b, d, h: s_scr[(b * 2 + d) * DN_HEADS + h]

    @pl.when(i == 0)
    def _():
        for b in range(nb):
            for d in range(2):
                for h in range(DN_HEADS):
                    chain(b, d, h)[...] = s0_ref[b, d, h]

    chains, sinks = [], []
    for d, (x_ref, bg_ref, o_ref) in enumerate(((xf_ref, bgf_ref, of_ref), (xb_ref, bgb_ref, ob_ref))):
        cum, _, incl, strict, eye = _chunk_masks(rev=bool(d))
        for b in range(nb):
            bg = bg_ref[b]
            beta_all = _sigmoid(bg)
            g_all = -jnp.exp(par_ref[0:1, :]) * _softplus(bg + par_ref[1:2, :])
            gs_all = _dot_hi(cum, g_all)
            tot = gs_all[0:1, :] if d else gs_all[CHUNK - 1:CHUNK, :]
            eg_all, egt_all, etot_all = jnp.exp(gs_all), jnp.exp(tot - gs_all), jnp.exp(tot)
            gs_t = jnp.concatenate([gs_all, jnp.zeros_like(gs_all)], axis=0).T
            for h in range(DN_HEADS):
                cb = d * DN_HEADS + h
                col = 2 * DN_HEADS + cb
                chains.append(dict(
                    q=x_ref[b, :, h * hd:(h + 1) * hd].astype(F32),
                    k=x_ref[b, :, DN_DIM + h * hd:DN_DIM + (h + 1) * hd].astype(F32),
                    v=x_ref[b, :, 2 * DN_DIM + h * hd:2 * DN_DIM + (h + 1) * hd].astype(F32),
                    beta=beta_all[:, cb:cb + 1], gs=gs_all[:, col:col + 1], eg=eg_all[:, col:col + 1],
                    egt=egt_all[:, col:col + 1], etot=etot_all[:, col:col + 1], gs_row=gs_t[col:col + 1, :CHUNK],
                    s=chain(b, d, h)[...], incl=incl, strict=strict, eye=eye))
                sinks.append((o_ref, b, h, chain(b, d, h)))
    for (o_ref, b, h, s_ref), (o, s_new) in zip(sinks, _dn_chunks(chains)):
        o_ref[b, :, h * hd:(h + 1) * hd] = o.astype(o_ref.dtype)
        s_ref[...] = s_new

    @pl.when(i == n - 1)
    def _():
        for b in range(nb):
            for d in range(2):
                for h in range(DN_HEADS):
                    sfin_ref[b, d, h] = chain(b, d, h)[...]


def deltanet_scan(qkv, bg, par, s0):
    b, l, _ = qkv.shape
    n = l // CHUNK
    fwd = lambda w: pl.BlockSpec((b, CHUNK, w), lambda i: (0, i, 0))
    bwd = lambda w: pl.BlockSpec((b, CHUNK, w), lambda i: (0, n - 1 - i, 0))
    st = pl.BlockSpec(s0.shape, lambda i: (0, 0, 0, 0, 0))
    return pl.pallas_call(
        functools.partial(_dn_scan_kernel, nb=b),
        grid=(n,),
        in_specs=[fwd(3 * DN_DIM), fwd(128), bwd(3 * DN_DIM), bwd(128),
                  pl.BlockSpec((8, 128), lambda i: (0, 0)), st],
        out_specs=[fwd(DN_DIM), bwd(DN_DIM), st],
        out_shape=[jax.ShapeDtypeStruct((b, l, DN_DIM), BF16), jax.ShapeDtypeStruct((b, l, DN_DIM), BF16),
                   jax.ShapeDtypeStruct(s0.shape, F32)],
        scratch_shapes=[pltpu.VMEM((DN_HEAD_DIM, DN_HEAD_DIM), F32)] * (b * 2 * DN_HEADS),
        compiler_params=_cp(("arbitrary",)),
        name="deltanet_scan",
    )(qkv, bg, qkv, bg, par, s0)


GLA_SUB = 16


def _gla_chunks(chains):
    row = lax.broadcasted_iota(jnp.int32, (CHUNK, 1), 0)
    lane = lax.broadcasted_iota(jnp.int32, (1, 2 * GLA_DK), 1)
    o_inter = [_dot_nt(c["qp"] * jnp.exp(c["bp"]), c["st"]) for c in chains]
    blocks = [([], []) for _ in chains]
    for blk in range(CHUNK // GLA_SUB):
        i0 = blk * GLA_SUB
        mid = i0 + GLA_SUB // 2
        ri = lax.broadcasted_iota(jnp.int32, (GLA_SUB, CHUNK), 0) + i0
        ci = lax.broadcasted_iota(jnp.int32, (GLA_SUB, CHUNK), 1)
        for c, blks in zip(chains, blocks):
            bp = c["bp"]
            ref = bp[mid:mid + 1, :]
            qt = c["qp"][i0:i0 + GLA_SUB, :] * jnp.exp(bp[i0:i0 + GLA_SUB, :] - ref)
            valid = (row >= i0) if c["rev"] else (row < i0 + GLA_SUB)
            kt = c["kp"] * jnp.exp(jnp.where(valid, ref - bp, 0.0))
            causal = (ci >= ri) if c["rev"] else (ci <= ri)
            qh = [jnp.where((lane >= hh * GLA_DK) & (lane < (hh + 1) * GLA_DK), qt, 0.0) for hh in range(2)]
            both = _dot_nt(jnp.concatenate(qh, axis=0), kt)
            for hh in range(2):
                blks[hh].append(jnp.where(causal, both[hh * GLA_SUB:(hh + 1) * GLA_SUB], 0.0))
    o_intra = [[_dot(jnp.concatenate(blks[hh], axis=0), c["vp"][:, hh * GLA_DV:(hh + 1) * GLA_DV]) for hh in range(2)]
               for c, blks in zip(chains, blocks)]
    srow = lax.broadcasted_iota(jnp.int32, (2 * GLA_DV, 2 * GLA_DK), 0)
    scol = lax.broadcasted_iota(jnp.int32, (2 * GLA_DV, 2 * GLA_DK), 1)
    bd = (srow < GLA_DV) == (scol < GLA_DK)
    out = []
    for c, oi, ox in zip(chains, o_inter, o_intra):
        bp = c["bp"]
        b_last = bp[0:1, :] if c["rev"] else bp[CHUNK - 1:CHUNK, :]
        st_new = jnp.where(bd, c["st"] * jnp.exp(b_last) + _dot_tn(c["vp"], c["kp"] * jnp.exp(b_last - bp)), 0.0)
        out.append((oi + jnp.concatenate(ox, axis=1), st_new))
    return out


def _gla_scan_kernel(qkf, vf, lrf, qkb, vb, lrb, w2_ref, b2_ref, s0_ref, of_ref, ob_ref, sfin_ref, *s_scr, nb):
    i = pl.program_id(0)
    n = pl.num_programs(0)
    npair = GLA_HEADS // 2
    chain = lambda b, d, p: s_scr[(b * 2 + d) * npair + p]

    @pl.when(i == 0)
    def _():
        for b in range(nb):
            for d in range(2):
                for p in range(npair):
                    chain(b, d, p)[...] = s0_ref[b, d, p]

    chains, sinks = [], []
    zs = [[_dot_hi(lr_ref[b], w2_ref[d]) + b2_ref[d] for b in range(nb)]
          for d, lr_ref in enumerate((lrf, lrb))]
    for d, (qk_ref, v_ref, o_ref) in enumerate(((qkf, vf, of_ref), (qkb, vb, ob_ref))):
        cum = _chunk_masks(rev=bool(d))[0]
        for b in range(nb):
            bs = _dot_hi(cum, -_softplus(-zs[d][b]) * (1.0 / GLA_TAU))
            for p in range(npair):
                lo = p * 2 * GLA_DK
                chains.append(dict(
                    qp=qk_ref[b, :, lo:lo + 2 * GLA_DK].astype(F32) * (GLA_DK ** -0.5),
                    kp=qk_ref[b, :, GLA_KDIM + lo:GLA_KDIM + lo + 2 * GLA_DK].astype(F32),
                    vp=v_ref[b, :, p * 2 * GLA_DV:(p + 1) * 2 * GLA_DV].astype(F32),
                    bp=bs[:, lo:lo + 2 * GLA_DK], st=chain(b, d, p)[...], rev=bool(d)))
                sinks.append((o_ref, b, p, chain(b, d, p)))
    for (o_ref, b, p, s_ref), (o, st_new) in zip(sinks, _gla_chunks(chains)):
        o_ref[b, :, p * 2 * GLA_DV:(p + 1) * 2 * GLA_DV] = o.astype(o_ref.dtype)
        s_ref[...] = st_new

    @pl.when(i == n - 1)
    def _():
        for b in range(nb):
            for d in range(2):
                for p in range(npair):
                    sfin_ref[b, d, p] = chain(b, d, p)[...]


def gla_scan(p_main, small, w2p, b2, s0, col_major):
    b, l, cols = p_main.shape
    n = l // CHUNK
    gw = GRID_W if col_major else 1
    nr = l // gw // CHUNK
    view = lambda t: t.reshape(b, l // gw, gw * t.shape[-1])

    def spec(width, total, cb, rev):
        def index(i):
            j = n - 1 - i if rev else i
            return (0, j % nr, (j // nr) * (total // width) + cb)
        return pl.BlockSpec((b, CHUNK, width), index)

    g0 = MAIN_GLA // 512
    st = pl.BlockSpec(s0.shape, lambda i: (0, 0, 0, 0, 0))
    o_f, o_b, s_fin = pl.pallas_call(
        functools.partial(_gla_scan_kernel, nb=b),
        grid=(n,),
        in_specs=[spec(512, cols, g0, False), spec(512, cols, g0 + 1, False), spec(128, 128, 0, False),
                  spec(512, cols, g0, True), spec(512, cols, g0 + 1, True), spec(128, 128, 0, True),
                  pl.BlockSpec((2, 128, GLA_KDIM), lambda i: (0, 0, 0)),
                  pl.BlockSpec((2, 1, GLA_KDIM), lambda i: (0, 0, 0)), st],
        out_specs=[spec(GLA_VDIM, GLA_VDIM, 0, False), spec(GLA_VDIM, GLA_VDIM, 0, True), st],
        out_shape=[jax.ShapeDtypeStruct((b, l // gw, gw * GLA_VDIM), BF16)] * 2 + [jax.ShapeDtypeStruct(s0.shape, F32)],
        scratch_shapes=[pltpu.VMEM((2 * GLA_DV, 2 * GLA_DK), F32)] * (b * 2 * (GLA_HEADS // 2)),
        compiler_params=_cp(("arbitrary",)),
        name="gla_scan",
    )(view(p_main), view(p_main), view(small), view(p_main), view(p_main), view(small), w2p, b2, s0)
    return o_f.reshape(b, l, GLA_VDIM), o_b.reshape(b, l, GLA_VDIM), s_fin


def _headnorm_kernel(of_ref, ob_ref, z_ref, g_ref, o_ref):
    o = of_ref[...].astype(F32) + ob_ref[...].astype(F32)
    z = z_ref[...].astype(F32)
    for h in range(4):
        sl = slice(h * 128, (h + 1) * 128)
        oh = o[:, sl]
        oh = oh * lax.rsqrt(jnp.mean(oh * oh, axis=-1, keepdims=True) + EPS) * g_ref[...]
        o_ref[:, sl] = (oh * _silu(z[:, sl])).astype(o_ref.dtype)


def gated_head_norm(o_f, o_b, z_arr, z_blk, g):
    b, l, c = o_f.shape
    tl = min(l, 512)
    blk = pl.BlockSpec((None, tl, c), lambda bi, i: (bi, i, 0))
    return pl.pallas_call(
        _headnorm_kernel,
        grid=(b, l // tl),
        in_specs=[blk, blk, pl.BlockSpec((None, tl, c), lambda bi, i: (bi, i, z_blk)),
                  pl.BlockSpec((1, 128), lambda bi, i: (0, 0))],
        out_specs=blk,
        out_shape=jax.ShapeDtypeStruct((b, l, c), BF16),
        compiler_params=_cp(("parallel", "parallel")),
        name="gated_head_norm",
    )(o_f, o_b, z_arr, g.reshape(1, 128).astype(F32))


def _dft_tables(n):
    ang = 2.0 * np.pi * (np.outer(np.arange(n), np.arange(n)) % n) / n
    return np.cos(ang), np.sin(ang)


def _fnet_small_kernel(x_ref, wc_ref, cl_ref, sl_ref, o_ref):
    y = jnp.dot(x_ref[...], wc_ref[...], preferred_element_type=F32)
    out = _dot(cl_ref[...], y[:, :FN_GROUP_DIM]) + _dot(sl_ref[...], y[:, FN_GROUP_DIM:])
    o_ref[...] = out.astype(o_ref.dtype)


def _fnet_big_kernel(x_ref, wc_ref, f1_ref, twc_ref, tws_ref, f2_ref, o_ref, y_scr, yi_scr, b_scr, *, n1):
    n2 = FN_GROUP_DIM
    l = n1 * n2
    rb = 512

    def step0(r, c):
        r0 = pl.multiple_of(r * rb, rb)
        y = jnp.dot(x_ref[pl.ds(r0, rb), :], wc_ref[...], preferred_element_type=F32)
        y_scr[pl.ds(r0, rb), :] = y[:, :n2]
        yi_scr[pl.ds(r0, rb), :] = y[:, n2:]
        return c

    lax.fori_loop(0, l // rb, step0, 0)

    def step1(j, c):
        mr = _dot(f1_ref[...], y_scr[pl.ds(j, n1, stride=n2), :])
        mi = _dot(f1_ref[...], yi_scr[pl.ds(j, n1, stride=n2), :])
        ar = mr[:n1] + mi[n1:]
        ai = mi[:n1] - mr[n1:]
        tc, ts = twc_ref[j], tws_ref[j]
        b_scr[pl.ds(j, n1, stride=2 * n2), :] = ar * tc + ai * ts
        b_scr[pl.ds(n2 + j, n1, stride=2 * n2), :] = ai * tc - ar * ts
        return c

    lax.fori_loop(0, n2, step1, 0, unroll=8)

    def step2(k1, c):
        bk = b_scr[pl.ds(pl.multiple_of(k1 * 2 * n2, 2 * n2), 2 * n2), :]
        y_scr[pl.ds(k1, n2, stride=n1), :] = _dot(f2_ref[...], bk)
        return c

    lax.fori_loop(0, n1, step2, 0, unroll=8)

    def step3(r, c):
        r0 = pl.multiple_of(r * rb, rb)
        o_ref[pl.ds(r0, rb), :] = y_scr[pl.ds(r0, rb), :].astype(o_ref.dtype)
        return c

    lax.fori_loop(0, l // rb, step3, 0)


def fourier_mix(p_main):
    b, l, _ = p_main.shape
    gd = FN_GROUP_DIM
    cc, sc = _dft_tables(gd)
    wc = jnp.asarray(np.concatenate([cc, -sc], axis=1) / math.sqrt(gd), BF16)
    x_spec = pl.BlockSpec((None, l, gd), lambda bi, g: (bi, 0, MAIN_FN // gd + g))
    o_spec = pl.BlockSpec((None, l, gd), lambda bi, g: (bi, 0, g))
    full = lambda shape: pl.BlockSpec(shape, lambda bi, g: (0,) * len(shape))
    out_shape = jax.ShapeDtypeStruct((b, l, FN_DIM), BF16)
    if l <= 512:
        cl, sl = _dft_tables(l)
        scale = 1.0 / math.sqrt(l)
        return pl.pallas_call(
            _fnet_small_kernel, grid=(b, FN_GROUPS),
            in_specs=[x_spec, full((gd, 2 * gd)), full((l, l)), full((l, l))],
            out_specs=o_spec, out_shape=out_shape,
            compiler_params=_cp(("parallel", "parallel")), name="fourier_mix_small",
        )(p_main, wc, jnp.asarray(cl * scale, BF16), jnp.asarray(sl * scale, BF16))
    n1, n2 = l // gd, gd
    c1, s1 = _dft_tables(n1)
    c2, s2 = _dft_tables(n2)
    f1 = jnp.asarray(np.concatenate([c1, s1], axis=0), BF16)
    f2 = jnp.asarray(np.concatenate([c2, s2], axis=1) / math.sqrt(l), BF16)
    ang = 2.0 * np.pi * np.outer(np.arange(n2), np.arange(n1)) / l
    twc = jnp.asarray(np.broadcast_to(np.cos(ang)[:, :, None], (n2, n1, gd)), F32)
    tws = jnp.asarray(np.broadcast_to(np.sin(ang)[:, :, None], (n2, n1, gd)), F32)
    return pl.pallas_call(
        functools.partial(_fnet_big_kernel, n1=n1), grid=(b, FN_GROUPS),
        in_specs=[x_spec, full((gd, 2 * gd)), full((2 * n1, n1)), full((n2, n1, gd)), full((n2, n1, gd)),
                  full((n2, 2 * n2))],
        out_specs=o_spec, out_shape=out_shape,
        scratch_shapes=[pltpu.VMEM((l, gd), F32), pltpu.VMEM((l, gd), F32), pltpu.VMEM((n1 * 2 * n2, gd), F32)],
        compiler_params=_cp(("parallel", "parallel")), name="fourier_mix_big",
    )(p_main, wc, f1, twc, tws, f2)


def _merge_kernel(fa, fb, fc, fd, gt_ref, wb_ref, wo_ref, h_ref, mod_ref, lng_ref, lnb_ref, wr_ref,
                  h1_ref, v_ref, lg_ref):
    half = h_ref.shape[0] // 2
    for r0 in (0, half):
        rs = slice(r0, r0 + half)
        acc = None
        for n, f_ref in enumerate((fa, fb, fc, fd)):
            proj = jnp.dot(f_ref[rs, :], wb_ref[n], preferred_element_type=F32)
            term = _sigmoid(gt_ref[rs, n * D_MODEL:(n + 1) * D_MODEL].astype(F32)) * proj
            acc = term if acc is None else acc + term
        y = _dot(acc, wo_ref[...])
        h1 = _ln(DEEPNORM_ALPHA * h_ref[rs, :] + mod_ref[2:3, :] * y) * lng_ref[...] + lnb_ref[...]
        h1_ref[rs, :] = h1
        v = _ln(h1) * (1.0 + mod_ref[4:5, :]) + mod_ref[3:4, :]
        v_ref[rs, :] = _pack_bf16_pair(v)
        lg_ref[:, rs] = lax.dot_general(wr_ref[...], v, (((1,), (1,)), ((), ())), precision=HI,
                                        preferred_element_type=F32)


def merge_branches(feats, p_main, w_branch, w_o, h, mod, ln_g, ln_b, w_router_t):
    b, l, d = h.shape
    tm = 256
    nt = l // tm
    fblk = pl.BlockSpec((None, tm, BRANCH_DIM), lambda bi, i: (bi, i, 0))
    hblk = pl.BlockSpec((None, tm, d), lambda bi, i: (bi, i, 0))
    vec = pl.BlockSpec((1, d), lambda bi, i: (0, 0))
    return pl.pallas_call(
        _merge_kernel,
        grid=(b, l // tm),
        in_specs=[fblk, fblk, fblk, fblk,
                  pl.BlockSpec((None, tm, N_BRANCH * d), lambda bi, i: (bi, i, MAIN_GATE // (N_BRANCH * d))),
                  pl.BlockSpec((N_BRANCH, BRANCH_DIM, d), lambda bi, i: (0, 0, 0)),
                  pl.BlockSpec((d, d), lambda bi, i: (0, 0)),
                  hblk, pl.BlockSpec((None, 8, d), lambda bi, i: (bi, 0, 0)), vec, vec,
                  pl.BlockSpec((N_EXPERTS, d), lambda bi, i: (0, 0))],
        out_specs=[hblk, pl.BlockSpec((None, tm, d // 2), lambda bi, i: (bi, i, 0)),
                   pl.BlockSpec((N_EXPERTS, tm), lambda bi, i: (0, bi * nt + i))],
        out_shape=[jax.ShapeDtypeStruct((b, l, d), F32), jax.ShapeDtypeStruct((b, l, d // 2), jnp.int32),
                   jax.ShapeDtypeStruct((N_EXPERTS, b * l), F32)],
        compiler_params=_cp(("parallel", "parallel")),
        name="merge_branches",
    )(*feats, p_main, w_branch, w_o, h, mod, ln_g.reshape(1, d), ln_b.reshape(1, d), w_router_t)


def _first_argmax(x, axis, size):
    m = jnp.max(x, axis=axis, keepdims=True)
    idx = lax.broadcasted_iota(jnp.int32, x.shape, axis)
    first = jnp.min(jnp.where(x == m, idx, size), axis=axis, keepdims=True)
    return m, idx == first


def _route_kernel(lg_ref, rb_ref, pos_ref, w_ref, cnt_ref, cnt_scr, *, capacity):
    i = pl.program_id(0)

    @pl.when(i == 0)
    def _():
        cnt_scr[...] = jnp.zeros_like(cnt_scr)

    tm = lg_ref.shape[-1]
    per = N_EXPERTS // N_EXPERT_GROUPS
    scores = _sigmoid(lg_ref[...])
    biased = scores + rb_ref[...]
    x3 = biased.reshape(N_EXPERT_GROUPS, per, tm)
    m1, hit = _first_argmax(x3, 1, per)
    m2 = jnp.max(jnp.where(hit, -jnp.inf, x3), axis=1, keepdims=True)
    gscore = (m1 + m2).reshape(N_EXPERT_GROUPS, tm)
    gsel = jnp.zeros(gscore.shape, F32)
    for _ in range(TOP_GROUPS):
        _, hit = _first_argmax(gscore, 0, N_EXPERT_GROUPS)
        gsel = jnp.where(hit, 1.0, gsel)
        gscore = jnp.where(hit, -jnp.inf, gscore)
    masked = jnp.where(gsel.reshape(N_EXPERT_GROUPS, 1, tm) > 0.0, x3, -jnp.inf).reshape(N_EXPERTS, tm)
    sel = jnp.zeros(masked.shape, F32)
    for _ in range(TOP_K):
        _, hit = _first_argmax(masked, 0, N_EXPERTS)
        sel = jnp.where(hit, 1.0, sel)
        masked = jnp.where(hit, -jnp.inf, masked)
    w = sel * scores
    comb = w / jnp.sum(w, axis=0, keepdims=True) * ROUTED_SCALE
    tr = lax.broadcasted_iota(jnp.int32, (tm, tm), 0)
    tc = lax.broadcasted_iota(jnp.int32, (tm, tm), 1)
    rank = cnt_scr[:, 0:1] + _dot(sel, (tr < tc).astype(F32))
    slot = lax.broadcasted_iota(jnp.int32, sel.shape, 0).astype(F32) * float(capacity) + rank
    pos_rows, w_rows = [], []
    remaining = sel
    for _ in range(TOP_K):
        _, hit = _first_argmax(remaining, 0, N_EXPERTS)
        pos_rows.append(jnp.sum(jnp.where(hit, slot, 0.0), axis=0, keepdims=True))
        w_rows.append(jnp.sum(jnp.where(hit, comb, 0.0), axis=0, keepdims=True))
        remaining = jnp.where(hit, 0.0, remaining)
    zero = jnp.zeros((8 - TOP_K, tm), F32)
    pos_ref[...] = jnp.concatenate(pos_rows + [zero], axis=0).astype(jnp.int32)
    w_ref[...] = jnp.concatenate(w_rows + [zero], axis=0)
    cnt_scr[...] = cnt_scr[...] + jnp.sum(sel, axis=1, keepdims=True)

    @pl.when(i == pl.num_programs(0) - 1)
    def _():
        cnt_ref[...] = cnt_scr[...]


def moe_route(logits_t, b_router):
    e, t = logits_t.shape
    tm = 256
    blk = pl.BlockSpec((e, tm), lambda i: (0, i))
    oblk = pl.BlockSpec((8, tm), lambda i: (0, i))
    return pl.pallas_call(
        functools.partial(_route_kernel, capacity=t), grid=(t // tm,),
        in_specs=[blk, pl.BlockSpec((e, tm), lambda i: (0, 0))],
        out_specs=[oblk, oblk, pl.BlockSpec((e, 128), lambda i: (0, 0))],
        out_shape=[jax.ShapeDtypeStruct((8, t), jnp.int32), jax.ShapeDtypeStruct((8, t), F32),
                   jax.ShapeDtypeStruct((e, 128), F32)],
        scratch_shapes=[pltpu.VMEM((e, 128), F32)],
        compiler_params=_cp(("arbitrary",)), name="moe_route",
    )(logits_t, jnp.broadcast_to(b_router.astype(F32)[:, None], (e, tm)))


def _sc_workers():
    info = plsc.get_sparse_core_info()
    return info.num_cores, info.num_subcores


def _sc_chunk(per_worker):
    return max(c for c in range(8, 129, 8) if per_worker % c == 0)


def sc_scatter_rows(rows, pos, n_out):
    t, w = rows.shape
    nc, ns = _sc_workers()
    nw = nc * ns
    per_w = t // nw
    ch = _sc_chunk(per_w)
    nch = per_w // ch
    pos_w = pos[:TOP_K].reshape(TOP_K, nw, nch, ch).transpose(1, 2, 0, 3)
    mesh = plsc.VectorSubcoreMesh(core_axis_name="c", subcore_axis_name="s")

    @functools.partial(
        pl.kernel, mesh=mesh, out_type=jax.ShapeDtypeStruct((n_out, w), jnp.int32),
        scratch_types=[pltpu.VMEM((TOP_K, ch), jnp.int32), pltpu.VMEM((ch, w), jnp.int32), pltpu.SemaphoreType.DMA])
    def scatter(rows_hbm, pos_hbm, out_hbm, idx_v, rows_v, sem):
        wid = lax.axis_index("s") * nc + lax.axis_index("c")

        @pl.loop(0, nch)
        def _(j):
            pltpu.sync_copy(pos_hbm.at[wid, j], idx_v)
            pltpu.sync_copy(rows_hbm.at[pl.ds(wid * per_w + j * ch, ch)], rows_v)
            copies = [pltpu.async_copy(rows_v, out_hbm.at[idx_v.at[k]], sem) for k in range(TOP_K)]
            for cp in copies:
                cp.wait()

    return scatter(rows, pos_w)


def sc_gather_rows(table, idx):
    m = idx.shape[0]
    w = table.shape[1]
    nc, ns = _sc_workers()
    nw = nc * ns
    per_w = m // nw
    ch = _sc_chunk(per_w)
    nch = per_w // ch
    mesh = plsc.VectorSubcoreMesh(core_axis_name="c", subcore_axis_name="s")

    @functools.partial(
        pl.kernel, mesh=mesh, out_type=jax.ShapeDtypeStruct((m, w), jnp.int32),
        scratch_types=[pltpu.VMEM((ch,), jnp.int32), pltpu.VMEM((ch, w), jnp.int32), pltpu.SemaphoreType.DMA])
    def gather(table_hbm, idx_hbm, out_hbm, idx_v, rows_v, sem):
        wid = lax.axis_index("s") * nc + lax.axis_index("c")

        @pl.loop(0, nch)
        def _(j):
            off = wid * per_w + j * ch
            pltpu.sync_copy(idx_hbm.at[pl.ds(off, ch)], idx_v)
            pltpu.async_copy(table_hbm.at[idx_v], rows_v, sem).wait()
            pltpu.sync_copy(rows_v, out_hbm.at[pl.ds(off, ch)])

    return gather(table, idx)


MOE_TM = 512


def _gmm_kernel(te_ref, tj_ref, na_ref, x_ref, wg_ref, wu_ref, wd_ref, y_ref, wg_s, wu_s, wd_s):
    i = pl.program_id(0)

    @pl.when(i < na_ref[0])
    def _():
        @pl.when(tj_ref[i] == 0)
        def _():
            wg_s[...] = wg_ref[...].astype(BF16)
            wu_s[...] = wu_ref[...].astype(BF16)
            wd_s[...] = wd_ref[...].astype(BF16)

        half = D_MODEL // 2
        lo, hi = _unpack_bf16_pair(x_ref[...])
        gate = _dot(lo, wg_s[:half, :]) + _dot(hi, wg_s[half:, :])
        up = _dot(lo, wu_s[:half, :]) + _dot(hi, wu_s[half:, :])
        y_ref[...] = _pack_bf16_pair(_dot(_silu(gate) * up, wd_s[...]))


def moe_grouped_experts(xs, counts, w_gate, w_up, w_down, capacity):
    n_e, d, hdim = w_gate.shape
    blocks_per_e = capacity // MOE_TM
    n_tiles = capacity * TOP_K // MOE_TM + n_e
    tiles_e = (counts.astype(jnp.int32) + MOE_TM - 1) // MOE_TM
    ends = jnp.cumsum(tiles_e)
    n_active = ends[-1]
    step = jnp.minimum(jnp.arange(n_tiles, dtype=jnp.int32), n_active - 1)
    owned = step[:, None] >= ends[None, :]
    te = jnp.sum(owned, axis=1).astype(jnp.int32)
    tj = step - jnp.sum(jnp.where(owned, tiles_e[None, :], 0), axis=1).astype(jnp.int32)
    row_blk = lambda i, te_r, tj_r, na_r: (te_r[i] * blocks_per_e + tj_r[i], 0)
    wmap = lambda i, te_r, tj_r, na_r: (te_r[i], 0, 0)
    return pl.pallas_call(
        _gmm_kernel,
        grid_spec=pltpu.PrefetchScalarGridSpec(
            num_scalar_prefetch=3, grid=(n_tiles,),
            in_specs=[pl.BlockSpec((MOE_TM, d // 2), row_blk),
                      pl.BlockSpec((None, d, hdim), wmap), pl.BlockSpec((None, d, hdim), wmap),
                      pl.BlockSpec((None, hdim, d), wmap)],
            out_specs=pl.BlockSpec((MOE_TM, d // 2), row_blk),
            scratch_shapes=[pltpu.VMEM((d, hdim), BF16), pltpu.VMEM((d, hdim), BF16), pltpu.VMEM((hdim, d), BF16)]),
        out_shape=jax.ShapeDtypeStruct(xs.shape, jnp.int32),
        compiler_params=_cp(("arbitrary",)),
        name="moe_grouped_experts",
    )(te, tj, n_active.reshape(1).astype(jnp.int32), xs, w_gate, w_up, w_down)


def _moe_out_kernel(v_ref, g_ref, w_ref, sg_ref, su_ref, sd_ref, h_ref, mod_ref, nmod_ref, lng_ref, lnb_ref,
                    h2_ref, u_ref):
    half = D_MODEL // 2
    lo, hi = _unpack_bf16_pair(v_ref[...])
    gate = _dot(lo, sg_ref[:half, :]) + _dot(hi, sg_ref[half:, :])
    up = _dot(lo, su_ref[:half, :]) + _dot(hi, su_ref[half:, :])
    f = _dot(_silu(gate) * up, sd_ref[...])
    acc_lo = acc_hi = None
    for k in range(TOP_K):
        ylo, yhi = _unpack_bf16_pair(g_ref[k])
        wk = w_ref[:, k:k + 1]
        acc_lo = ylo * wk if acc_lo is None else acc_lo + ylo * wk
        acc_hi = yhi * wk if acc_hi is None else acc_hi + yhi * wk
    f = f + jnp.concatenate([acc_lo, acc_hi], axis=1)
    h2 = _ln(DEEPNORM_ALPHA * h_ref[...] + mod_ref[5:6, :] * f) * lng_ref[...] + lnb_ref[...]
    h2_ref[...] = h2
    u_ref[...] = (_ln(h2) * (1.0 + nmod_ref[1:2, :]) + nmod_ref[0:1, :]).astype(u_ref.dtype)


def moe_output(vp, gathered, w_tok, tok0, ws_gate, ws_up, ws_down, h1, mod, next_mod, ln_g, ln_b):
    b, l, d = h1.shape
    tm = 256
    nt = l // tm
    blk0 = tok0 // tm
    xblk = pl.BlockSpec((None, tm, d), lambda bi, i: (bi, i, 0))
    pblk = pl.BlockSpec((None, tm, d // 2), lambda bi, i: (bi, i, 0))
    vec = pl.BlockSpec((1, d), lambda bi, i: (0, 0))
    modblk = pl.BlockSpec((None, 8, d), lambda bi, i: (bi, 0, 0))
    return pl.pallas_call(
        _moe_out_kernel,
        grid=(b, nt),
        in_specs=[pblk, pl.BlockSpec((TOP_K, tm, d // 2), lambda bi, i: (0, blk0 + bi * nt + i, 0)),
                  pl.BlockSpec((tm, 8), lambda bi, i: (blk0 + bi * nt + i, 0)),
                  pl.BlockSpec((d, EXPERT_DIM), lambda bi, i: (0, 0)),
                  pl.BlockSpec((d, EXPERT_DIM), lambda bi, i: (0, 0)),
                  pl.BlockSpec((EXPERT_DIM, d), lambda bi, i: (0, 0)),
                  xblk, modblk, modblk, vec, vec],
        out_specs=[xblk, xblk],
        out_shape=[jax.ShapeDtypeStruct((b, l, d), F32), jax.ShapeDtypeStruct((b, l, d), BF16)],
        compiler_params=_cp(("parallel", "parallel")),
        name="moe_output",
    )(vp, gathered, w_tok, ws_gate, ws_up, ws_down, h1, mod, next_mod, ln_g.reshape(1, d), ln_b.reshape(1, d))


def moe_layer(streams, b_router, w_gate, w_up, w_down, ws_gate, ws_up, ws_down, ln_g, ln_b):
    half = D_MODEL // 2
    vps = [s[0].reshape(-1, half) for s in streams]
    sizes = [v.shape[0] for v in vps]
    t = sum(sizes)
    vp_all = vps[0] if len(vps) == 1 else jnp.concatenate(vps, axis=0)
    lg_all = streams[0][1] if len(vps) == 1 else jnp.concatenate([s[1] for s in streams], axis=1)
    pos, w_rows, counts = moe_route(lg_all, b_router)
    xs = sc_scatter_rows(vp_all, pos, N_EXPERTS * t)
    ys = moe_grouped_experts(xs, counts[:, 0], w_gate, w_up, w_down, t)
    gathered = sc_gather_rows(ys, pos[:TOP_K].reshape(-1)).reshape(TOP_K, t, half)
    w_tok = w_rows.T
    outs, tok0 = [], 0
    for (vp, _, h1, mod, next_mod), n in zip(streams, sizes):
        outs.append(moe_output(vp, gathered, w_tok, tok0, ws_gate, ws_up, ws_down, h1, mod, next_mod, ln_g, ln_b))
        tok0 += n
    return outs


def _layer_weights(l, w_in, b_in, dn_a_log, dn_dt_bias, gla_w2, gla_b2):
    w, bvec = w_in[l], b_in[l]
    cols = lambda a, n: (w[:, a:a + n], bvec[a:a + n])
    parts = (cols(_GATE0, 4096), cols(_DN0, 2048), cols(_CONV0, 1024), cols(_FN0, 512), cols(_GLA0, 1536))
    w_main = jnp.concatenate([q[0] for q in parts], axis=1).astype(BF16)
    b_main = jnp.concatenate([q[1] for q in parts])
    small = (cols(2048, 16), cols(_GLA0 + 1536, 32))
    w_small = jnp.pad(jnp.concatenate([q[0] for q in small], axis=1), ((0, 0), (0, 128 - 48))).astype(BF16)
    b_small = jnp.pad(jnp.concatenate([q[1] for q in small]), (0, 128 - 48))
    par = jnp.zeros((8, 128), F32)
    par = par.at[0, 8:16].set(dn_a_log[l].reshape(-1)).at[1, 8:16].set(dn_dt_bias[l].reshape(-1))
    w2p = jnp.zeros((2, 128, GLA_KDIM), F32)
    w2p = w2p.at[0, 16:32].set(gla_w2[l, 0]).at[1, 32:48].set(gla_w2[l, 1])
    return dict(w_main=w_main, b_main=b_main, w_small=w_small, b_small=b_small, par=par, w2p=w2p,
                b2=gla_b2[l].reshape(2, 1, GLA_KDIM).astype(F32))


def _project(u, lw):
    b, l, d = u.shape
    flat = u.reshape(b * l, d)
    p_main = matmul_bias(flat, lw["w_main"], lw["b_main"], BF16, 1536).reshape(b, l, MAIN_COLS)
    small = matmul_bias(flat, lw["w_small"], lw["b_small"], F32, 128).reshape(b, l, 128)
    return p_main, small


def kernel(x, c, ctx, c_ctx, w_mod, b_mod, w_in, b_in, dn_conv_w, dn_a_log, dn_dt_bias, dn_norm_g, gla_w2, gla_b2, gla_norm_g, conv_w, conv_b, conv_ln_g, conv_ln_b, w_branch, w_o, ln_g, ln_b, w_router, b_router, w_gate, w_up, w_down, ws_gate, ws_up, ws_down):
    batch, seq, d = x.shape
    c8 = jnp.zeros((8, d), F32).at[:batch].set(c).at[batch].set(c_ctx)
    mods = adaln_vectors(c8, w_mod, b_mod).reshape(DEPTH, 8, 6, d)
    zrow = jnp.zeros((DEPTH, 2, d), F32)
    mod_x = [jnp.concatenate([mods[l, :batch], jnp.broadcast_to(zrow[l][None], (batch, 2, d))], axis=1)
             for l in range(DEPTH)]
    mod_c = [jnp.broadcast_to(jnp.concatenate([mods[l, batch], zrow[l]], axis=0)[None], (batch, 8, d))
             for l in range(DEPTH)]

    h, hc = x, ctx
    u_x, u_c = ln_modulate(h, mod_x[0]), ln_modulate(hc, mod_c[0])
    for l in range(DEPTH):
        lw = _layer_weights(l, w_in, b_in, dn_a_log, dn_dt_bias, gla_w2, gla_b2)
        px = _project(u_x, lw)
        pc = _project(u_c, lw)
        qkv_c, qkv_x = deltanet_shortconv(pc[0], dn_conv_w[l]), deltanet_shortconv(px[0], dn_conv_w[l])
        s0 = jnp.zeros((batch, 2, DN_HEADS, DN_HEAD_DIM, DN_HEAD_DIM), F32)
        ocf, ocb, s_c = deltanet_scan(qkv_c, pc[1], lw["par"], s0)
        oxf, oxb, _ = deltanet_scan(qkv_x, px[1], lw["par"], s_c)
        dn_x = gated_head_norm(oxf, oxb, px[0], MAIN_DN // 512 + 3, dn_norm_g[l])
        dn_c = gated_head_norm(ocf, ocb, pc[0], MAIN_DN // 512 + 3, dn_norm_g[l])
        g0 = jnp.zeros((batch, 2, 2, 2 * GLA_DV, 2 * GLA_DK), F32)
        gcf, gcb, gs_c = gla_scan(pc[0], pc[1], lw["w2p"], lw["b2"], g0, False)
        gxf, gxb, _ = gla_scan(px[0], px[1], lw["w2p"], lw["b2"], gs_c, True)
        gla_x = gated_head_norm(gxf, gxb, px[0], MAIN_GLA // 512 + 2, gla_norm_g[l])
        gla_c = gated_head_norm(gcf, gcb, pc[0], MAIN_GLA // 512 + 2, gla_norm_g[l])
        wb = w_branch[l].astype(BF16)
        wo = w_o[l].astype(BF16)
        wrt = w_router[l].T
        last = l == DEPTH - 1
        nxt = min(l + 1, DEPTH - 1)
        streams = []
        for (p, dn_f, gla_f, hh, mod, nmod) in ((px, dn_x, gla_x, h, mod_x[l], mod_x[nxt]),
                                               (pc, dn_c, gla_c, hc, mod_c[l], mod_c[nxt]))[:1 if last else 2]:
            feats = (conformer_conv(p[0], conv_w[l], conv_b[l], conv_ln_g[l], conv_ln_b[l]), dn_f,
                     fourier_mix(p[0]), gla_f)
            h1, vp, lg = merge_branches(feats, p[0], wb, wo, hh, mod, ln_g[l, 0], ln_b[l, 0], wrt)
            streams.append((vp, lg, h1, mod, nmod))
        outs = moe_layer(streams, b_router[l], w_gate[l], w_up[l], w_down[l], ws_gate[l], ws_up[l], ws_down[l],
                         ln_g[l, 1], ln_b[l, 1])
        (h, u_x) = outs[0]
        if not last:
            (hc, u_c) = outs[1]
    return h
```

```python
import functools
import math

import jax
import jax.numpy as jnp
import numpy as np
from jax import lax
from jax.experimental import pallas as pl
from jax.experimental.pallas import tpu as pltpu
from jax.experimental.pallas import tpu_sc as plsc

F32 = jnp.float32
BF16 = jnp.bfloat16
HI = lax.Precision.HIGHEST

D_MODEL = 1024
DEPTH = 4
GRID_W = 64
CHUNK = 64
EPS = 1e-6
CONV_DIM = 512
CONV_WIDTH = 31
DN_HEADS = 4
DN_HEAD_DIM = 128
DN_DIM = 512
DN_CONV_WIDTH = 5
FN_GROUPS = 4
FN_GROUP_DIM = 128
FN_DIM = 512
GLA_HEADS = 4
GLA_DK = 64
GLA_DV = 128
GLA_KDIM = 256
GLA_VDIM = 512
GLA_GATE_RANK = 16
GLA_TAU = 16.0
N_BRANCH = 4
BRANCH_DIM = 512
N_EXPERTS = 64
N_EXPERT_GROUPS = 8
TOP_GROUPS = 4
TOP_K = 6
EXPERT_DIM = 256
ROUTED_SCALE = 2.5
DEEPNORM_ALPHA = (2 * DEPTH) ** 0.25

_DN0 = 0
_GLA0 = 4 * DN_DIM + 4 * DN_HEADS
_CONV0 = _GLA0 + 2 * GLA_KDIM + 2 * GLA_VDIM + 2 * GLA_GATE_RANK
_FN0 = _CONV0 + 2 * CONV_DIM
_GATE0 = _FN0 + FN_DIM
IN_DIM = _GATE0 + N_BRANCH * D_MODEL

MAIN_GATE, MAIN_DN, MAIN_CONV, MAIN_FN, MAIN_GLA = 0, 4096, 6144, 7168, 7680
MAIN_COLS = 9216
HALO = 16
VMEM_LIMIT = 56 * 1024 * 1024


def _cp(sem, vmem=None):
    return pltpu.CompilerParams(dimension_semantics=sem, vmem_limit_bytes=vmem or VMEM_LIMIT)


def _sigmoid(x):
    return 0.5 * jnp.tanh(0.5 * x) + 0.5


def _silu(x):
    return x * _sigmoid(x)


def _softplus(x):
    return jnp.maximum(x, 0.0) + jnp.log(1.0 + jnp.exp(-jnp.abs(x)))


def _ln(x):
    mu = jnp.mean(x, axis=-1, keepdims=True)
    xc = x - mu
    var = jnp.mean(xc * xc, axis=-1, keepdims=True)
    return xc * lax.rsqrt(var + EPS)


def _dot(a, b):
    return jnp.dot(a.astype(BF16), b.astype(BF16), preferred_element_type=F32)


def _dot_nt(a, b):
    return lax.dot_general(a.astype(BF16), b.astype(BF16), (((1,), (1,)), ((), ())), preferred_element_type=F32)


def _dot_tn(a, b):
    return lax.dot_general(a.astype(BF16), b.astype(BF16), (((0,), (0,)), ((), ())), preferred_element_type=F32)


def _dot_hi(a, b):
    return jnp.dot(a, b, precision=HI, preferred_element_type=F32)


_HI16 = -65536


def _pack_bf16_pair(x):
    n = x.shape[-1] // 2
    bits = lambda t: lax.bitcast_convert_type(t.astype(BF16).astype(F32), jnp.int32)
    return (bits(x[:, n:]) & _HI16) | ((bits(x[:, :n]) >> 16) & 0xFFFF)


def _unpack_bf16_pair(w):
    lo = lax.bitcast_convert_type(w << 16, F32)
    hi = lax.bitcast_convert_type(w & _HI16, F32)
    return lo, hi


def _mod_kernel(c_ref, w_ref, b_ref, o_ref):
    o_ref[...] = _dot_hi(_silu(c_ref[...]), w_ref[...]) + b_ref[...]


def adaln_vectors(c8, w_mod, b_mod):
    depth, d, n = w_mod.shape
    tn = 1536
    return pl.pallas_call(
        _mod_kernel,
        grid=(depth, n // tn),
        in_specs=[pl.BlockSpec((8, d), lambda l, j: (0, 0)),
                  pl.BlockSpec((None, d, tn), lambda l, j: (l, 0, j)),
                  pl.BlockSpec((None, 1, tn), lambda l, j: (l, 0, j))],
        out_specs=pl.BlockSpec((None, 8, tn), lambda l, j: (l, 0, j)),
        out_shape=jax.ShapeDtypeStruct((depth, 8, n), F32),
        compiler_params=_cp(("parallel", "parallel")),
        name="adaln_vectors",
    )(c8, w_mod, b_mod.reshape(depth, 1, n))


def _lnmod_kernel(h_ref, mod_ref, o_ref):
    u = _ln(h_ref[...]) * (1.0 + mod_ref[1:2, :]) + mod_ref[0:1, :]
    o_ref[...] = u.astype(o_ref.dtype)


def ln_modulate(h, mod):
    b, l, d = h.shape
    tm = min(l, 512)
    return pl.pallas_call(
        _lnmod_kernel,
        grid=(b, l // tm),
        in_specs=[pl.BlockSpec((None, tm, d), lambda i, j: (i, j, 0)),
                  pl.BlockSpec((None, 8, d), lambda i, j: (i, 0, 0))],
        out_specs=pl.BlockSpec((None, tm, d), lambda i, j: (i, j, 0)),
        out_shape=jax.ShapeDtypeStruct((b, l, d), BF16),
        compiler_params=_cp(("parallel", "parallel")),
        name="ln_modulate",
    )(h, mod)


def _mm_kernel(x_ref, w_ref, b_ref, o_ref):
    o_ref[...] = (jnp.dot(x_ref[...], w_ref[...], preferred_element_type=F32) + b_ref[...]).astype(o_ref.dtype)


def matmul_bias(x, w, b, out_dtype, tn):
    m, k = x.shape
    n = w.shape[1]
    tm = 1024 if m % 1024 == 0 else 512
    return pl.pallas_call(
        _mm_kernel,
        grid=(m // tm, n // tn),
        in_specs=[pl.BlockSpec((tm, k), lambda i, j: (i, 0)),
                  pl.BlockSpec((k, tn), lambda i, j: (0, j)),
                  pl.BlockSpec((1, tn), lambda i, j: (0, j))],
        out_specs=pl.BlockSpec((tm, tn), lambda i, j: (i, j)),
        out_shape=jax.ShapeDtypeStruct((m, n), out_dtype),
        compiler_params=_cp(("parallel", "parallel")),
        name="matmul_bias",
    )(x, w, b.reshape(1, n).astype(F32))


SUBLANES = 8


def _conv_shifted_copies(g_scr, width, tl):
    pad = (width - 1) // 2
    offs = [HALO + k - pad for k in range(width)]
    n = tl + (max(offs) // SUBLANES) * SUBLANES
    step = 64
    for b in sorted({o % SUBLANES for o in offs} - {0}):
        for s in range(0, n, step):
            m = min(step, n - s)
            g_scr[b, s:s + m, :] = g_scr[0, pl.ds(s + b, m), :]


def _conv_rows(g_scr, w_ref, width, r0, rs):
    pad = (width - 1) // 2
    acc = None
    for k in range(width):
        o = HALO + k - pad
        term = w_ref[k:k + 1, :] * g_scr[o % SUBLANES, pl.ds(r0 + (o // SUBLANES) * SUBLANES, rs), :]
        acc = term if acc is None else acc + term
    return acc


def _conformer_kernel(vc, vp, vn, gc, gp, gn, w_ref, cb_ref, lg_ref, lb_ref, o_ref, g_scr, *, tl, rs):
    i = pl.program_id(1)
    nt = pl.num_programs(1)
    glu = lambda v, g: v[...].astype(F32) * _sigmoid(g[...].astype(F32))
    g_scr[0, HALO:HALO + tl, :] = glu(vc, gc)
    g_scr[0, 0:HALO, :] = jnp.where(i > 0, glu(vp, gp), 0.0)
    g_scr[0, HALO + tl:2 * HALO + tl, :] = jnp.where(i < nt - 1, glu(vn, gn), 0.0)
    _conv_shifted_copies(g_scr, CONV_WIDTH, tl)
    for s in range(tl // rs):
        y = _conv_rows(g_scr, w_ref, CONV_WIDTH, s * rs, rs) + cb_ref[...]
        y = _silu(_ln(y) * lg_ref[...] + lb_ref[...])
        o_ref[s * rs:(s + 1) * rs, :] = y.astype(o_ref.dtype)


def conformer_conv(p_main, conv_w, conv_b, ln_g, ln_b):
    b, l, _ = p_main.shape
    c = CONV_DIM
    tl = min(l, 512)
    rs = 64
    hb = tl // HALO
    nhb = l // HALO
    vblk, gblk = MAIN_CONV // c, MAIN_CONV // c + 1
    cur = lambda cb: pl.BlockSpec((None, tl, c), lambda bi, i: (bi, i, cb))
    prv = lambda cb: pl.BlockSpec((None, HALO, c), lambda bi, i: (bi, jnp.maximum(i * hb - 1, 0), cb))
    nxt = lambda cb: pl.BlockSpec((None, HALO, c), lambda bi, i: (bi, jnp.minimum((i + 1) * hb, nhb - 1), cb))
    vec = pl.BlockSpec((1, c), lambda bi, i: (0, 0))
    return pl.pallas_call(
        functools.partial(_conformer_kernel, tl=tl, rs=rs),
        grid=(b, l // tl),
        in_specs=[cur(vblk), prv(vblk), nxt(vblk), cur(gblk), prv(gblk), nxt(gblk),
                  pl.BlockSpec((CONV_WIDTH, c), lambda bi, i: (0, 0)), vec, vec, vec],
        out_specs=pl.BlockSpec((None, tl, c), lambda bi, i: (bi, i, 0)),
        out_shape=jax.ShapeDtypeStruct((b, l, c), BF16),
        scratch_shapes=[pltpu.VMEM((SUBLANES, tl + 2 * HALO, c), F32)],
        compiler_params=_cp(("parallel", "parallel")),
        name="conformer_conv",
    )(p_main, p_main, p_main, p_main, p_main, p_main, conv_w,
      conv_b.reshape(1, c), ln_g.reshape(1, c), ln_b.reshape(1, c))


def _shortconv_kernel(xc, xp, xn, w_ref, o_ref, g_scr, *, tl, rs):
    i = pl.program_id(1)
    nt = pl.num_programs(1)
    g_scr[0, HALO:HALO + tl, :] = xc[...].astype(F32)
    g_scr[0, 0:HALO, :] = jnp.where(i > 0, xp[...].astype(F32), 0.0)
    g_scr[0, HALO + tl:2 * HALO + tl, :] = jnp.where(i < nt - 1, xn[...].astype(F32), 0.0)
    _conv_shifted_copies(g_scr, DN_CONV_WIDTH, tl)
    is_qk = pl.program_id(2) < 2
    for s in range(tl // rs):
        y = _silu(_conv_rows(g_scr, w_ref, DN_CONV_WIDTH, s * rs, rs))
        for h in range(DN_HEADS):
            yh = y[:, h * DN_HEAD_DIM:(h + 1) * DN_HEAD_DIM]
            yh = jnp.where(is_qk, _l2n(yh), yh)
            o_ref[s * rs:(s + 1) * rs, h * DN_HEAD_DIM:(h + 1) * DN_HEAD_DIM] = yh.astype(o_ref.dtype)


def deltanet_shortconv(p_main, dn_conv_w):
    b, l, _ = p_main.shape
    c = 512
    tl = min(l, 512)
    rs = 64
    hb = tl // HALO
    nhb = l // HALO
    cb0 = MAIN_DN // c
    return pl.pallas_call(
        functools.partial(_shortconv_kernel, tl=tl, rs=rs),
        grid=(b, l // tl, 3),
        in_specs=[pl.BlockSpec((None, tl, c), lambda bi, i, j: (bi, i, cb0 + j)),
                  pl.BlockSpec((None, HALO, c), lambda bi, i, j: (bi, jnp.maximum(i * hb - 1, 0), cb0 + j)),
                  pl.BlockSpec((None, HALO, c), lambda bi, i, j: (bi, jnp.minimum((i + 1) * hb, nhb - 1), cb0 + j)),
                  pl.BlockSpec((DN_CONV_WIDTH, c), lambda bi, i, j: (0, j))],
        out_specs=pl.BlockSpec((None, tl, c), lambda bi, i, j: (bi, i, j)),
        out_shape=jax.ShapeDtypeStruct((b, l, 3 * c), BF16),
        scratch_shapes=[pltpu.VMEM((SUBLANES, tl + 2 * HALO, c), F32)],
        compiler_params=_cp(("parallel", "parallel", "parallel")),
        name="deltanet_shortconv",
    )(p_main, p_main, p_main, dn_conv_w)


def _chunk_masks(rev):
    r = lax.broadcasted_iota(jnp.int32, (CHUNK, CHUNK), 0)
    c = lax.broadcasted_iota(jnp.int32, (CHUNK, CHUNK), 1)
    if rev:
        cum, sx, incl, strict = c >= r, r < c, c >= r, c > r
    else:
        cum, sx, incl, strict = c <= r, r > c, c <= r, c < r
    return cum.astype(F32), sx.astype(F32), incl, strict, (r == c).astype(F32)


def _dn_chunks(chains):
    each = lambda f: [f(c) for c in chains]
    scale = DN_HEAD_DIM ** -0.5
    n = CHUNK
    rows = lambda top, bot: jnp.concatenate([top, bot], axis=0)
    decay = each(lambda c: jnp.where(c["incl"], jnp.exp(c["gs"] - c["gs_row"]), 0.0))
    kb = each(lambda c: c["k"] * c["beta"])
    qs = each(lambda c: c["q"] * scale)
    kq = [_dot_nt(rows(kbi, qi), c["k"]) for c, kbi, qi in zip(chains, kb, qs)]
    m = [jnp.where(c["strict"], r[:n] * di, 0.0) for c, r, di in zip(chains, kq, decay)]
    a = [r[n:] * di for r, di in zip(kq, decay)]
    inv = [c["eye"] - mi for c, mi in zip(chains, m)]
    p = [_dot(mi, mi) for mi in m]
    for _ in range(int(math.log2(CHUNK)) - 2):
        r = [_dot(rows(pi, ii), pi) for pi, ii in zip(p, inv)]
        inv = [ii + ri[n:] for ii, ri in zip(inv, r)]
        p = [ri[:n] for ri in r]
    inv = [ii + _dot(ii, pi) for ii, pi in zip(inv, p)]
    uw = [_dot(ii, jnp.concatenate([c["v"] * c["beta"], kbi * c["eg"]], axis=1)) for c, ii, kbi in zip(chains, inv, kb)]
    hd = DN_HEAD_DIM
    ws = [_dot(rows(r[:, hd:], qi * c["eg"]), c["s"]) for c, r, qi in zip(chains, uw, qs)]
    v_new = [r[:, :hd] - wsi[:n] for r, wsi in zip(uw, ws)]
    o = [wsi[n:] + _dot(ai, vi) for wsi, ai, vi in zip(ws, a, v_new)]
    s_new = [c["s"] * c["etot"] + _dot_tn(c["k"] * c["egt"], vi) for c, vi in zip(chains, v_new)]
    return list(zip(o, s_new))


def _l2n(t):
    return t * lax.rsqrt(jnp.sum(t * t, axis=-1, keepdims=True) + EPS)


def _dn_scan_kernel(xf_ref, bgf_ref, xb_ref, bgb_ref, par_ref, s0_ref, of_ref, ob_ref, sfin_ref, *s_scr, nb):
    i = pl.program_id(0)
    n = pl.num_programs(0)
    hd = DN_HEAD_DIM
    chain = lambda b, d, h: s_scr[(b * 2 + d) * DN_HEADS + h]

    @pl.when(i == 0)
    def _():
        for b in range(nb):
            for d in range(2):
                for h in range(DN_HEADS):
                    chain(b, d, h)[...] = s0_ref[b, d, h]

    chains, sinks = [], []
    for d, (x_ref, bg_ref, o_ref) in enumerate(((xf_ref, bgf_ref, of_ref), (xb_ref, bgb_ref, ob_ref))):
        cum, _, incl, strict, eye = _chunk_masks(rev=bool(d))
        for b in range(nb):
            bg = bg_ref[b]
            beta_all = _sigmoid(bg)
            g_all = -jnp.exp(par_ref[0:1, :]) * _softplus(bg + par_ref[1:2, :])
            gs_all = _dot_hi(cum, g_all)
            tot = gs_all[0:1, :] if d else gs_all[CHUNK - 1:CHUNK, :]
            eg_all, egt_all, etot_all = jnp.exp(gs_all), jnp.exp(tot - gs_all), jnp.exp(tot)
            gs_t = jnp.concatenate([gs_all, jnp.zeros_like(gs_all)], axis=0).T
            for h in range(DN_HEADS):
                cb = d * DN_HEADS + h
                col = 2 * DN_HEADS + cb
                chains.append(dict(
                    q=x_ref[b, :, h * hd:(h + 1) * hd].astype(F32),
                    k=x_ref[b, :, DN_DIM + h * hd:DN_DIM + (h + 1) * hd].astype(F32),
                    v=x_ref[b, :, 2 * DN_DIM + h * hd:2 * DN_DIM + (h + 1) * hd].astype(F32),
                    beta=beta_all[:, cb:cb + 1], gs=gs_all[:, col:col + 1], eg=eg_all[:, col:col + 1],
                    egt=egt_all[:, col:col + 1], etot=etot_all[:, col:col + 1], gs_row=gs_t[col:col + 1, :CHUNK],
                    s=chain(b, d, h)[...], incl=incl, strict=strict, eye=eye))
                sinks.append((o_ref, b, h, chain(b, d, h)))
    for (o_ref, b, h, s_ref), (o, s_new) in zip(sinks, _dn_chunks(chains)):
        o_ref[b, :, h * hd:(h + 1) * hd] = o.astype(o_ref.dtype)
        s_ref[...] = s_new

    @pl.when(i == n - 1)
    def _():
        for b in range(nb):
            for d in range(2):
                for h in range(DN_HEADS):
                    sfin_ref[b, d, h] = chain(b, d, h)[...]


def deltanet_scan(qkv, bg, par, s0):
    b, l, _ = qkv.shape
    n = l // CHUNK
    fwd = lambda w: pl.BlockSpec((b, CHUNK, w), lambda i: (0, i, 0))
    bwd = lambda w: pl.BlockSpec((b, CHUNK, w), lambda i: (0, n - 1 - i, 0))
    st = pl.BlockSpec(s0.shape, lambda i: (0, 0, 0, 0, 0))
    return pl.pallas_call(
        functools.partial(_dn_scan_kernel, nb=b),
        grid=(n,),
        in_specs=[fwd(3 * DN_DIM), fwd(128), bwd(3 * DN_DIM), bwd(128),
                  pl.BlockSpec((8, 128), lambda i: (0, 0)), st],
        out_specs=[fwd(DN_DIM), bwd(DN_DIM), st],
        out_shape=[jax.ShapeDtypeStruct((b, l, DN_DIM), BF16), jax.ShapeDtypeStruct((b, l, DN_DIM), BF16),
                   jax.ShapeDtypeStruct(s0.shape, F32)],
        scratch_shapes=[pltpu.VMEM((DN_HEAD_DIM, DN_HEAD_DIM), F32)] * (b * 2 * DN_HEADS),
        compiler_params=_cp(("arbitrary",)),
        name="deltanet_scan",
    )(qkv, bg, qkv, bg, par, s0)


GLA_SUB = 16


def _gla_chunks(chains):
    row = lax.broadcasted_iota(jnp.int32, (CHUNK, 1), 0)
    lane = lax.broadcasted_iota(jnp.int32, (1, 2 * GLA_DK), 1)
    o_inter = [_dot_nt(c["qp"] * jnp.exp(c["bp"]), c["st"]) for c in chains]
    blocks = [([], []) for _ in chains]
    for blk in range(CHUNK // GLA_SUB):
        i0 = blk * GLA_SUB
        mid = i0 + GLA_SUB // 2
        ri = lax.broadcasted_iota(jnp.int32, (GLA_SUB, CHUNK), 0) + i0
        ci = lax.broadcasted_iota(jnp.int32, (GLA_SUB, CHUNK), 1)
        for c, blks in zip(chains, blocks):
            bp = c["bp"]
            ref = bp[mid:mid + 1, :]
            qt = c["qp"][i0:i0 + GLA_SUB, :] * jnp.exp(bp[i0:i0 + GLA_SUB, :] - ref)
            valid = (row >= i0) if c["rev"] else (row < i0 + GLA_SUB)
            kt = c["kp"] * jnp.exp(jnp.where(valid, ref - bp, 0.0))
            causal = (ci >= ri) if c["rev"] else (ci <= ri)
            qh = [jnp.where((lane >= hh * GLA_DK) & (lane < (hh + 1) * GLA_DK), qt, 0.0) for hh in range(2)]
            both = _dot_nt(jnp.concatenate(qh, axis=0), kt)
            for hh in range(2):
                blks[hh].append(jnp.where(causal, both[hh * GLA_SUB:(hh + 1) * GLA_SUB], 0.0))
    o_intra = [[_dot(jnp.concatenate(blks[hh], axis=0), c["vp"][:, hh * GLA_DV:(hh + 1) * GLA_DV]) for hh in range(2)]
               for c, blks in zip(chains, blocks)]
    srow = lax.broadcasted_iota(jnp.int32, (2 * GLA_DV, 2 * GLA_DK), 0)
    scol = lax.broadcasted_iota(jnp.int32, (2 * GLA_DV, 2 * GLA_DK), 1)
    bd = (srow < GLA_DV) == (scol < GLA_DK)
    out = []
    for c, oi, ox in zip(chains, o_inter, o_intra):
        bp = c["bp"]
        b_last = bp[0:1, :] if c["rev"] else bp[CHUNK - 1:CHUNK, :]
        st_new = jnp.where(bd, c["st"] * jnp.exp(b_last) + _dot_tn(c["vp"], c["kp"] * jnp.exp(b_last - bp)), 0.0)
        out.append((oi + jnp.concatenate(ox, axis=1), st_new))
    return out


def _gla_scan_kernel(qkf, vf, lrf, qkb, vb, lrb, w2_ref, b2_ref, s0_ref, of_ref, ob_ref, sfin_ref, *s_scr, nb):
    i = pl.program_id(0)
    n = pl.num_programs(0)
    npair = GLA_HEADS // 2
    chain = lambda b, d, p: s_scr[(b * 2 + d) * npair + p]

    @pl.when(i == 0)
    def _():
        for b in range(nb):
            for d in range(2):
                for p in range(npair):
                    chain(b, d, p)[...] = s0_ref[b, d, p]

    chains, sinks = [], []
    zs = [[_dot_hi(lr_ref[b], w2_ref[d]) + b2_ref[d] for b in range(nb)]
          for d, lr_ref in enumerate((lrf, lrb))]
    for d, (qk_ref, v_ref, o_ref) in enumerate(((qkf, vf, of_ref), (qkb, vb, ob_ref))):
        cum = _chunk_masks(rev=bool(d))[0]
        for b in range(nb):
            bs = _dot_hi(cum, -_softplus(-zs[d][b]) * (1.0 / GLA_TAU))
            for p in range(npair):
                lo = p * 2 * GLA_DK
                chains.append(dict(
                    qp=qk_ref[b, :, lo:lo + 2 * GLA_DK].astype(F32) * (GLA_DK ** -0.5),
                    kp=qk_ref[b, :, GLA_KDIM + lo:GLA_KDIM + lo + 2 * GLA_DK].astype(F32),
                    vp=v_ref[b, :, p * 2 * GLA_DV:(p + 1) * 2 * GLA_DV].astype(F32),
                    bp=bs[:, lo:lo + 2 * GLA_DK], st=chain(b, d, p)[...], rev=bool(d)))
                sinks.append((o_ref, b, p, chain(b, d, p)))
    for (o_ref, b, p, s_ref), (o, st_new) in zip(sinks, _gla_chunks(chains)):
        o_ref[b, :, p * 2 * GLA_DV:(p + 1) * 2 * GLA_DV] = o.astype(o_ref.dtype)
        s_ref[...] = st_new

    @pl.when(i == n - 1)
    def _():
        for b in range(nb):
            for d in range(2):
                for p in range(npair):
                    sfin_ref[b, d, p] = chain(b, d, p)[...]


def gla_scan(p, col0, small, w2p, b2, s0):
    b, l, _ = p.shape
    n = l // CHUNK
    fwd = lambda w, cb: pl.BlockSpec((b, CHUNK, w), lambda i: (0, i, cb))
    bwd = lambda w, cb: pl.BlockSpec((b, CHUNK, w), lambda i: (0, n - 1 - i, cb))
    st = pl.BlockSpec(s0.shape, lambda i: (0, 0, 0, 0, 0))
    return pl.pallas_call(
        functools.partial(_gla_scan_kernel, nb=b),
        grid=(n,),
        in_specs=[fwd(512, col0), fwd(512, col0 + 1), fwd(128, 0), bwd(512, col0), bwd(512, col0 + 1), bwd(128, 0),
                  pl.BlockSpec((2, 128, GLA_KDIM), lambda i: (0, 0, 0)),
                  pl.BlockSpec((2, 1, GLA_KDIM), lambda i: (0, 0, 0)), st],
        out_specs=[fwd(GLA_VDIM, 0), bwd(GLA_VDIM, 0), st],
        out_shape=[jax.ShapeDtypeStruct((b, l, GLA_VDIM), BF16)] * 2 + [jax.ShapeDtypeStruct(s0.shape, F32)],
        scratch_shapes=[pltpu.VMEM((2 * GLA_DV, 2 * GLA_DK), F32)] * (b * 2 * (GLA_HEADS // 2)),
        compiler_params=_cp(("arbitrary",)),
        name="gla_scan",
    )(p, p, small, p, p, small, w2p, b2, s0)


CM_ROWS = 16
LANE = 128


def _proj_colmajor_kernel(x_ref, w_ref, b_ref, ws_ref, bs_ref, o_ref, os_ref, scr, *, ncol):
    res = jnp.dot(x_ref[...], w_ref[...], preferred_element_type=F32) + b_ref[...]
    for k in range(ncol):
        scr[k] = res[:, k * LANE:(k + 1) * LANE]
    scr[ncol] = jnp.dot(x_ref[...], ws_ref[...], preferred_element_type=F32) + bs_ref[...]
    for c in range(GRID_W):
        for k in range(ncol):
            o_ref[c, :, k * LANE:(k + 1) * LANE] = scr[k, pl.ds(c, CM_ROWS, stride=GRID_W), :].astype(o_ref.dtype)
        os_ref[c] = scr[ncol, pl.ds(c, CM_ROWS, stride=GRID_W), :]


def project_colmajor(u, w, bias, w_small, b_small):
    b, l, d = u.shape
    n = w.shape[1]
    rows = l // GRID_W
    tm = CM_ROWS * GRID_W
    nt = rows // CM_ROWS
    o, os_ = pl.pallas_call(
        functools.partial(_proj_colmajor_kernel, ncol=n // LANE),
        grid=(b, nt),
        in_specs=[pl.BlockSpec((None, tm, d), lambda bi, i: (bi, i, 0)),
                  pl.BlockSpec((d, n), lambda bi, i: (0, 0)), pl.BlockSpec((1, n), lambda bi, i: (0, 0)),
                  pl.BlockSpec((d, LANE), lambda bi, i: (0, 0)), pl.BlockSpec((1, LANE), lambda bi, i: (0, 0))],
        out_specs=[pl.BlockSpec((None, GRID_W, CM_ROWS, n), lambda bi, i: (bi, 0, i, 0)),
                   pl.BlockSpec((None, GRID_W, CM_ROWS, LANE), lambda bi, i: (bi, 0, i, 0))],
        out_shape=[jax.ShapeDtypeStruct((b, GRID_W, rows, n), BF16), jax.ShapeDtypeStruct((b, GRID_W, rows, LANE), F32)],
        scratch_shapes=[pltpu.VMEM((n // LANE + 1, tm, LANE), F32)],
        compiler_params=_cp(("parallel", "parallel")),
        name="project_colmajor",
    )(u, w, bias.reshape(1, n).astype(F32), w_small, b_small.reshape(1, LANE).astype(F32))
    return o.reshape(b, l, n), os_.reshape(b, l, LANE)


def _headnorm_colmajor_kernel(of_ref, ob_ref, z_ref, g_ref, o_ref, scr):
    o = of_ref[...].astype(F32) + ob_ref[...].astype(F32)
    z = z_ref[...].astype(F32)
    for h in range(4):
        sl = slice(h * LANE, (h + 1) * LANE)
        oh = o[:, :, sl]
        y = oh * lax.rsqrt(jnp.mean(oh * oh, axis=-1, keepdims=True) + EPS) * g_ref[...] * _silu(z[:, :, sl])
        for c in range(GRID_W):
            scr[h, pl.ds(c, CM_ROWS, stride=GRID_W), :] = y[c]
    for h in range(4):
        o_ref[:, h * LANE:(h + 1) * LANE] = scr[h].astype(o_ref.dtype)


def gated_head_norm_colmajor(o_f, o_b, p_gla, z_blk, g):
    b, l, c = o_f.shape
    rows = l // GRID_W
    tm = CM_ROWS * GRID_W
    cm = lambda t: t.reshape(b, GRID_W, rows, t.shape[-1])
    blk = lambda cb: pl.BlockSpec((None, GRID_W, CM_ROWS, c), lambda bi, i: (bi, 0, i, cb))
    return pl.pallas_call(
        _headnorm_colmajor_kernel,
        grid=(b, rows // CM_ROWS),
        in_specs=[blk(0), blk(0), blk(z_blk), pl.BlockSpec((1, LANE), lambda bi, i: (0, 0))],
        out_specs=pl.BlockSpec((None, tm, c), lambda bi, i: (bi, i, 0)),
        out_shape=jax.ShapeDtypeStruct((b, l, c), BF16),
        scratch_shapes=[pltpu.VMEM((4, tm, LANE), F32)],
        compiler_params=_cp(("parallel", "parallel")),
        name="gated_head_norm_colmajor",
    )(cm(o_f), cm(o_b), cm(p_gla), g.reshape(1, LANE).astype(F32))


def _headnorm_kernel(of_ref, ob_ref, z_ref, g_ref, o_ref):
    o = of_ref[...].astype(F32) + ob_ref[...].astype(F32)
    z = z_ref[...].astype(F32)
    for h in range(4):
        sl = slice(h * 128, (h + 1) * 128)
        oh = o[:, sl]
        oh = oh * lax.rsqrt(jnp.mean(oh * oh, axis=-1, keepdims=True) + EPS) * g_ref[...]
        o_ref[:, sl] = (oh * _silu(z[:, sl])).astype(o_ref.dtype)


def gated_head_norm(o_f, o_b, z_arr, z_blk, g):
    b, l, c = o_f.shape
    tl = min(l, 512)
    blk = pl.BlockSpec((None, tl, c), lambda bi, i: (bi, i, 0))
    return pl.pallas_call(
        _headnorm_kernel,
        grid=(b, l // tl),
        in_specs=[blk, blk, pl.BlockSpec((None, tl, c), lambda bi, i: (bi, i, z_blk)),
                  pl.BlockSpec((1, 128), lambda bi, i: (0, 0))],
        out_specs=blk,
        out_shape=jax.ShapeDtypeStruct((b, l, c), BF16),
        compiler_params=_cp(("parallel", "parallel")),
        name="gated_head_norm",
    )(o_f, o_b, z_arr, g.reshape(1, 128).astype(F32))


def _dft_tables(n):
    ang = 2.0 * np.pi * (np.outer(np.arange(n), np.arange(n)) % n) / n
    return np.cos(ang), np.sin(ang)


def _fnet_small_kernel(x_ref, wc_ref, cl_ref, sl_ref, o_ref):
    y = jnp.dot(x_ref[...], wc_ref[...], preferred_element_type=F32)
    out = _dot(cl_ref[...], y[:, :FN_GROUP_DIM]) + _dot(sl_ref[...], y[:, FN_GROUP_DIM:])
    o_ref[...] = out.astype(o_ref.dtype)


def _fnet_big_kernel(x_ref, wc_ref, f1_ref, twc_ref, tws_ref, f2_ref, o_ref, y_scr, yi_scr, b_scr, *, n1):
    n2 = FN_GROUP_DIM
    l = n1 * n2
    rb = 512

    def step0(r, c):
        r0 = pl.multiple_of(r * rb, rb)
        y = jnp.dot(x_ref[pl.ds(r0, rb), :], wc_ref[...], preferred_element_type=F32)
        y_scr[pl.ds(r0, rb), :] = y[:, :n2]
        yi_scr[pl.ds(r0, rb), :] = y[:, n2:]
        return c

    lax.fori_loop(0, l // rb, step0, 0)

    def step1(j, c):
        mr = _dot(f1_ref[...], y_scr[pl.ds(j, n1, stride=n2), :])
        mi = _dot(f1_ref[...], yi_scr[pl.ds(j, n1, stride=n2), :])
        ar = mr[:n1] + mi[n1:]
        ai = mi[:n1] - mr[n1:]
        tc, ts = twc_ref[j], tws_ref[j]
        b_scr[pl.ds(j, n1, stride=2 * n2), :] = ar * tc + ai * ts
        b_scr[pl.ds(n2 + j, n1, stride=2 * n2), :] = ai * tc - ar * ts
        return c

    lax.fori_loop(0, n2, step1, 0, unroll=8)

    def step2(k1, c):
        bk = b_scr[pl.ds(pl.multiple_of(k1 * 2 * n2, 2 * n2), 2 * n2), :]
        y_scr[pl.ds(k1, n2, stride=n1), :] = _dot(f2_ref[...], bk)
        return c

    lax.fori_loop(0, n1, step2, 0, unroll=8)

    def step3(r, c):
        r0 = pl.multiple_of(r * rb, rb)
        o_ref[pl.ds(r0, rb), :] = y_scr[pl.ds(r0, rb), :].astype(o_ref.dtype)
        return c

    lax.fori_loop(0, l // rb, step3, 0)


def fourier_mix(p_main):
    b, l, _ = p_main.shape
    gd = FN_GROUP_DIM
    cc, sc = _dft_tables(gd)
    wc = jnp.asarray(np.concatenate([cc, -sc], axis=1) / math.sqrt(gd), BF16)
    x_spec = pl.BlockSpec((None, l, gd), lambda bi, g: (bi, 0, MAIN_FN // gd + g))
    o_spec = pl.BlockSpec((None, l, gd), lambda bi, g: (bi, 0, g))
    full = lambda shape: pl.BlockSpec(shape, lambda bi, g: (0,) * len(shape))
    out_shape = jax.ShapeDtypeStruct((b, l, FN_DIM), BF16)
    if l <= 512:
        cl, sl = _dft_tables(l)
        scale = 1.0 / math.sqrt(l)
        return pl.pallas_call(
            _fnet_small_kernel, grid=(b, FN_GROUPS),
            in_specs=[x_spec, full((gd, 2 * gd)), full((l, l)), full((l, l))],
            out_specs=o_spec, out_shape=out_shape,
            compiler_params=_cp(("parallel", "parallel")), name="fourier_mix_small",
        )(p_main, wc, jnp.asarray(cl * scale, BF16), jnp.asarray(sl * scale, BF16))
    n1, n2 = l // gd, gd
    c1, s1 = _dft_tables(n1)
    c2, s2 = _dft_tables(n2)
    f1 = jnp.asarray(np.concatenate([c1, s1], axis=0), BF16)
    f2 = jnp.asarray(np.concatenate([c2, s2], axis=1) / math.sqrt(l), BF16)
    ang = 2.0 * np.pi * np.outer(np.arange(n2), np.arange(n1)) / l
    twc = jnp.asarray(np.broadcast_to(np.cos(ang)[:, :, None], (n2, n1, gd)), F32)
    tws = jnp.asarray(np.broadcast_to(np.sin(ang)[:, :, None], (n2, n1, gd)), F32)
    return pl.pallas_call(
        functools.partial(_fnet_big_kernel, n1=n1), grid=(b, FN_GROUPS),
        in_specs=[x_spec, full((gd, 2 * gd)), full((2 * n1, n1)), full((n2, n1, gd)), full((n2, n1, gd)),
                  full((n2, 2 * n2))],
        out_specs=o_spec, out_shape=out_shape,
        scratch_shapes=[pltpu.VMEM((l, gd), F32), pltpu.VMEM((l, gd), F32), pltpu.VMEM((n1 * 2 * n2, gd), F32)],
        compiler_params=_cp(("parallel", "parallel")), name="fourier_mix_big",
    )(p_main, wc, f1, twc, tws, f2)


def _merge_kernel(fa, fb, fc, fd, gt_ref, wb_ref, wo_ref, h_ref, mod_ref, lng_ref, lnb_ref, wr_ref,
                  h1_ref, v_ref, lg_ref):
    half = h_ref.shape[0] // 2
    for r0 in (0, half):
        rs = slice(r0, r0 + half)
        acc = None
        for n, f_ref in enumerate((fa, fb, fc, fd)):
            proj = jnp.dot(f_ref[rs, :], wb_ref[n], preferred_element_type=F32)
            term = _sigmoid(gt_ref[rs, n * D_MODEL:(n + 1) * D_MODEL].astype(F32)) * proj
            acc = term if acc is None else acc + term
        y = _dot(acc, wo_ref[...])
        h1 = _ln(DEEPNORM_ALPHA * h_ref[rs, :] + mod_ref[2:3, :] * y) * lng_ref[...] + lnb_ref[...]
        h1_ref[rs, :] = h1
        v = _ln(h1) * (1.0 + mod_ref[4:5, :]) + mod_ref[3:4, :]
        v_ref[rs, :] = _pack_bf16_pair(v)
        lg_ref[:, rs] = lax.dot_general(wr_ref[...], v, (((1,), (1,)), ((), ())), precision=HI,
                                        preferred_element_type=F32)


def merge_branches(feats, p_main, w_branch, w_o, h, mod, ln_g, ln_b, w_router_t):
    b, l, d = h.shape
    tm = 256
    nt = l // tm
    fblk = pl.BlockSpec((None, tm, BRANCH_DIM), lambda bi, i: (bi, i, 0))
    hblk = pl.BlockSpec((None, tm, d), lambda bi, i: (bi, i, 0))
    vec = pl.BlockSpec((1, d), lambda bi, i: (0, 0))
    return pl.pallas_call(
        _merge_kernel,
        grid=(b, l // tm),
        in_specs=[fblk, fblk, fblk, fblk,
                  pl.BlockSpec((None, tm, N_BRANCH * d), lambda bi, i: (bi, i, MAIN_GATE // (N_BRANCH * d))),
                  pl.BlockSpec((N_BRANCH, BRANCH_DIM, d), lambda bi, i: (0, 0, 0)),
                  pl.BlockSpec((d, d), lambda bi, i: (0, 0)),
                  hblk, pl.BlockSpec((None, 8, d), lambda bi, i: (bi, 0, 0)), vec, vec,
                  pl.BlockSpec((N_EXPERTS, d), lambda bi, i: (0, 0))],
        out_specs=[hblk, pl.BlockSpec((None, tm, d // 2), lambda bi, i: (bi, i, 0)),
                   pl.BlockSpec((N_EXPERTS, tm), lambda bi, i: (0, bi * nt + i))],
        out_shape=[jax.ShapeDtypeStruct((b, l, d), F32), jax.ShapeDtypeStruct((b, l, d // 2), jnp.int32),
                   jax.ShapeDtypeStruct((N_EXPERTS, b * l), F32)],
        compiler_params=_cp(("parallel", "parallel")),
        name="merge_branches",
    )(*feats, p_main, w_branch, w_o, h, mod, ln_g.reshape(1, d), ln_b.reshape(1, d), w_router_t)


def _first_argmax(x, axis, size):
    m = jnp.max(x, axis=axis, keepdims=True)
    idx = lax.broadcasted_iota(jnp.int32, x.shape, axis)
    first = jnp.min(jnp.where(x == m, idx, size), axis=axis, keepdims=True)
    return m, idx == first


def _route_kernel(lg_ref, rb_ref, pos_ref, w_ref, cnt_ref, cnt_scr, *, capacity):
    i = pl.program_id(0)

    @pl.when(i == 0)
    def _():
        cnt_scr[...] = jnp.zeros_like(cnt_scr)

    tm = lg_ref.shape[-1]
    per = N_EXPERTS // N_EXPERT_GROUPS
    scores = _sigmoid(lg_ref[...])
    biased = scores + rb_ref[...]
    x3 = biased.reshape(N_EXPERT_GROUPS, per, tm)
    m1, hit = _first_argmax(x3, 1, per)
    m2 = jnp.max(jnp.where(hit, -jnp.inf, x3), axis=1, keepdims=True)
    gscore = (m1 + m2).reshape(N_EXPERT_GROUPS, tm)
    gsel = jnp.zeros(gscore.shape, F32)
    for _ in range(TOP_GROUPS):
        _, hit = _first_argmax(gscore, 0, N_EXPERT_GROUPS)
        gsel = jnp.where(hit, 1.0, gsel)
        gscore = jnp.where(hit, -jnp.inf, gscore)
    masked = jnp.where(gsel.reshape(N_EXPERT_GROUPS, 1, tm) > 0.0, x3, -jnp.inf).reshape(N_EXPERTS, tm)
    sel = jnp.zeros(masked.shape, F32)
    for _ in range(TOP_K):
        _, hit = _first_argmax(masked, 0, N_EXPERTS)
        sel = jnp.where(hit, 1.0, sel)
        masked = jnp.where(hit, -jnp.inf, masked)
    w = sel * scores
    comb = w / jnp.sum(w, axis=0, keepdims=True) * ROUTED_SCALE
    tr = lax.broadcasted_iota(jnp.int32, (tm, tm), 0)
    tc = lax.broadcasted_iota(jnp.int32, (tm, tm), 1)
    rank = cnt_scr[:, 0:1] + _dot(sel, (tr < tc).astype(F32))
    slot = lax.broadcasted_iota(jnp.int32, sel.shape, 0).astype(F32) * float(capacity) + rank
    pos_rows, w_rows = [], []
    remaining = sel
    for _ in range(TOP_K):
        _, hit = _first_argmax(remaining, 0, N_EXPERTS)
        pos_rows.append(jnp.sum(jnp.where(hit, slot, 0.0), axis=0, keepdims=True))
        w_rows.append(jnp.sum(jnp.where(hit, comb, 0.0), axis=0, keepdims=True))
        remaining = jnp.where(hit, 0.0, remaining)
    zero = jnp.zeros((8 - TOP_K, tm), F32)
    pos_ref[...] = jnp.concatenate(pos_rows + [zero], axis=0).astype(jnp.int32)
    w_ref[...] = jnp.concatenate(w_rows + [zero], axis=0)
    cnt_scr[...] = cnt_scr[...] + jnp.sum(sel, axis=1, keepdims=True)

    @pl.when(i == pl.num_programs(0) - 1)
    def _():
        cnt_ref[...] = cnt_scr[...]


def moe_route(logits_t, b_router):
    e, t = logits_t.shape
    tm = 256
    blk = pl.BlockSpec((e, tm), lambda i: (0, i))
    oblk = pl.BlockSpec((8, tm), lambda i: (0, i))
    return pl.pallas_call(
        functools.partial(_route_kernel, capacity=t), grid=(t // tm,),
        in_specs=[blk, pl.BlockSpec((e, tm), lambda i: (0, 0))],
        out_specs=[oblk, oblk, pl.BlockSpec((e, 128), lambda i: (0, 0))],
        out_shape=[jax.ShapeDtypeStruct((8, t), jnp.int32), jax.ShapeDtypeStruct((8, t), F32),
                   jax.ShapeDtypeStruct((e, 128), F32)],
        scratch_shapes=[pltpu.VMEM((e, 128), F32)],
        compiler_params=_cp(("arbitrary",)), name="moe_route",
    )(logits_t, jnp.broadcast_to(b_router.astype(F32)[:, None], (e, tm)))


def _sc_workers():
    info = plsc.get_sparse_core_info()
    return info.num_cores, info.num_subcores


def _sc_chunk(per_worker):
    return max(c for c in range(8, 129, 8) if per_worker % c == 0)


def sc_scatter_rows(rows, pos, n_out):
    t, w = rows.shape
    nc, ns = _sc_workers()
    nw = nc * ns
    per_w = t // nw
    ch = _sc_chunk(per_w)
    nch = per_w // ch
    pos_w = pos[:TOP_K].reshape(TOP_K, nw, nch, ch).transpose(1, 2, 0, 3)
    mesh = plsc.VectorSubcoreMesh(core_axis_name="c", subcore_axis_name="s")

    @functools.partial(
        pl.kernel, mesh=mesh, out_type=jax.ShapeDtypeStruct((n_out, w), jnp.int32),
        scratch_types=[pltpu.VMEM((TOP_K, ch), jnp.int32), pltpu.VMEM((ch, w), jnp.int32), pltpu.SemaphoreType.DMA])
    def scatter(rows_hbm, pos_hbm, out_hbm, idx_v, rows_v, sem):
        wid = lax.axis_index("s") * nc + lax.axis_index("c")

        @pl.loop(0, nch)
        def _(j):
            pltpu.sync_copy(pos_hbm.at[wid, j], idx_v)
            pltpu.sync_copy(rows_hbm.at[pl.ds(wid * per_w + j * ch, ch)], rows_v)
            copies = [pltpu.async_copy(rows_v, out_hbm.at[idx_v.at[k]], sem) for k in range(TOP_K)]
            for cp in copies:
                cp.wait()

    return scatter(rows, pos_w)


def sc_gather_rows(table, idx):
    m = idx.shape[0]
    w = table.shape[1]
    nc, ns = _sc_workers()
    nw = nc * ns
    per_w = m // nw
    ch = _sc_chunk(per_w)
    nch = per_w // ch
    mesh = plsc.VectorSubcoreMesh(core_axis_name="c", subcore_axis_name="s")

    @functools.partial(
        pl.kernel, mesh=mesh, out_type=jax.ShapeDtypeStruct((m, w), jnp.int32),
        scratch_types=[pltpu.VMEM((ch,), jnp.int32), pltpu.VMEM((ch, w), jnp.int32), pltpu.SemaphoreType.DMA])
    def gather(table_hbm, idx_hbm, out_hbm, idx_v, rows_v, sem):
        wid = lax.axis_index("s") * nc + lax.axis_index("c")

        @pl.loop(0, nch)
        def _(j):
            off = wid * per_w + j * ch
            pltpu.sync_copy(idx_hbm.at[pl.ds(off, ch)], idx_v)
            pltpu.async_copy(table_hbm.at[idx_v], rows_v, sem).wait()
            pltpu.sync_copy(rows_v, out_hbm.at[pl.ds(off, ch)])

    return gather(table, idx)


MOE_TM = 512


def _gmm_kernel(te_ref, tj_ref, na_ref, x_ref, wg_ref, wu_ref, wd_ref, y_ref, wg_s, wu_s, wd_s):
    i = pl.program_id(0)

    @pl.when(i < na_ref[0])
    def _():
        @pl.when(tj_ref[i] == 0)
        def _():
            wg_s[...] = wg_ref[...].astype(BF16)
            wu_s[...] = wu_ref[...].astype(BF16)
            wd_s[...] = wd_ref[...].astype(BF16)

        half = D_MODEL // 2
        lo, hi = _unpack_bf16_pair(x_ref[...])
        gate = _dot(lo, wg_s[:half, :]) + _dot(hi, wg_s[half:, :])
        up = _dot(lo, wu_s[:half, :]) + _dot(hi, wu_s[half:, :])
        y_ref[...] = _pack_bf16_pair(_dot(_silu(gate) * up, wd_s[...]))


def moe_grouped_experts(xs, counts, w_gate, w_up, w_down, capacity):
    n_e, d, hdim = w_gate.shape
    blocks_per_e = capacity // MOE_TM
    n_tiles = capacity * TOP_K // MOE_TM + n_e
    tiles_e = (counts.astype(jnp.int32) + MOE_TM - 1) // MOE_TM
    ends = jnp.cumsum(tiles_e)
    n_active = ends[-1]
    step = jnp.minimum(jnp.arange(n_tiles, dtype=jnp.int32), n_active - 1)
    owned = step[:, None] >= ends[None, :]
    te = jnp.sum(owned, axis=1).astype(jnp.int32)
    tj = step - jnp.sum(jnp.where(owned, tiles_e[None, :], 0), axis=1).astype(jnp.int32)
    row_blk = lambda i, te_r, tj_r, na_r: (te_r[i] * blocks_per_e + tj_r[i], 0)
    wmap = lambda i, te_r, tj_r, na_r: (te_r[i], 0, 0)
    return pl.pallas_call(
        _gmm_kernel,
        grid_spec=pltpu.PrefetchScalarGridSpec(
            num_scalar_prefetch=3, grid=(n_tiles,),
            in_specs=[pl.BlockSpec((MOE_TM, d // 2), row_blk),
                      pl.BlockSpec((None, d, hdim), wmap), pl.BlockSpec((None, d, hdim), wmap),
                      pl.BlockSpec((None, hdim, d), wmap)],
            out_specs=pl.BlockSpec((MOE_TM, d // 2), row_blk),
            scratch_shapes=[pltpu.VMEM((d, hdim), BF16), pltpu.VMEM((d, hdim), BF16), pltpu.VMEM((hdim, d), BF16)]),
        out_shape=jax.ShapeDtypeStruct(xs.shape, jnp.int32),
        compiler_params=_cp(("arbitrary",)),
        name="moe_grouped_experts",
    )(te, tj, n_active.reshape(1).astype(jnp.int32), xs, w_gate, w_up, w_down)


def _moe_out_kernel(v_ref, g_ref, w_ref, sg_ref, su_ref, sd_ref, h_ref, mod_ref, nmod_ref, lng_ref, lnb_ref,
                    h2_ref, u_ref):
    half = D_MODEL // 2
    lo, hi = _unpack_bf16_pair(v_ref[...])
    gate = _dot(lo, sg_ref[:half, :]) + _dot(hi, sg_ref[half:, :])
    up = _dot(lo, su_ref[:half, :]) + _dot(hi, su_ref[half:, :])
    f = _dot(_silu(gate) * up, sd_ref[...])
    acc_lo = acc_hi = None
    for k in range(TOP_K):
        ylo, yhi = _unpack_bf16_pair(g_ref[k])
        wk = w_ref[:, k:k + 1]
        acc_lo = ylo * wk if acc_lo is None else acc_lo + ylo * wk
        acc_hi = yhi * wk if acc_hi is None else acc_hi + yhi * wk
    f = f + jnp.concatenate([acc_lo, acc_hi], axis=1)
    h2 = _ln(DEEPNORM_ALPHA * h_ref[...] + mod_ref[5:6, :] * f) * lng_ref[...] + lnb_ref[...]
    h2_ref[...] = h2
    u_ref[...] = (_ln(h2) * (1.0 + nmod_ref[1:2, :]) + nmod_ref[0:1, :]).astype(u_ref.dtype)


def moe_output(vp, gathered, w_tok, tok0, ws_gate, ws_up, ws_down, h1, mod, next_mod, ln_g, ln_b):
    b, l, d = h1.shape
    tm = 256
    nt = l // tm
    blk0 = tok0 // tm
    xblk = pl.BlockSpec((None, tm, d), lambda bi, i: (bi, i, 0))
    pblk = pl.BlockSpec((None, tm, d // 2), lambda bi, i: (bi, i, 0))
    vec = pl.BlockSpec((1, d), lambda bi, i: (0, 0))
    modblk = pl.BlockSpec((None, 8, d), lambda bi, i: (bi, 0, 0))
    return pl.pallas_call(
        _moe_out_kernel,
        grid=(b, nt),
        in_specs=[pblk, pl.BlockSpec((TOP_K, tm, d // 2), lambda bi, i: (0, blk0 + bi * nt + i, 0)),
                  pl.BlockSpec((tm, 8), lambda bi, i: (blk0 + bi * nt + i, 0)),
                  pl.BlockSpec((d, EXPERT_DIM), lambda bi, i: (0, 0)),
                  pl.BlockSpec((d, EXPERT_DIM), lambda bi, i: (0, 0)),
                  pl.BlockSpec((EXPERT_DIM, d), lambda bi, i: (0, 0)),
                  xblk, modblk, modblk, vec, vec],
        out_specs=[xblk, xblk],
        out_shape=[jax.ShapeDtypeStruct((b, l, d), F32), jax.ShapeDtypeStruct((b, l, d), BF16)],
        compiler_params=_cp(("parallel", "parallel")),
        name="moe_output",
    )(vp, gathered, w_tok, ws_gate, ws_up, ws_down, h1, mod, next_mod, ln_g.reshape(1, d), ln_b.reshape(1, d))


def moe_layer(streams, b_router, w_gate, w_up, w_down, ws_gate, ws_up, ws_down, ln_g, ln_b):
    half = D_MODEL // 2
    vps = [s[0].reshape(-1, half) for s in streams]
    sizes = [v.shape[0] for v in vps]
    t = sum(sizes)
    vp_all = vps[0] if len(vps) == 1 else jnp.concatenate(vps, axis=0)
    lg_all = streams[0][1] if len(vps) == 1 else jnp.concatenate([s[1] for s in streams], axis=1)
    pos, w_rows, counts = moe_route(lg_all, b_router)
    xs = sc_scatter_rows(vp_all, pos, N_EXPERTS * t)
    ys = moe_grouped_experts(xs, counts[:, 0], w_gate, w_up, w_down, t)
    gathered = sc_gather_rows(ys, pos[:TOP_K].reshape(-1)).reshape(TOP_K, t, half)
    w_tok = w_rows.T
    outs, tok0 = [], 0
    for (vp, _, h1, mod, next_mod), n in zip(streams, sizes):
        outs.append(moe_output(vp, gathered, w_tok, tok0, ws_gate, ws_up, ws_down, h1, mod, next_mod, ln_g, ln_b))
        tok0 += n
    return outs


def _layer_weights(l, w_in, b_in, dn_a_log, dn_dt_bias, gla_w2, gla_b2):
    w, bvec = w_in[l], b_in[l]
    cols = lambda a, n: (w[:, a:a + n], bvec[a:a + n])
    parts = (cols(_GATE0, 4096), cols(_DN0, 2048), cols(_CONV0, 1024), cols(_FN0, 512), cols(_GLA0, 1536))
    w_main = jnp.concatenate([q[0] for q in parts], axis=1).astype(BF16)
    b_main = jnp.concatenate([q[1] for q in parts])
    small = (cols(2048, 16), cols(_GLA0 + 1536, 32))
    w_small = jnp.pad(jnp.concatenate([q[0] for q in small], axis=1), ((0, 0), (0, 128 - 48))).astype(BF16)
    b_small = jnp.pad(jnp.concatenate([q[1] for q in small]), (0, 128 - 48))
    par = jnp.zeros((8, 128), F32)
    par = par.at[0, 8:16].set(dn_a_log[l].reshape(-1)).at[1, 8:16].set(dn_dt_bias[l].reshape(-1))
    w2p = jnp.zeros((2, 128, GLA_KDIM), F32)
    w2p = w2p.at[0, 16:32].set(gla_w2[l, 0]).at[1, 32:48].set(gla_w2[l, 1])
    return dict(w_main=w_main, b_main=b_main, w_small=w_small, b_small=b_small, par=par, w2p=w2p,
                b2=gla_b2[l].reshape(2, 1, GLA_KDIM).astype(F32))


def _project(u, lw, with_gla):
    b, l, d = u.shape
    flat = u.reshape(b * l, d)
    n = MAIN_COLS if with_gla else MAIN_GLA
    p_main = matmul_bias(flat, lw["w_main"][:, :n], lw["b_main"][:n], BF16, 1536).reshape(b, l, n)
    small = matmul_bias(flat, lw["w_small"], lw["b_small"], F32, 128).reshape(b, l, 128)
    return p_main, small


def kernel(x, c, ctx, c_ctx, w_mod, b_mod, w_in, b_in, dn_conv_w, dn_a_log, dn_dt_bias, dn_norm_g, gla_w2, gla_b2, gla_norm_g, conv_w, conv_b, conv_ln_g, conv_ln_b, w_branch, w_o, ln_g, ln_b, w_router, b_router, w_gate, w_up, w_down, ws_gate, ws_up, ws_down):
    batch, seq, d = x.shape
    c8 = jnp.zeros((8, d), F32).at[:batch].set(c).at[batch].set(c_ctx)
    mods = adaln_vectors(c8, w_mod, b_mod).reshape(DEPTH, 8, 6, d)
    zrow = jnp.zeros((DEPTH, 2, d), F32)
    mod_x = [jnp.concatenate([mods[l, :batch], jnp.broadcast_to(zrow[l][None], (batch, 2, d))], axis=1)
             for l in range(DEPTH)]
    mod_c = [jnp.broadcast_to(jnp.concatenate([mods[l, batch], zrow[l]], axis=0)[None], (batch, 8, d))
             for l in range(DEPTH)]

    h, hc = x, ctx
    u_x, u_c = ln_modulate(h, mod_x[0]), ln_modulate(hc, mod_c[0])
    for l in range(DEPTH):
        lw = _layer_weights(l, w_in, b_in, dn_a_log, dn_dt_bias, gla_w2, gla_b2)
        px = _project(u_x, lw, False)
        pc = _project(u_c, lw, True)
        pg_x, lr_x = project_colmajor(u_x, lw["w_main"][:, MAIN_GLA:], lw["b_main"][MAIN_GLA:], lw["w_small"],
                                      lw["b_small"])
        qkv_c, qkv_x = deltanet_shortconv(pc[0], dn_conv_w[l]), deltanet_shortconv(px[0], dn_conv_w[l])
        s0 = jnp.zeros((batch, 2, DN_HEADS, DN_HEAD_DIM, DN_HEAD_DIM), F32)
        ocf, ocb, s_c = deltanet_scan(qkv_c, pc[1], lw["par"], s0)
        oxf, oxb, _ = deltanet_scan(qkv_x, px[1], lw["par"], s_c)
        dn_x = gated_head_norm(oxf, oxb, px[0], MAIN_DN // 512 + 3, dn_norm_g[l])
        dn_c = gated_head_norm(ocf, ocb, pc[0], MAIN_DN // 512 + 3, dn_norm_g[l])
        g0 = jnp.zeros((batch, 2, 2, 2 * GLA_DV, 2 * GLA_DK), F32)
        gcf, gcb, gs_c = gla_scan(pc[0], MAIN_GLA // 512, pc[1], lw["w2p"], lw["b2"], g0)
        gxf, gxb, _ = gla_scan(pg_x, 0, lr_x, lw["w2p"], lw["b2"], gs_c)
        gla_x = gated_head_norm_colmajor(gxf, gxb, pg_x, 2, gla_norm_g[l])
        gla_c = gated_head_norm(gcf, gcb, pc[0], MAIN_GLA // 512 + 2, gla_norm_g[l])
        wb = w_branch[l].astype(BF16)
        wo = w_o[l].astype(BF16)
        wrt = w_router[l].T
        last = l == DEPTH - 1
        nxt = min(l + 1, DEPTH - 1)
        streams = []
        for (p, dn_f, gla_f, hh, mod, nmod) in ((px, dn_x, gla_x, h, mod_x[l], mod_x[nxt]),
                                               (pc, dn_c, gla_c, hc, mod_c[l], mod_c[nxt]))[:1 if last else 2]:
            feats = (conformer_conv(p[0], conv_w[l], conv_b[l], conv_ln_g[l], conv_ln_b[l]), dn_f,
                     fourier_mix(p[0]), gla_f)
            h1, vp, lg = merge_branches(feats, p[0], wb, wo, hh, mod, ln_g[l, 0], ln_b[l, 0], wrt)
            streams.append((vp, lg, h1, mod, nmod))
        outs = moe_layer(streams, b_router[l], w_gate[l], w_up[l], w_down[l], ws_gate[l], ws_up[l], ws_down[l],
                         ln_g[l, 1], ln_b[l, 1])
        (h, u_x) = outs[0]
        if not last:
            (hc, u_c) = outs[1]
    return h
```

```python
import functools
import math

import jax
import jax.numpy as jnp
import numpy as np
from jax import lax
from jax.experimental import pallas as pl
from jax.experimental.pallas import tpu as pltpu
from jax.experimental.pallas import tpu_sc as plsc

F32 = jnp.float32
BF16 = jnp.bfloat16
HI = lax.Precision.HIGHEST

D_MODEL = 1024
DEPTH = 4
GRID_W = 64
CHUNK = 64
EPS = 1e-6
CONV_DIM = 512
CONV_WIDTH = 31
DN_HEADS = 4
DN_HEAD_DIM = 128
DN_DIM = 512
DN_CONV_WIDTH = 5
FN_GROUPS = 4
FN_GROUP_DIM = 128
FN_DIM = 512
GLA_HEADS = 4
GLA_DK = 64
GLA_DV = 128
GLA_KDIM = 256
GLA_VDIM = 512
GLA_GATE_RANK = 16
GLA_TAU = 16.0
N_BRANCH = 4
BRANCH_DIM = 512
N_EXPERTS = 64
N_EXPERT_GROUPS = 8
TOP_GROUPS = 4
TOP_K = 6
EXPERT_DIM = 256
ROUTED_SCALE = 2.5
DEEPNORM_ALPHA = (2 * DEPTH) ** 0.25

_DN0 = 0
_GLA0 = 4 * DN_DIM + 4 * DN_HEADS
_CONV0 = _GLA0 + 2 * GLA_KDIM + 2 * GLA_VDIM + 2 * GLA_GATE_RANK
_FN0 = _CONV0 + 2 * CONV_DIM
_GATE0 = _FN0 + FN_DIM
IN_DIM = _GATE0 + N_BRANCH * D_MODEL

MAIN_GATE, MAIN_DN, MAIN_CONV, MAIN_FN, MAIN_GLA = 0, 4096, 6144, 7168, 7680
MAIN_COLS = 9216
HALO = 16
VMEM_LIMIT = 56 * 1024 * 1024


def _cp(sem, vmem=None):
    return pltpu.CompilerParams(dimension_semantics=sem, vmem_limit_bytes=vmem or VMEM_LIMIT)


def _sigmoid(x):
    return 0.5 * jnp.tanh(0.5 * x) + 0.5


def _silu(x):
    return x * _sigmoid(x)


def _softplus(x):
    return jnp.maximum(x, 0.0) + jnp.log(1.0 + jnp.exp(-jnp.abs(x)))


def _ln(x):
    mu = jnp.mean(x, axis=-1, keepdims=True)
    xc = x - mu
    var = jnp.mean(xc * xc, axis=-1, keepdims=True)
    return xc * lax.rsqrt(var + EPS)


def _dot(a, b):
    return jnp.dot(a.astype(BF16), b.astype(BF16), preferred_element_type=F32)


def _dot_nt(a, b):
    return lax.dot_general(a.astype(BF16), b.astype(BF16), (((1,), (1,)), ((), ())), preferred_element_type=F32)


def _dot_tn(a, b):
    return lax.dot_general(a.astype(BF16), b.astype(BF16), (((0,), (0,)), ((), ())), preferred_element_type=F32)


def _dot_hi(a, b):
    return jnp.dot(a, b, precision=HI, preferred_element_type=F32)


_HI16 = -65536


def _pack_bf16_pair(x):
    n = x.shape[-1] // 2
    bits = lambda t: lax.bitcast_convert_type(t.astype(BF16).astype(F32), jnp.int32)
    return (bits(x[:, n:]) & _HI16) | ((bits(x[:, :n]) >> 16) & 0xFFFF)


def _unpack_bf16_pair(w):
    lo = lax.bitcast_convert_type(w << 16, F32)
    hi = lax.bitcast_convert_type(w & _HI16, F32)
    return lo, hi


def _mod_kernel(c_ref, w_ref, b_ref, o_ref):
    o_ref[...] = _dot_hi(_silu(c_ref[...]), w_ref[...]) + b_ref[...]


def adaln_vectors(c8, w_mod, b_mod):
    depth, d, n = w_mod.shape
    tn = 1536
    return pl.pallas_call(
        _mod_kernel,
        grid=(depth, n // tn),
        in_specs=[pl.BlockSpec((8, d), lambda l, j: (0, 0)),
                  pl.BlockSpec((None, d, tn), lambda l, j: (l, 0, j)),
                  pl.BlockSpec((None, 1, tn), lambda l, j: (l, 0, j))],
        out_specs=pl.BlockSpec((None, 8, tn), lambda l, j: (l, 0, j)),
        out_shape=jax.ShapeDtypeStruct((depth, 8, n), F32),
        compiler_params=_cp(("parallel", "parallel")),
        name="adaln_vectors",
    )(c8, w_mod, b_mod.reshape(depth, 1, n))


def _repack_kernel(w_ref, main_ref, small_ref):
    seg = lambda a, n: w_ref[:, a:a + n].astype(BF16)
    main_ref[:, MAIN_GATE:MAIN_GATE + 4096] = seg(_GATE0, 4096)
    main_ref[:, MAIN_DN:MAIN_DN + 2048] = seg(_DN0, 2048)
    main_ref[:, MAIN_CONV:MAIN_CONV + 1024] = seg(_CONV0, 1024)
    main_ref[:, MAIN_FN:MAIN_FN + 512] = seg(_FN0, 512)
    main_ref[:, MAIN_GLA:MAIN_GLA + 1536] = seg(_GLA0, 1536)
    small = jnp.concatenate([w_ref[:, 2048:2064], w_ref[:, _GLA0 + 1536:_GLA0 + 1568],
                             jnp.zeros((w_ref.shape[0], LANE - 48), F32)], axis=1)
    small_ref[...] = small.astype(BF16)


def repack_projection_weights(w_in):
    depth, d, n = w_in.shape
    tr = 128
    return pl.pallas_call(
        _repack_kernel,
        grid=(depth, d // tr),
        in_specs=[pl.BlockSpec((None, tr, n), lambda l, i: (l, i, 0))],
        out_specs=[pl.BlockSpec((None, tr, MAIN_COLS), lambda l, i: (l, i, 0)),
                   pl.BlockSpec((None, tr, LANE), lambda l, i: (l, i, 0))],
        out_shape=[jax.ShapeDtypeStruct((depth, d, MAIN_COLS), BF16), jax.ShapeDtypeStruct((depth, d, LANE), BF16)],
        compiler_params=_cp(("parallel", "parallel")),
        name="repack_projection_weights",
    )(w_in)


def _lnmod_kernel(h_ref, mod_ref, o_ref):
    u = _ln(h_ref[...]) * (1.0 + mod_ref[1:2, :]) + mod_ref[0:1, :]
    o_ref[...] = u.astype(o_ref.dtype)


def ln_modulate(h, mod):
    b, l, d = h.shape
    tm = min(l, 512)
    return pl.pallas_call(
        _lnmod_kernel,
        grid=(b, l // tm),
        in_specs=[pl.BlockSpec((None, tm, d), lambda i, j: (i, j, 0)),
                  pl.BlockSpec((None, 8, d), lambda i, j: (i, 0, 0))],
        out_specs=pl.BlockSpec((None, tm, d), lambda i, j: (i, j, 0)),
        out_shape=jax.ShapeDtypeStruct((b, l, d), BF16),
        compiler_params=_cp(("parallel", "parallel")),
        name="ln_modulate",
    )(h, mod)


def _mm_kernel(x_ref, w_ref, b_ref, o_ref):
    o_ref[...] = (jnp.dot(x_ref[...], w_ref[...], preferred_element_type=F32) + b_ref[...]).astype(o_ref.dtype)


def matmul_bias(x, w, b, out_dtype, tn, n=None):
    m, k = x.shape
    n = n or w.shape[1]
    b = b[:n]
    tm = 1024 if m % 1024 == 0 else 512
    return pl.pallas_call(
        _mm_kernel,
        grid=(m // tm, n // tn),
        in_specs=[pl.BlockSpec((tm, k), lambda i, j: (i, 0)),
                  pl.BlockSpec((k, tn), lambda i, j: (0, j)),
                  pl.BlockSpec((1, tn), lambda i, j: (0, j))],
        out_specs=pl.BlockSpec((tm, tn), lambda i, j: (i, j)),
        out_shape=jax.ShapeDtypeStruct((m, n), out_dtype),
        compiler_params=_cp(("parallel", "parallel")),
        name="matmul_bias",
    )(x, w, b.reshape(1, n).astype(F32))


SUBLANES = 8


def _conv_shifted_copies(g_scr, width, tl):
    pad = (width - 1) // 2
    offs = [HALO + k - pad for k in range(width)]
    n = tl + (max(offs) // SUBLANES) * SUBLANES
    step = 64
    for b in sorted({o % SUBLANES for o in offs} - {0}):
        for s in range(0, n, step):
            m = min(step, n - s)
            g_scr[b, s:s + m, :] = g_scr[0, pl.ds(s + b, m), :]


def _conv_rows(g_scr, w_ref, width, r0, rs):
    pad = (width - 1) // 2
    acc = None
    for k in range(width):
        o = HALO + k - pad
        term = w_ref[k:k + 1, :] * g_scr[o % SUBLANES, pl.ds(r0 + (o // SUBLANES) * SUBLANES, rs), :]
        acc = term if acc is None else acc + term
    return acc


def _conformer_kernel(vc, vp, vn, gc, gp, gn, w_ref, cb_ref, lg_ref, lb_ref, o_ref, g_scr, *, tl, rs):
    i = pl.program_id(1)
    nt = pl.num_programs(1)
    glu = lambda v, g: v[...].astype(F32) * _sigmoid(g[...].astype(F32))
    g_scr[0, HALO:HALO + tl, :] = glu(vc, gc)
    g_scr[0, 0:HALO, :] = jnp.where(i > 0, glu(vp, gp), 0.0)
    g_scr[0, HALO + tl:2 * HALO + tl, :] = jnp.where(i < nt - 1, glu(vn, gn), 0.0)
    _conv_shifted_copies(g_scr, CONV_WIDTH, tl)
    for s in range(tl // rs):
        y = _conv_rows(g_scr, w_ref, CONV_WIDTH, s * rs, rs) + cb_ref[...]
        y = _silu(_ln(y) * lg_ref[...] + lb_ref[...])
        o_ref[s * rs:(s + 1) * rs, :] = y.astype(o_ref.dtype)


def conformer_conv(p_main, conv_w, conv_b, ln_g, ln_b):
    b, l, _ = p_main.shape
    c = CONV_DIM
    tl = min(l, 512)
    rs = 64
    hb = tl // HALO
    nhb = l // HALO
    vblk, gblk = MAIN_CONV // c, MAIN_CONV // c + 1
    cur = lambda cb: pl.BlockSpec((None, tl, c), lambda bi, i: (bi, i, cb))
    prv = lambda cb: pl.BlockSpec((None, HALO, c), lambda bi, i: (bi, jnp.maximum(i * hb - 1, 0), cb))
    nxt = lambda cb: pl.BlockSpec((None, HALO, c), lambda bi, i: (bi, jnp.minimum((i + 1) * hb, nhb - 1), cb))
    vec = pl.BlockSpec((1, c), lambda bi, i: (0, 0))
    return pl.pallas_call(
        functools.partial(_conformer_kernel, tl=tl, rs=rs),
        grid=(b, l // tl),
        in_specs=[cur(vblk), prv(vblk), nxt(vblk), cur(gblk), prv(gblk), nxt(gblk),
                  pl.BlockSpec((CONV_WIDTH, c), lambda bi, i: (0, 0)), vec, vec, vec],
        out_specs=pl.BlockSpec((None, tl, c), lambda bi, i: (bi, i, 0)),
        out_shape=jax.ShapeDtypeStruct((b, l, c), BF16),
        scratch_shapes=[pltpu.VMEM((SUBLANES, tl + 2 * HALO, c), F32)],
        compiler_params=_cp(("parallel", "parallel")),
        name="conformer_conv",
    )(p_main, p_main, p_main, p_main, p_main, p_main, conv_w,
      conv_b.reshape(1, c), ln_g.reshape(1, c), ln_b.reshape(1, c))


def _shortconv_kernel(xc, xp, xn, w_ref, o_ref, g_scr, *, tl, rs):
    i = pl.program_id(1)
    nt = pl.num_programs(1)
    g_scr[0, HALO:HALO + tl, :] = xc[...].astype(F32)
    g_scr[0, 0:HALO, :] = jnp.where(i > 0, xp[...].astype(F32), 0.0)
    g_scr[0, HALO + tl:2 * HALO + tl, :] = jnp.where(i < nt - 1, xn[...].astype(F32), 0.0)
    _conv_shifted_copies(g_scr, DN_CONV_WIDTH, tl)
    is_qk = pl.program_id(2) < 2
    for s in range(tl // rs):
        y = _silu(_conv_rows(g_scr, w_ref, DN_CONV_WIDTH, s * rs, rs))
        for h in range(DN_HEADS):
            yh = y[:, h * DN_HEAD_DIM:(h + 1) * DN_HEAD_DIM]
            yh = jnp.where(is_qk, _l2n(yh), yh)
            o_ref[s * rs:(s + 1) * rs, h * DN_HEAD_DIM:(h + 1) * DN_HEAD_DIM] = yh.astype(o_ref.dtype)


def deltanet_shortconv(p_main, dn_conv_w):
    b, l, _ = p_main.shape
    c = 512
    tl = min(l, 512)
    rs = 64
    hb = tl // HALO
    nhb = l // HALO
    cb0 = MAIN_DN // c
    return pl.pallas_call(
        functools.partial(_shortconv_kernel, tl=tl, rs=rs),
        grid=(b, l // tl, 3),
        in_specs=[pl.BlockSpec((None, tl, c), lambda bi, i, j: (bi, i, cb0 + j)),
                  pl.BlockSpec((None, HALO, c), lambda bi, i, j: (bi, jnp.maximum(i * hb - 1, 0), cb0 + j)),
                  pl.BlockSpec((None, HALO, c), lambda bi, i, j: (bi, jnp.minimum((i + 1) * hb, nhb - 1), cb0 + j)),
                  pl.BlockSpec((DN_CONV_WIDTH, c), lambda bi, i, j: (0, j))],
        out_specs=pl.BlockSpec((None, tl, c), lambda bi, i, j: (bi, i, j)),
        out_shape=jax.ShapeDtypeStruct((b, l, 3 * c), BF16),
        scratch_shapes=[pltpu.VMEM((SUBLANES, tl + 2 * HALO, c), F32)],
        compiler_params=_cp(("parallel", "parallel", "parallel")),
        name="deltanet_shortconv",
    )(p_main, p_main, p_main, dn_conv_w)


def _chunk_masks(rev):
    r = lax.broadcasted_iota(jnp.int32, (CHUNK, CHUNK), 0)
    c = lax.broadcasted_iota(jnp.int32, (CHUNK, CHUNK), 1)
    if rev:
        cum, sx, incl, strict = c >= r, r < c, c >= r, c > r
    else:
        cum, sx, incl, strict = c <= r, r > c, c <= r, c < r
    return cum.astype(F32), sx.astype(F32), incl, strict, (r == c).astype(F32)


def _dn_chunks(chains):
    each = lambda f: [f(c) for c in chains]
    scale = DN_HEAD_DIM ** -0.5
    n = CHUNK
    rows = lambda top, bot: jnp.concatenate([top, bot], axis=0)
    decay = each(lambda c: jnp.where(c["incl"], jnp.exp(c["gs"] - c["gs_row"]), 0.0))
    kb = each(lambda c: c["k"] * c["beta"])
    qs = each(lambda c: c["q"] * scale)
    kq = [_dot_nt(rows(kbi, qi), c["k"]) for c, kbi, qi in zip(chains, kb, qs)]
    m = [jnp.where(c["strict"], r[:n] * di, 0.0) for c, r, di in zip(chains, kq, decay)]
    a = [r[n:] * di for r, di in zip(kq, decay)]
    inv = [c["eye"] - mi for c, mi in zip(chains, m)]
    p = [_dot(mi, mi) for mi in m]
    for _ in range(int(math.log2(CHUNK)) - 2):
        r = [_dot(rows(pi, ii), pi) for pi, ii in zip(p, inv)]
        inv = [ii + ri[n:] for ii, ri in zip(inv, r)]
        p = [ri[:n] for ri in r]
    inv = [ii + _dot(ii, pi) for ii, pi in zip(inv, p)]
    uw = [_dot(ii, jnp.concatenate([c["v"] * c["beta"], kbi * c["eg"]], axis=1)) for c, ii, kbi in zip(chains, inv, kb)]
    hd = DN_HEAD_DIM
    ws = [_dot(rows(r[:, hd:], qi * c["eg"]), c["s"]) for c, r, qi in zip(chains, uw, qs)]
    v_new = [r[:, :hd] - wsi[:n] for r, wsi in zip(uw, ws)]
    o = [wsi[n:] + _dot(ai, vi) for wsi, ai, vi in zip(ws, a, v_new)]
    s_new = [c["s"] * c["etot"] + _dot_tn(c["k"] * c["egt"], vi) for c, vi in zip(chains, v_new)]
    return list(zip(o, s_new))


def _l2n(t):
    return t * lax.rsqrt(jnp.sum(t * t, axis=-1, keepdims=True) + EPS)


def _dn_scan_kernel(xf_ref, bgf_ref, xb_ref, bgb_ref, par_ref, s0_ref, of_ref, ob_ref, sfin_ref, *s_scr, nb):
    i = pl.program_id(0)
    n = pl.num_programs(0)
    hd = DN_HEAD_DIM
    chain = lambda b, d, h: s_scr[(b * 2 + d) * DN_HEADS + h]

    @pl.when(i == 0)
    def _():
        for b in range(nb):
            for d in range(2):
                for h in range(DN_HEADS):
                    chain(b, d, h)[...] = s0_ref[b, d, h]

    chains, sinks = [], []
    for d, (x_ref, bg_ref, o_ref) in enumerate(((xf_ref, bgf_ref, of_ref), (xb_ref, bgb_ref, ob_ref))):
        cum, _, incl, strict, eye = _chunk_masks(rev=bool(d))
        for b in range(nb):
            bg = bg_ref[b]
            beta_all = _sigmoid(bg)
            g_all = -jnp.exp(par_ref[0:1, :]) * _softplus(bg + par_ref[1:2, :])
            gs_all = _dot_hi(cum, g_all)
            tot = gs_all[0:1, :] if d else gs_all[CHUNK - 1:CHUNK, :]
            eg_all, egt_all, etot_all = jnp.exp(gs_all), jnp.exp(tot - gs_all), jnp.exp(tot)
            gs_t = jnp.concatenate([gs_all, jnp.zeros_like(gs_all)], axis=0).T
            for h in range(DN_HEADS):
                cb = d * DN_HEADS + h
                col = 2 * DN_HEADS + cb
                chains.append(dict(
                    q=x_ref[b, :, h * hd:(h + 1) * hd].astype(F32),
                    k=x_ref[b, :, DN_DIM + h * hd:DN_DIM + (h + 1) * hd].astype(F32),
                    v=x_ref[b, :, 2 * DN_DIM + h * hd:2 * DN_DIM + (h + 1) * hd].astype(F32),
                    beta=beta_all[:, cb:cb + 1], gs=gs_all[:, col:col + 1], eg=eg_all[:, col:col + 1],
                    egt=egt_all[:, col:col + 1], etot=etot_all[:, col:col + 1], gs_row=gs_t[col:col + 1, :CHUNK],
                    s=chain(b, d, h)[...], incl=incl, strict=strict, eye=eye))
                sinks.append((o_ref, b, h, chain(b, d, h)))
    for (o_ref, b, h, s_ref), (o, s_new) in zip(sinks, _dn_chunks(chains)):
        o_ref[b, :, h * hd:(h + 1) * hd] = o.astype(o_ref.dtype)
        s_ref[...] = s_new

    @pl.when(i == n - 1)
    def _():
        for b in range(nb):
            for d in range(2):
                for h in range(DN_HEADS):
                    sfin_ref[b, d, h] = chain(b, d, h)[...]


def deltanet_scan(qkv, bg, par, s0):
    b, l, _ = qkv.shape
    n = l // CHUNK
    fwd = lambda w: pl.BlockSpec((b, CHUNK, w), lambda i: (0, i, 0))
    bwd = lambda w: pl.BlockSpec((b, CHUNK, w), lambda i: (0, n - 1 - i, 0))
    st = pl.BlockSpec(s0.shape, lambda i: (0, 0, 0, 0, 0))
    return pl.pallas_call(
        functools.partial(_dn_scan_kernel, nb=b),
        grid=(n,),
        in_specs=[fwd(3 * DN_DIM), fwd(128), bwd(3 * DN_DIM), bwd(128),
                  pl.BlockSpec((8, 128), lambda i: (0, 0)), st],
        out_specs=[fwd(DN_DIM), bwd(DN_DIM), st],
        out_shape=[jax.ShapeDtypeStruct((b, l, DN_DIM), BF16), jax.ShapeDtypeStruct((b, l, DN_DIM), BF16),
                   jax.ShapeDtypeStruct(s0.shape, F32)],
        scratch_shapes=[pltpu.VMEM((DN_HEAD_DIM, DN_HEAD_DIM), F32)] * (b * 2 * DN_HEADS),
        compiler_params=_cp(("arbitrary",)),
        name="deltanet_scan",
    )(qkv, bg, qkv, bg, par, s0)


GLA_SUB = 16


def _gla_chunks(chains):
    row = lax.broadcasted_iota(jnp.int32, (CHUNK, 1), 0)
    lane = lax.broadcasted_iota(jnp.int32, (1, 2 * GLA_DK), 1)
    o_inter = [_dot_nt(c["qp"] * jnp.exp(c["bp"]), c["st"]) for c in chains]
    blocks = [([], []) for _ in chains]
    for blk in range(CHUNK // GLA_SUB):
        i0 = blk * GLA_SUB
        mid = i0 + GLA_SUB // 2
        ri = lax.broadcasted_iota(jnp.int32, (GLA_SUB, CHUNK), 0) + i0
        ci = lax.broadcasted_iota(jnp.int32, (GLA_SUB, CHUNK), 1)
        for c, blks in zip(chains, blocks):
            bp = c["bp"]
            ref = bp[mid:mid + 1, :]
            qt = c["qp"][i0:i0 + GLA_SUB, :] * jnp.exp(bp[i0:i0 + GLA_SUB, :] - ref)
            valid = (row >= i0) if c["rev"] else (row < i0 + GLA_SUB)
            kt = c["kp"] * jnp.exp(jnp.where(valid, ref - bp, 0.0))
            causal = (ci >= ri) if c["rev"] else (ci <= ri)
            qh = [jnp.where((lane >= hh * GLA_DK) & (lane < (hh + 1) * GLA_DK), qt, 0.0) for hh in range(2)]
            both = _dot_nt(jnp.concatenate(qh, axis=0), kt)
            for hh in range(2):
                blks[hh].append(jnp.where(causal, both[hh * GLA_SUB:(hh + 1) * GLA_SUB], 0.0))
    o_intra = [[_dot(jnp.concatenate(blks[hh], axis=0), c["vp"][:, hh * GLA_DV:(hh + 1) * GLA_DV]) for hh in range(2)]
               for c, blks in zip(chains, blocks)]
    srow = lax.broadcasted_iota(jnp.int32, (2 * GLA_DV, 2 * GLA_DK), 0)
    scol = lax.broadcasted_iota(jnp.int32, (2 * GLA_DV, 2 * GLA_DK), 1)
    bd = (srow < GLA_DV) == (scol < GLA_DK)
    out = []
    for c, oi, ox in zip(chains, o_inter, o_intra):
        bp = c["bp"]
        b_last = bp[0:1, :] if c["rev"] else bp[CHUNK - 1:CHUNK, :]
        st_new = jnp.where(bd, c["st"] * jnp.exp(b_last) + _dot_tn(c["vp"], c["kp"] * jnp.exp(b_last - bp)), 0.0)
        out.append((oi + jnp.concatenate(ox, axis=1), st_new))
    return out


def _gla_scan_kernel(qkf, vf, lrf, qkb, vb, lrb, w2_ref, b2_ref, s0_ref, of_ref, ob_ref, sfin_ref, *s_scr, nb):
    i = pl.program_id(0)
    n = pl.num_programs(0)
    npair = GLA_HEADS // 2
    chain = lambda b, d, p: s_scr[(b * 2 + d) * npair + p]

    @pl.when(i == 0)
    def _():
        for b in range(nb):
            for d in range(2):
                for p in range(npair):
                    chain(b, d, p)[...] = s0_ref[b, d, p]

    chains, sinks = [], []
    zs = [[_dot_hi(lr_ref[b], w2_ref[d]) + b2_ref[d] for b in range(nb)]
          for d, lr_ref in enumerate((lrf, lrb))]
    for d, (qk_ref, v_ref, o_ref) in enumerate(((qkf, vf, of_ref), (qkb, vb, ob_ref))):
        cum = _chunk_masks(rev=bool(d))[0]
        for b in range(nb):
            bs = _dot_hi(cum, -_softplus(-zs[d][b]) * (1.0 / GLA_TAU))
            for p in range(npair):
                lo = p * 2 * GLA_DK
                chains.append(dict(
                    qp=qk_ref[b, :, lo:lo + 2 * GLA_DK].astype(F32) * (GLA_DK ** -0.5),
                    kp=qk_ref[b, :, GLA_KDIM + lo:GLA_KDIM + lo + 2 * GLA_DK].astype(F32),
                    vp=v_ref[b, :, p * 2 * GLA_DV:(p + 1) * 2 * GLA_DV].astype(F32),
                    bp=bs[:, lo:lo + 2 * GLA_DK], st=chain(b, d, p)[...], rev=bool(d)))
                sinks.append((o_ref, b, p, chain(b, d, p)))
    for (o_ref, b, p, s_ref), (o, st_new) in zip(sinks, _gla_chunks(chains)):
        o_ref[b, :, p * 2 * GLA_DV:(p + 1) * 2 * GLA_DV] = o.astype(o_ref.dtype)
        s_ref[...] = st_new

    @pl.when(i == n - 1)
    def _():
        for b in range(nb):
            for d in range(2):
                for p in range(npair):
                    sfin_ref[b, d, p] = chain(b, d, p)[...]


def gla_scan(p, col0, small, w2p, b2, s0):
    b, l, _ = p.shape
    n = l // CHUNK
    fwd = lambda w, cb: pl.BlockSpec((b, CHUNK, w), lambda i: (0, i, cb))
    bwd = lambda w, cb: pl.BlockSpec((b, CHUNK, w), lambda i: (0, n - 1 - i, cb))
    st = pl.BlockSpec(s0.shape, lambda i: (0, 0, 0, 0, 0))
    return pl.pallas_call(
        functools.partial(_gla_scan_kernel, nb=b),
        grid=(n,),
        in_specs=[fwd(512, col0), fwd(512, col0 + 1), fwd(128, 0), bwd(512, col0), bwd(512, col0 + 1), bwd(128, 0),
                  pl.BlockSpec((2, 128, GLA_KDIM), lambda i: (0, 0, 0)),
                  pl.BlockSpec((2, 1, GLA_KDIM), lambda i: (0, 0, 0)), st],
        out_specs=[fwd(GLA_VDIM, 0), bwd(GLA_VDIM, 0), st],
        out_shape=[jax.ShapeDtypeStruct((b, l, GLA_VDIM), BF16)] * 2 + [jax.ShapeDtypeStruct(s0.shape, F32)],
        scratch_shapes=[pltpu.VMEM((2 * GLA_DV, 2 * GLA_DK), F32)] * (b * 2 * (GLA_HEADS // 2)),
        compiler_params=_cp(("arbitrary",)),
        name="gla_scan",
    )(p, p, small, p, p, small, w2p, b2, s0)


CM_ROWS = 16
LANE = 128


def _proj_colmajor_kernel(x_ref, w_ref, b_ref, ws_ref, bs_ref, o_ref, os_ref, scr, *, ncol):
    res = jnp.dot(x_ref[...], w_ref[...], preferred_element_type=F32) + b_ref[...]
    for k in range(ncol):
        scr[k] = res[:, k * LANE:(k + 1) * LANE]
    scr[ncol] = jnp.dot(x_ref[...], ws_ref[...], preferred_element_type=F32) + bs_ref[...]
    for c in range(GRID_W):
        for k in range(ncol):
            o_ref[c, :, k * LANE:(k + 1) * LANE] = scr[k, pl.ds(c, CM_ROWS, stride=GRID_W), :].astype(o_ref.dtype)
        os_ref[c] = scr[ncol, pl.ds(c, CM_ROWS, stride=GRID_W), :]


def project_colmajor(u, w, bias, col0, n, w_small, b_small):
    b, l, d = u.shape
    cblk = col0 // n
    bias = bias[col0:col0 + n]
    rows = l // GRID_W
    tm = CM_ROWS * GRID_W
    nt = rows // CM_ROWS
    o, os_ = pl.pallas_call(
        functools.partial(_proj_colmajor_kernel, ncol=n // LANE),
        grid=(b, nt),
        in_specs=[pl.BlockSpec((None, tm, d), lambda bi, i: (bi, i, 0)),
                  pl.BlockSpec((d, n), lambda bi, i: (0, cblk)), pl.BlockSpec((1, n), lambda bi, i: (0, 0)),
                  pl.BlockSpec((d, LANE), lambda bi, i: (0, 0)), pl.BlockSpec((1, LANE), lambda bi, i: (0, 0))],
        out_specs=[pl.BlockSpec((None, GRID_W, CM_ROWS, n), lambda bi, i: (bi, 0, i, 0)),
                   pl.BlockSpec((None, GRID_W, CM_ROWS, LANE), lambda bi, i: (bi, 0, i, 0))],
        out_shape=[jax.ShapeDtypeStruct((b, GRID_W, rows, n), BF16), jax.ShapeDtypeStruct((b, GRID_W, rows, LANE), F32)],
        scratch_shapes=[pltpu.VMEM((n // LANE + 1, tm, LANE), F32)],
        compiler_params=_cp(("parallel", "parallel")),
        name="project_colmajor",
    )(u, w, bias.reshape(1, n).astype(F32), w_small, b_small.reshape(1, LANE).astype(F32))
    return o.reshape(b, l, n), os_.reshape(b, l, LANE)


def _headnorm_colmajor_kernel(of_ref, ob_ref, z_ref, g_ref, o_ref, scr):
    o = of_ref[...].astype(F32) + ob_ref[...].astype(F32)
    z = z_ref[...].astype(F32)
    for h in range(4):
        sl = slice(h * LANE, (h + 1) * LANE)
        oh = o[:, :, sl]
        y = oh * lax.rsqrt(jnp.mean(oh * oh, axis=-1, keepdims=True) + EPS) * g_ref[...] * _silu(z[:, :, sl])
        for c in range(GRID_W):
            scr[h, pl.ds(c, CM_ROWS, stride=GRID_W), :] = y[c]
    for h in range(4):
        o_ref[:, h * LANE:(h + 1) * LANE] = scr[h].astype(o_ref.dtype)


def gated_head_norm_colmajor(o_f, o_b, p_gla, z_blk, g):
    b, l, c = o_f.shape
    rows = l // GRID_W
    tm = CM_ROWS * GRID_W
    cm = lambda t: t.reshape(b, GRID_W, rows, t.shape[-1])
    blk = lambda cb: pl.BlockSpec((None, GRID_W, CM_ROWS, c), lambda bi, i: (bi, 0, i, cb))
    return pl.pallas_call(
        _headnorm_colmajor_kernel,
        grid=(b, rows // CM_ROWS),
        in_specs=[blk(0), blk(0), blk(z_blk), pl.BlockSpec((1, LANE), lambda bi, i: (0, 0))],
        out_specs=pl.BlockSpec((None, tm, c), lambda bi, i: (bi, i, 0)),
        out_shape=jax.ShapeDtypeStruct((b, l, c), BF16),
        scratch_shapes=[pltpu.VMEM((4, tm, LANE), F32)],
        compiler_params=_cp(("parallel", "parallel")),
        name="gated_head_norm_colmajor",
    )(cm(o_f), cm(o_b), cm(p_gla), g.reshape(1, LANE).astype(F32))


def _headnorm_kernel(of_ref, ob_ref, z_ref, g_ref, o_ref):
    o = of_ref[...].astype(F32) + ob_ref[...].astype(F32)
    z = z_ref[...].astype(F32)
    for h in range(4):
        sl = slice(h * 128, (h + 1) * 128)
        oh = o[:, sl]
        oh = oh * lax.rsqrt(jnp.mean(oh * oh, axis=-1, keepdims=True) + EPS) * g_ref[...]
        o_ref[:, sl] = (oh * _silu(z[:, sl])).astype(o_ref.dtype)


def gated_head_norm(o_f, o_b, z_arr, z_blk, g):
    b, l, c = o_f.shape
    tl = min(l, 512)
    blk = pl.BlockSpec((None, tl, c), lambda bi, i: (bi, i, 0))
    return pl.pallas_call(
        _headnorm_kernel,
        grid=(b, l // tl),
        in_specs=[blk, blk, pl.BlockSpec((None, tl, c), lambda bi, i: (bi, i, z_blk)),
                  pl.BlockSpec((1, 128), lambda bi, i: (0, 0))],
        out_specs=blk,
        out_shape=jax.ShapeDtypeStruct((b, l, c), BF16),
        compiler_params=_cp(("parallel", "parallel")),
        name="gated_head_norm",
    )(o_f, o_b, z_arr, g.reshape(1, 128).astype(F32))


def _dft_tables(n):
    ang = 2.0 * np.pi * (np.outer(np.arange(n), np.arange(n)) % n) / n
    return np.cos(ang), np.sin(ang)


def _fnet_small_kernel(x_ref, wc_ref, cl_ref, sl_ref, o_ref):
    y = jnp.dot(x_ref[...], wc_ref[...], preferred_element_type=F32)
    out = _dot(cl_ref[...], y[:, :FN_GROUP_DIM]) + _dot(sl_ref[...], y[:, FN_GROUP_DIM:])
    o_ref[...] = out.astype(o_ref.dtype)


def _fnet_big_kernel(x_ref, wc_ref, f1_ref, twc_ref, tws_ref, f2_ref, o_ref, y_scr, yi_scr, b_scr, *, n1):
    n2 = FN_GROUP_DIM
    l = n1 * n2
    rb = 512

    def step0(r, c):
        r0 = pl.multiple_of(r * rb, rb)
        y = jnp.dot(x_ref[pl.ds(r0, rb), :], wc_ref[...], preferred_element_type=F32)
        y_scr[pl.ds(r0, rb), :] = y[:, :n2]
        yi_scr[pl.ds(r0, rb), :] = y[:, n2:]
        return c

    lax.fori_loop(0, l // rb, step0, 0)

    def step1(j, c):
        mr = _dot(f1_ref[...], y_scr[pl.ds(j, n1, stride=n2), :])
        mi = _dot(f1_ref[...], yi_scr[pl.ds(j, n1, stride=n2), :])
        ar = mr[:n1] + mi[n1:]
        ai = mi[:n1] - mr[n1:]
        tc, ts = twc_ref[j], tws_ref[j]
        b_scr[pl.ds(j, n1, stride=2 * n2), :] = ar * tc + ai * ts
        b_scr[pl.ds(n2 + j, n1, stride=2 * n2), :] = ai * tc - ar * ts
        return c

    lax.fori_loop(0, n2, step1, 0, unroll=8)

    def step2(k1, c):
        bk = b_scr[pl.ds(pl.multiple_of(k1 * 2 * n2, 2 * n2), 2 * n2), :]
        y_scr[pl.ds(k1, n2, stride=n1), :] = _dot(f2_ref[...], bk)
        return c

    lax.fori_loop(0, n1, step2, 0, unroll=8)

    def step3(r, c):
        r0 = pl.multiple_of(r * rb, rb)
        o_ref[pl.ds(r0, rb), :] = y_scr[pl.ds(r0, rb), :].astype(o_ref.dtype)
        return c

    lax.fori_loop(0, l // rb, step3, 0)


def fourier_mix(p_main):
    b, l, _ = p_main.shape
    gd = FN_GROUP_DIM
    cc, sc = _dft_tables(gd)
    wc = jnp.asarray(np.concatenate([cc, -sc], axis=1) / math.sqrt(gd), BF16)
    x_spec = pl.BlockSpec((None, l, gd), lambda bi, g: (bi, 0, MAIN_FN // gd + g))
    o_spec = pl.BlockSpec((None, l, gd), lambda bi, g: (bi, 0, g))
    full = lambda shape: pl.BlockSpec(shape, lambda bi, g: (0,) * len(shape))
    out_shape = jax.ShapeDtypeStruct((b, l, FN_DIM), BF16)
    if l <= 512:
        cl, sl = _dft_tables(l)
        scale = 1.0 / math.sqrt(l)
        return pl.pallas_call(
            _fnet_small_kernel, grid=(b, FN_GROUPS),
            in_specs=[x_spec, full((gd, 2 * gd)), full((l, l)), full((l, l))],
            out_specs=o_spec, out_shape=out_shape,
            compiler_params=_cp(("parallel", "parallel")), name="fourier_mix_small",
        )(p_main, wc, jnp.asarray(cl * scale, BF16), jnp.asarray(sl * scale, BF16))
    n1, n2 = l // gd, gd
    c1, s1 = _dft_tables(n1)
    c2, s2 = _dft_tables(n2)
    f1 = jnp.asarray(np.concatenate([c1, s1], axis=0), BF16)
    f2 = jnp.asarray(np.concatenate([c2, s2], axis=1) / math.sqrt(l), BF16)
    ang = 2.0 * np.pi * np.outer(np.arange(n2), np.arange(n1)) / l
    twc = jnp.asarray(np.broadcast_to(np.cos(ang)[:, :, None], (n2, n1, gd)), F32)
    tws = jnp.asarray(np.broadcast_to(np.sin(ang)[:, :, None], (n2, n1, gd)), F32)
    return pl.pallas_call(
        functools.partial(_fnet_big_kernel, n1=n1), grid=(b, FN_GROUPS),
        in_specs=[x_spec, full((gd, 2 * gd)), full((2 * n1, n1)), full((n2, n1, gd)), full((n2, n1, gd)),
                  full((n2, 2 * n2))],
        out_specs=o_spec, out_shape=out_shape,
        scratch_shapes=[pltpu.VMEM((l, gd), F32), pltpu.VMEM((l, gd), F32), pltpu.VMEM((n1 * 2 * n2, gd), F32)],
        compiler_params=_cp(("parallel", "parallel")), name="fourier_mix_big",
    )(p_main, wc, f1, twc, tws, f2)


def _merge_kernel(fa, fb, fc, fd, gt_ref, wb_ref, wo_ref, h_ref, mod_ref, lng_ref, lnb_ref, wr_ref,
                  h1_ref, v_ref, lg_ref):
    half = h_ref.shape[0] // 2
    for r0 in (0, half):
        rs = slice(r0, r0 + half)
        acc = None
        for n, f_ref in enumerate((fa, fb, fc, fd)):
            proj = jnp.dot(f_ref[rs, :], wb_ref[n], preferred_element_type=F32)
            term = _sigmoid(gt_ref[rs, n * D_MODEL:(n + 1) * D_MODEL].astype(F32)) * proj
            acc = term if acc is None else acc + term
        y = _dot(acc, wo_ref[...])
        h1 = _ln(DEEPNORM_ALPHA * h_ref[rs, :] + mod_ref[2:3, :] * y) * lng_ref[...] + lnb_ref[...]
        h1_ref[rs, :] = h1
        v = _ln(h1) * (1.0 + mod_ref[4:5, :]) + mod_ref[3:4, :]
        v_ref[rs, :] = _pack_bf16_pair(v)
        lg_ref[:, rs] = lax.dot_general(wr_ref[...], v, (((1,), (1,)), ((), ())), precision=HI,
                                        preferred_element_type=F32)


def merge_branches(feats, p_main, w_branch, w_o, h, mod, ln_g, ln_b, w_router_t):
    b, l, d = h.shape
    tm = 256
    nt = l // tm
    fblk = pl.BlockSpec((None, tm, BRANCH_DIM), lambda bi, i: (bi, i, 0))
    hblk = pl.BlockSpec((None, tm, d), lambda bi, i: (bi, i, 0))
    vec = pl.BlockSpec((1, d), lambda bi, i: (0, 0))
    return pl.pallas_call(
        _merge_kernel,
        grid=(b, l // tm),
        in_specs=[fblk, fblk, fblk, fblk,
                  pl.BlockSpec((None, tm, N_BRANCH * d), lambda bi, i: (bi, i, MAIN_GATE // (N_BRANCH * d))),
                  pl.BlockSpec((N_BRANCH, BRANCH_DIM, d), lambda bi, i: (0, 0, 0)),
                  pl.BlockSpec((d, d), lambda bi, i: (0, 0)),
                  hblk, pl.BlockSpec((None, 8, d), lambda bi, i: (bi, 0, 0)), vec, vec,
                  pl.BlockSpec((N_EXPERTS, d), lambda bi, i: (0, 0))],
        out_specs=[hblk, pl.BlockSpec((None, tm, d // 2), lambda bi, i: (bi, i, 0)),
                   pl.BlockSpec((N_EXPERTS, tm), lambda bi, i: (0, bi * nt + i))],
        out_shape=[jax.ShapeDtypeStruct((b, l, d), F32), jax.ShapeDtypeStruct((b, l, d // 2), jnp.int32),
                   jax.ShapeDtypeStruct((N_EXPERTS, b * l), F32)],
        compiler_params=_cp(("parallel", "parallel")),
        name="merge_branches",
    )(*feats, p_main, w_branch, w_o, h, mod, ln_g.reshape(1, d), ln_b.reshape(1, d), w_router_t)


def _first_argmax(x, axis, size):
    m = jnp.max(x, axis=axis, keepdims=True)
    idx = lax.broadcasted_iota(jnp.int32, x.shape, axis)
    first = jnp.min(jnp.where(x == m, idx, size), axis=axis, keepdims=True)
    return m, idx == first


def _route_kernel(lg_ref, rb_ref, pos_ref, w_ref, cnt_ref, cnt_scr, *, capacity):
    i = pl.program_id(0)

    @pl.when(i == 0)
    def _():
        cnt_scr[...] = jnp.zeros_like(cnt_scr)

    tm = lg_ref.shape[-1]
    per = N_EXPERTS // N_EXPERT_GROUPS
    scores = _sigmoid(lg_ref[...])
    biased = scores + rb_ref[...]
    x3 = biased.reshape(N_EXPERT_GROUPS, per, tm)
    m1, hit = _first_argmax(x3, 1, per)
    m2 = jnp.max(jnp.where(hit, -jnp.inf, x3), axis=1, keepdims=True)
    gscore = (m1 + m2).reshape(N_EXPERT_GROUPS, tm)
    gsel = jnp.zeros(gscore.shape, F32)
    for _ in range(TOP_GROUPS):
        _, hit = _first_argmax(gscore, 0, N_EXPERT_GROUPS)
        gsel = jnp.where(hit, 1.0, gsel)
        gscore = jnp.where(hit, -jnp.inf, gscore)
    masked = jnp.where(gsel.reshape(N_EXPERT_GROUPS, 1, tm) > 0.0, x3, -jnp.inf).reshape(N_EXPERTS, tm)
    sel = jnp.zeros(masked.shape, F32)
    for _ in range(TOP_K):
        _, hit = _first_argmax(masked, 0, N_EXPERTS)
        sel = jnp.where(hit, 1.0, sel)
        masked = jnp.where(hit, -jnp.inf, masked)
    w = sel * scores
    comb = w / jnp.sum(w, axis=0, keepdims=True) * ROUTED_SCALE
    tr = lax.broadcasted_iota(jnp.int32, (tm, tm), 0)
    tc = lax.broadcasted_iota(jnp.int32, (tm, tm), 1)
    rank = cnt_scr[:, 0:1] + _dot(sel, (tr < tc).astype(F32))
    slot = lax.broadcasted_iota(jnp.int32, sel.shape, 0).astype(F32) * float(capacity) + rank
    pos_rows, w_rows = [], []
    remaining = sel
    for _ in range(TOP_K):
        _, hit = _first_argmax(remaining, 0, N_EXPERTS)
        pos_rows.append(jnp.sum(jnp.where(hit, slot, 0.0), axis=0, keepdims=True))
        w_rows.append(jnp.sum(jnp.where(hit, comb, 0.0), axis=0, keepdims=True))
        remaining = jnp.where(hit, 0.0, remaining)
    zero = jnp.zeros((8 - TOP_K, tm), F32)
    pos_ref[...] = jnp.concatenate(pos_rows + [zero], axis=0).astype(jnp.int32)
    w_ref[...] = jnp.concatenate(w_rows + [zero], axis=0)
    cnt_scr[...] = cnt_scr[...] + jnp.sum(sel, axis=1, keepdims=True)

    @pl.when(i == pl.num_programs(0) - 1)
    def _():
        cnt_ref[...] = cnt_scr[...]


def moe_route(logits_t, b_router):
    e, t = logits_t.shape
    tm = 256
    blk = pl.BlockSpec((e, tm), lambda i: (0, i))
    oblk = pl.BlockSpec((8, tm), lambda i: (0, i))
    return pl.pallas_call(
        functools.partial(_route_kernel, capacity=t), grid=(t // tm,),
        in_specs=[blk, pl.BlockSpec((e, tm), lambda i: (0, 0))],
        out_specs=[oblk, oblk, pl.BlockSpec((e, 128), lambda i: (0, 0))],
        out_shape=[jax.ShapeDtypeStruct((8, t), jnp.int32), jax.ShapeDtypeStruct((8, t), F32),
                   jax.ShapeDtypeStruct((e, 128), F32)],
        scratch_shapes=[pltpu.VMEM((e, 128), F32)],
        compiler_params=_cp(("arbitrary",)), name="moe_route",
    )(logits_t, jnp.broadcast_to(b_router.astype(F32)[:, None], (e, tm)))


def _sc_workers():
    info = plsc.get_sparse_core_info()
    return info.num_cores, info.num_subcores


def _sc_chunk(per_worker):
    return max(c for c in range(8, 129, 8) if per_worker % c == 0)


def sc_scatter_rows(rows, pos, n_out):
    t, w = rows.shape
    nc, ns = _sc_workers()
    nw = nc * ns
    per_w = t // nw
    ch = _sc_chunk(per_w)
    nch = per_w // ch
    pos_w = pos[:TOP_K].reshape(TOP_K, nw, nch, ch).transpose(1, 2, 0, 3)
    mesh = plsc.VectorSubcoreMesh(core_axis_name="c", subcore_axis_name="s")

    @functools.partial(
        pl.kernel, mesh=mesh, out_type=jax.ShapeDtypeStruct((n_out, w), jnp.int32),
        scratch_types=[pltpu.VMEM((TOP_K, ch), jnp.int32), pltpu.VMEM((ch, w), jnp.int32), pltpu.SemaphoreType.DMA])
    def scatter(rows_hbm, pos_hbm, out_hbm, idx_v, rows_v, sem):
        wid = lax.axis_index("s") * nc + lax.axis_index("c")

        @pl.loop(0, nch)
        def _(j):
            pltpu.sync_copy(pos_hbm.at[wid, j], idx_v)
            pltpu.sync_copy(rows_hbm.at[pl.ds(wid * per_w + j * ch, ch)], rows_v)
            copies = [pltpu.async_copy(rows_v, out_hbm.at[idx_v.at[k]], sem) for k in range(TOP_K)]
            for cp in copies:
                cp.wait()

    return scatter(rows, pos_w)


def sc_gather_rows(table, idx):
    m = idx.shape[0]
    w = table.shape[1]
    nc, ns = _sc_workers()
    nw = nc * ns
    per_w = m // nw
    ch = _sc_chunk(per_w)
    nch = per_w // ch
    mesh = plsc.VectorSubcoreMesh(core_axis_name="c", subcore_axis_name="s")

    @functools.partial(
        pl.kernel, mesh=mesh, out_type=jax.ShapeDtypeStruct((m, w), jnp.int32),
        scratch_types=[pltpu.VMEM((ch,), jnp.int32), pltpu.VMEM((ch, w), jnp.int32), pltpu.SemaphoreType.DMA])
    def gather(table_hbm, idx_hbm, out_hbm, idx_v, rows_v, sem):
        wid = lax.axis_index("s") * nc + lax.axis_index("c")

        @pl.loop(0, nch)
        def _(j):
            off = wid * per_w + j * ch
            pltpu.sync_copy(idx_hbm.at[pl.ds(off, ch)], idx_v)
            pltpu.async_copy(table_hbm.at[idx_v], rows_v, sem).wait()
            pltpu.sync_copy(rows_v, out_hbm.at[pl.ds(off, ch)])

    return gather(table, idx)


MOE_TM = 512


def _gmm_kernel(te_ref, tj_ref, na_ref, x_ref, wg_ref, wu_ref, wd_ref, y_ref, wg_s, wu_s, wd_s):
    i = pl.program_id(0)

    @pl.when(i < na_ref[0])
    def _():
        @pl.when(tj_ref[i] == 0)
        def _():
            wg_s[...] = wg_ref[...].astype(BF16)
            wu_s[...] = wu_ref[...].astype(BF16)
            wd_s[...] = wd_ref[...].astype(BF16)

        half = D_MODEL // 2
        lo, hi = _unpack_bf16_pair(x_ref[...])
        gate = _dot(lo, wg_s[:half, :]) + _dot(hi, wg_s[half:, :])
        up = _dot(lo, wu_s[:half, :]) + _dot(hi, wu_s[half:, :])
        y_ref[...] = _pack_bf16_pair(_dot(_silu(gate) * up, wd_s[...]))


def moe_grouped_experts(xs, counts, w_gate, w_up, w_down, capacity):
    n_e, d, hdim = w_gate.shape
    blocks_per_e = capacity // MOE_TM
    n_tiles = capacity * TOP_K // MOE_TM + n_e
    tiles_e = (counts.astype(jnp.int32) + MOE_TM - 1) // MOE_TM
    ends = jnp.cumsum(tiles_e)
    n_active = ends[-1]
    step = jnp.minimum(jnp.arange(n_tiles, dtype=jnp.int32), n_active - 1)
    owned = step[:, None] >= ends[None, :]
    te = jnp.sum(owned, axis=1).astype(jnp.int32)
    tj = step - jnp.sum(jnp.where(owned, tiles_e[None, :], 0), axis=1).astype(jnp.int32)
    row_blk = lambda i, te_r, tj_r, na_r: (te_r[i] * blocks_per_e + tj_r[i], 0)
    wmap = lambda i, te_r, tj_r, na_r: (te_r[i], 0, 0)
    return pl.pallas_call(
        _gmm_kernel,
        grid_spec=pltpu.PrefetchScalarGridSpec(
            num_scalar_prefetch=3, grid=(n_tiles,),
            in_specs=[pl.BlockSpec((MOE_TM, d // 2), row_blk),
                      pl.BlockSpec((None, d, hdim), wmap), pl.BlockSpec((None, d, hdim), wmap),
                      pl.BlockSpec((None, hdim, d), wmap)],
            out_specs=pl.BlockSpec((MOE_TM, d // 2), row_blk),
            scratch_shapes=[pltpu.VMEM((d, hdim), BF16), pltpu.VMEM((d, hdim), BF16), pltpu.VMEM((hdim, d), BF16)]),
        out_shape=jax.ShapeDtypeStruct(xs.shape, jnp.int32),
        compiler_params=_cp(("arbitrary",)),
        name="moe_grouped_experts",
    )(te, tj, n_active.reshape(1).astype(jnp.int32), xs, w_gate, w_up, w_down)


def _moe_out_kernel(v_ref, g_ref, w_ref, sg_ref, su_ref, sd_ref, h_ref, mod_ref, nmod_ref, lng_ref, lnb_ref,
                    h2_ref, u_ref):
    half = D_MODEL // 2
    lo, hi = _unpack_bf16_pair(v_ref[...])
    gate = _dot(lo, sg_ref[:half, :]) + _dot(hi, sg_ref[half:, :])
    up = _dot(lo, su_ref[:half, :]) + _dot(hi, su_ref[half:, :])
    f = _dot(_silu(gate) * up, sd_ref[...])
    acc_lo = acc_hi = None
    for k in range(TOP_K):
        ylo, yhi = _unpack_bf16_pair(g_ref[k])
        wk = w_ref[:, k:k + 1]
        acc_lo = ylo * wk if acc_lo is None else acc_lo + ylo * wk
        acc_hi = yhi * wk if acc_hi is None else acc_hi + yhi * wk
    f = f + jnp.concatenate([acc_lo, acc_hi], axis=1)
    h2 = _ln(DEEPNORM_ALPHA * h_ref[...] + mod_ref[5:6, :] * f) * lng_ref[...] + lnb_ref[...]
    h2_ref[...] = h2
    u_ref[...] = (_ln(h2) * (1.0 + nmod_ref[1:2, :]) + nmod_ref[0:1, :]).astype(u_ref.dtype)


def moe_output(vp, gathered, w_tok, tok0, ws_gate, ws_up, ws_down, h1, mod, next_mod, ln_g, ln_b):
    b, l, d = h1.shape
    tm = 256
    nt = l // tm
    blk0 = tok0 // tm
    xblk = pl.BlockSpec((None, tm, d), lambda bi, i: (bi, i, 0))
    pblk = pl.BlockSpec((None, tm, d // 2), lambda bi, i: (bi, i, 0))
    vec = pl.BlockSpec((1, d), lambda bi, i: (0, 0))
    modblk = pl.BlockSpec((None, 8, d), lambda bi, i: (bi, 0, 0))
    return pl.pallas_call(
        _moe_out_kernel,
        grid=(b, nt),
        in_specs=[pblk, pl.BlockSpec((TOP_K, tm, d // 2), lambda bi, i: (0, blk0 + bi * nt + i, 0)),
                  pl.BlockSpec((tm, 8), lambda bi, i: (blk0 + bi * nt + i, 0)),
                  pl.BlockSpec((d, EXPERT_DIM), lambda bi, i: (0, 0)),
                  pl.BlockSpec((d, EXPERT_DIM), lambda bi, i: (0, 0)),
                  pl.BlockSpec((EXPERT_DIM, d), lambda bi, i: (0, 0)),
                  xblk, modblk, modblk, vec, vec],
        out_specs=[xblk, xblk],
        out_shape=[jax.ShapeDtypeStruct((b, l, d), F32), jax.ShapeDtypeStruct((b, l, d), BF16)],
        compiler_params=_cp(("parallel", "parallel")),
        name="moe_output",
    )(vp, gathered, w_tok, ws_gate, ws_up, ws_down, h1, mod, next_mod, ln_g.reshape(1, d), ln_b.reshape(1, d))


def moe_layer(streams, b_router, w_gate, w_up, w_down, ws_gate, ws_up, ws_down, ln_g, ln_b):
    half = D_MODEL // 2
    vps = [s[0].reshape(-1, half) for s in streams]
    sizes = [v.shape[0] for v in vps]
    t = sum(sizes)
    vp_all = vps[0] if len(vps) == 1 else jnp.concatenate(vps, axis=0)
    lg_all = streams[0][1] if len(vps) == 1 else jnp.concatenate([s[1] for s in streams], axis=1)
    pos, w_rows, counts = moe_route(lg_all, b_router)
    xs = sc_scatter_rows(vp_all, pos, N_EXPERTS * t)
    ys = moe_grouped_experts(xs, counts[:, 0], w_gate, w_up, w_down, t)
    gathered = sc_gather_rows(ys, pos[:TOP_K].reshape(-1)).reshape(TOP_K, t, half)
    w_tok = w_rows.T
    outs, tok0 = [], 0
    for (vp, _, h1, mod, next_mod), n in zip(streams, sizes):
        outs.append(moe_output(vp, gathered, w_tok, tok0, ws_gate, ws_up, ws_down, h1, mod, next_mod, ln_g, ln_b))
        tok0 += n
    return outs


def _layer_weights(l, w_in, b_in, dn_a_log, dn_dt_bias, gla_w2, gla_b2):
    w_main, w_small = w_in[0][l], w_in[1][l]
    bvec = b_in[l]
    cols = lambda a, n: bvec[a:a + n]
    b_main = jnp.concatenate([cols(_GATE0, 4096), cols(_DN0, 2048), cols(_CONV0, 1024), cols(_FN0, 512),
                              cols(_GLA0, 1536)])
    b_small = jnp.pad(jnp.concatenate([cols(2048, 16), cols(_GLA0 + 1536, 32)]), (0, 128 - 48))
    par = jnp.zeros((8, 128), F32)
    par = par.at[0, 8:16].set(dn_a_log[l].reshape(-1)).at[1, 8:16].set(dn_dt_bias[l].reshape(-1))
    w2p = jnp.zeros((2, 128, GLA_KDIM), F32)
    w2p = w2p.at[0, 16:32].set(gla_w2[l, 0]).at[1, 32:48].set(gla_w2[l, 1])
    return dict(w_main=w_main, b_main=b_main, w_small=w_small, b_small=b_small, par=par, w2p=w2p,
                b2=gla_b2[l].reshape(2, 1, GLA_KDIM).astype(F32))


def _project(u, lw, with_gla):
    b, l, d = u.shape
    flat = u.reshape(b * l, d)
    n = MAIN_COLS if with_gla else MAIN_GLA
    p_main = matmul_bias(flat, lw["w_main"], lw["b_main"], BF16, 1536, n).reshape(b, l, n)
    small = matmul_bias(flat, lw["w_small"], lw["b_small"], F32, 128).reshape(b, l, 128)
    return p_main, small


def kernel(x, c, ctx, c_ctx, w_mod, b_mod, w_in, b_in, dn_conv_w, dn_a_log, dn_dt_bias, dn_norm_g, gla_w2, gla_b2, gla_norm_g, conv_w, conv_b, conv_ln_g, conv_ln_b, w_branch, w_o, ln_g, ln_b, w_router, b_router, w_gate, w_up, w_down, ws_gate, ws_up, ws_down):
    batch, seq, d = x.shape
    c8 = jnp.zeros((8, d), F32).at[:batch].set(c).at[batch].set(c_ctx)
    mods = adaln_vectors(c8, w_mod, b_mod).reshape(DEPTH, 8, 6, d)
    zrow = jnp.zeros((DEPTH, 2, d), F32)
    mod_x = [jnp.concatenate([mods[l, :batch], jnp.broadcast_to(zrow[l][None], (batch, 2, d))], axis=1)
             for l in range(DEPTH)]
    mod_c = [jnp.broadcast_to(jnp.concatenate([mods[l, batch], zrow[l]], axis=0)[None], (batch, 8, d))
             for l in range(DEPTH)]

    w_packed = repack_projection_weights(w_in)
    h, hc = x, ctx
    u_x, u_c = ln_modulate(h, mod_x[0]), ln_modulate(hc, mod_c[0])
    for l in range(DEPTH):
        lw = _layer_weights(l, w_packed, b_in, dn_a_log, dn_dt_bias, gla_w2, gla_b2)
        px = _project(u_x, lw, False)
        pc = _project(u_c, lw, True)
        pg_x, lr_x = project_colmajor(u_x, lw["w_main"], lw["b_main"], MAIN_GLA, MAIN_COLS - MAIN_GLA,
                                      lw["w_small"], lw["b_small"])
        qkv_c, qkv_x = deltanet_shortconv(pc[0], dn_conv_w[l]), deltanet_shortconv(px[0], dn_conv_w[l])
        s0 = jnp.zeros((batch, 2, DN_HEADS, DN_HEAD_DIM, DN_HEAD_DIM), F32)
        ocf, ocb, s_c = deltanet_scan(qkv_c, pc[1], lw["par"], s0)
        oxf, oxb, _ = deltanet_scan(qkv_x, px[1], lw["par"], s_c)
        dn_x = gated_head_norm(oxf, oxb, px[0], MAIN_DN // 512 + 3, dn_norm_g[l])
        dn_c = gated_head_norm(ocf, ocb, pc[0], MAIN_DN // 512 + 3, dn_norm_g[l])
        g0 = jnp.zeros((batch, 2, 2, 2 * GLA_DV, 2 * GLA_DK), F32)
        gcf, gcb, gs_c = gla_scan(pc[0], MAIN_GLA // 512, pc[1], lw["w2p"], lw["b2"], g0)
        gxf, gxb, _ = gla_scan(pg_x, 0, lr_x, lw["w2p"], lw["b2"], gs_c)
        gla_x = gated_head_norm_colmajor(gxf, gxb, pg_x, 2, gla_norm_g[l])
        gla_c = gated_head_norm(gcf, gcb, pc[0], MAIN_GLA // 512 + 2, gla_norm_g[l])
        wb = w_branch[l].astype(BF16)
        wo = w_o[l].astype(BF16)
        wrt = w_router[l].T
        last = l == DEPTH - 1
        nxt = min(l + 1, DEPTH - 1)
        streams = []
        for (p, dn_f, gla_f, hh, mod, nmod) in ((px, dn_x, gla_x, h, mod_x[l], mod_x[nxt]),
                                               (pc, dn_c, gla_c, hc, mod_c[l], mod_c[nxt]))[:1 if last else 2]:
            feats = (conformer_conv(p[0], conv_w[l], conv_b[l], conv_ln_g[l], conv_ln_b[l]), dn_f,
                     fourier_mix(p[0]), gla_f)
            h1, vp, lg = merge_branches(feats, p[0], wb, wo, hh, mod, ln_g[l, 0], ln_b[l, 0], wrt)
            streams.append((vp, lg, h1, mod, nmod))
        outs = moe_layer(streams, b_router[l], w_gate[l], w_up[l], w_down[l], ws_gate[l], ws_up[l], ws_down[l],
                         ln_g[l, 1], ln_b[l, 1])
        (h, u_x) = outs[0]
        if not last:
            (hc, u_c) = outs[1]
    return h
```

```python
import functools
import math

import jax
import jax.numpy as jnp
import numpy as np
from jax import lax
from jax.experimental import pallas as pl
from jax.experimental.pallas import tpu as pltpu
from jax.experimental.pallas import tpu_sc as plsc

F32 = jnp.float32
BF16 = jnp.bfloat16
HI = lax.Precision.HIGHEST

D_MODEL = 1024
DEPTH = 4
GRID_W = 64
CHUNK = 64
EPS = 1e-6
CONV_DIM = 512
CONV_WIDTH = 31
DN_HEADS = 4
DN_HEAD_DIM = 128
DN_DIM = 512
DN_CONV_WIDTH = 5
FN_GROUPS = 4
FN_GROUP_DIM = 128
FN_DIM = 512
GLA_HEADS = 4
GLA_DK = 64
GLA_DV = 128
GLA_KDIM = 256
GLA_VDIM = 512
GLA_GATE_RANK = 16
GLA_TAU = 16.0
N_BRANCH = 4
BRANCH_DIM = 512
N_EXPERTS = 64
N_EXPERT_GROUPS = 8
TOP_GROUPS = 4
TOP_K = 6
EXPERT_DIM = 256
ROUTED_SCALE = 2.5
DEEPNORM_ALPHA = (2 * DEPTH) ** 0.25

_DN0 = 0
_GLA0 = 4 * DN_DIM + 4 * DN_HEADS
_CONV0 = _GLA0 + 2 * GLA_KDIM + 2 * GLA_VDIM + 2 * GLA_GATE_RANK
_FN0 = _CONV0 + 2 * CONV_DIM
_GATE0 = _FN0 + FN_DIM
IN_DIM = _GATE0 + N_BRANCH * D_MODEL

MAIN_GATE, MAIN_DN, MAIN_CONV, MAIN_FN, MAIN_GLA = 0, 4096, 6144, 7168, 7680
MAIN_COLS = 9216
HALO = 16
VMEM_LIMIT = 56 * 1024 * 1024


def _cp(sem, vmem=None):
    return pltpu.CompilerParams(dimension_semantics=sem, vmem_limit_bytes=vmem or VMEM_LIMIT)


def _sigmoid(x):
    return 0.5 * jnp.tanh(0.5 * x) + 0.5


def _silu(x):
    return x * _sigmoid(x)


def _softplus(x):
    return jnp.maximum(x, 0.0) + jnp.log(1.0 + jnp.exp(-jnp.abs(x)))


def _ln(x):
    mu = jnp.mean(x, axis=-1, keepdims=True)
    xc = x - mu
    var = jnp.mean(xc * xc, axis=-1, keepdims=True)
    return xc * lax.rsqrt(var + EPS)


def _dot(a, b):
    return jnp.dot(a.astype(BF16), b.astype(BF16), preferred_element_type=F32)


def _dot_nt(a, b):
    return lax.dot_general(a.astype(BF16), b.astype(BF16), (((1,), (1,)), ((), ())), preferred_element_type=F32)


def _dot_tn(a, b):
    return lax.dot_general(a.astype(BF16), b.astype(BF16), (((0,), (0,)), ((), ())), preferred_element_type=F32)


def _dot_hi(a, b):
    return jnp.dot(a, b, precision=HI, preferred_element_type=F32)


_HI16 = -65536


def _pack_bf16_pair(x):
    n = x.shape[-1] // 2
    bits = lambda t: lax.bitcast_convert_type(t.astype(BF16).astype(F32), jnp.int32)
    return (bits(x[:, n:]) & _HI16) | ((bits(x[:, :n]) >> 16) & 0xFFFF)


def _unpack_bf16_pair(w):
    lo = lax.bitcast_convert_type(w << 16, F32)
    hi = lax.bitcast_convert_type(w & _HI16, F32)
    return lo, hi


def _mod_kernel(c_ref, w_ref, b_ref, o_ref):
    o_ref[...] = _dot_hi(_silu(c_ref[...]), w_ref[...]) + b_ref[...]


def adaln_vectors(c8, w_mod, b_mod):
    depth, d, n = w_mod.shape
    tn = 1536
    return pl.pallas_call(
        _mod_kernel,
        grid=(depth, n // tn),
        in_specs=[pl.BlockSpec((8, d), lambda l, j: (0, 0)),
                  pl.BlockSpec((None, d, tn), lambda l, j: (l, 0, j)),
                  pl.BlockSpec((None, 1, tn), lambda l, j: (l, 0, j))],
        out_specs=pl.BlockSpec((None, 8, tn), lambda l, j: (l, 0, j)),
        out_shape=jax.ShapeDtypeStruct((depth, 8, n), F32),
        compiler_params=_cp(("parallel", "parallel")),
        name="adaln_vectors",
    )(c8, w_mod, b_mod.reshape(depth, 1, n))


def _repack_kernel(w_ref, main_ref, small_ref):
    seg = lambda a, n: w_ref[:, a:a + n].astype(BF16)
    main_ref[:, MAIN_GATE:MAIN_GATE + 4096] = seg(_GATE0, 4096)
    main_ref[:, MAIN_DN:MAIN_DN + 2048] = seg(_DN0, 2048)
    main_ref[:, MAIN_CONV:MAIN_CONV + 1024] = seg(_CONV0, 1024)
    main_ref[:, MAIN_FN:MAIN_FN + 512] = seg(_FN0, 512)
    main_ref[:, MAIN_GLA:MAIN_GLA + 1536] = seg(_GLA0, 1536)
    small = jnp.concatenate([w_ref[:, 2048:2064], w_ref[:, _GLA0 + 1536:_GLA0 + 1568],
                             jnp.zeros((w_ref.shape[0], LANE - 48), F32)], axis=1)
    small_ref[...] = small.astype(BF16)


def repack_projection_weights(w_in):
    depth, d, n = w_in.shape
    tr = 128
    return pl.pallas_call(
        _repack_kernel,
        grid=(depth, d // tr),
        in_specs=[pl.BlockSpec((None, tr, n), lambda l, i: (l, i, 0))],
        out_specs=[pl.BlockSpec((None, tr, MAIN_COLS), lambda l, i: (l, i, 0)),
                   pl.BlockSpec((None, tr, LANE), lambda l, i: (l, i, 0))],
        out_shape=[jax.ShapeDtypeStruct((depth, d, MAIN_COLS), BF16), jax.ShapeDtypeStruct((depth, d, LANE), BF16)],
        compiler_params=_cp(("parallel", "parallel")),
        name="repack_projection_weights",
    )(w_in)


def _lnmod_kernel(h_ref, mod_ref, o_ref):
    u = _ln(h_ref[...]) * (1.0 + mod_ref[1:2, :]) + mod_ref[0:1, :]
    o_ref[...] = u.astype(o_ref.dtype)


def ln_modulate(h, mod):
    b, l, d = h.shape
    tm = min(l, 512)
    return pl.pallas_call(
        _lnmod_kernel,
        grid=(b, l // tm),
        in_specs=[pl.BlockSpec((None, tm, d), lambda i, j: (i, j, 0)),
                  pl.BlockSpec((None, 8, d), lambda i, j: (i, 0, 0))],
        out_specs=pl.BlockSpec((None, tm, d), lambda i, j: (i, j, 0)),
        out_shape=jax.ShapeDtypeStruct((b, l, d), BF16),
        compiler_params=_cp(("parallel", "parallel")),
        name="ln_modulate",
    )(h, mod)


def _mm_kernel(x_ref, w_ref, b_ref, o_ref):
    o_ref[...] = (jnp.dot(x_ref[...], w_ref[...], preferred_element_type=F32) + b_ref[...]).astype(o_ref.dtype)


def matmul_bias(x, w, b, out_dtype, tn, n=None):
    m, k = x.shape
    n = n or w.shape[1]
    b = b[:n]
    tm = 1024 if m % 1024 == 0 else 512
    return pl.pallas_call(
        _mm_kernel,
        grid=(m // tm, n // tn),
        in_specs=[pl.BlockSpec((tm, k), lambda i, j: (i, 0)),
                  pl.BlockSpec((k, tn), lambda i, j: (0, j)),
                  pl.BlockSpec((1, tn), lambda i, j: (0, j))],
        out_specs=pl.BlockSpec((tm, tn), lambda i, j: (i, j)),
        out_shape=jax.ShapeDtypeStruct((m, n), out_dtype),
        compiler_params=_cp(("parallel", "parallel")),
        name="matmul_bias",
    )(x, w, b.reshape(1, n).astype(F32))


SUBLANES = 8


def _conv_shifted_copies(g_scr, width, tl):
    pad = (width - 1) // 2
    offs = [HALO + k - pad for k in range(width)]
    n = tl + (max(offs) // SUBLANES) * SUBLANES
    step = 64
    for b in sorted({o % SUBLANES for o in offs} - {0}):
        for s in range(0, n, step):
            m = min(step, n - s)
            g_scr[b, s:s + m, :] = g_scr[0, pl.ds(s + b, m), :]


def _conv_rows(g_scr, w_ref, width, r0, rs):
    pad = (width - 1) // 2
    acc = None
    for k in range(width):
        o = HALO + k - pad
        term = w_ref[k:k + 1, :] * g_scr[o % SUBLANES, pl.ds(r0 + (o // SUBLANES) * SUBLANES, rs), :]
        acc = term if acc is None else acc + term
    return acc


def _conformer_kernel(vc, vp, vn, gc, gp, gn, w_ref, cb_ref, lg_ref, lb_ref, o_ref, g_scr, *, tl, rs):
    i = pl.program_id(1)
    nt = pl.num_programs(1)
    glu = lambda v, g: v[...].astype(F32) * _sigmoid(g[...].astype(F32))
    g_scr[0, HALO:HALO + tl, :] = glu(vc, gc)
    g_scr[0, 0:HALO, :] = jnp.where(i > 0, glu(vp, gp), 0.0)
    g_scr[0, HALO + tl:2 * HALO + tl, :] = jnp.where(i < nt - 1, glu(vn, gn), 0.0)
    _conv_shifted_copies(g_scr, CONV_WIDTH, tl)
    for s in range(tl // rs):
        y = _conv_rows(g_scr, w_ref, CONV_WIDTH, s * rs, rs) + cb_ref[...]
        y = _silu(_ln(y) * lg_ref[...] + lb_ref[...])
        o_ref[s * rs:(s + 1) * rs, :] = y.astype(o_ref.dtype)


def conformer_conv(p_main, conv_w, conv_b, ln_g, ln_b):
    b, l, _ = p_main.shape
    c = CONV_DIM
    tl = min(l, 512)
    rs = 64
    hb = tl // HALO
    nhb = l // HALO
    vblk, gblk = MAIN_CONV // c, MAIN_CONV // c + 1
    cur = lambda cb: pl.BlockSpec((None, tl, c), lambda bi, i: (bi, i, cb))
    prv = lambda cb: pl.BlockSpec((None, HALO, c), lambda bi, i: (bi, jnp.maximum(i * hb - 1, 0), cb))
    nxt = lambda cb: pl.BlockSpec((None, HALO, c), lambda bi, i: (bi, jnp.minimum((i + 1) * hb, nhb - 1), cb))
    vec = pl.BlockSpec((1, c), lambda bi, i: (0, 0))
    return pl.pallas_call(
        functools.partial(_conformer_kernel, tl=tl, rs=rs),
        grid=(b, l // tl),
        in_specs=[cur(vblk), prv(vblk), nxt(vblk), cur(gblk), prv(gblk), nxt(gblk),
                  pl.BlockSpec((CONV_WIDTH, c), lambda bi, i: (0, 0)), vec, vec, vec],
        out_specs=pl.BlockSpec((None, tl, c), lambda bi, i: (bi, i, 0)),
        out_shape=jax.ShapeDtypeStruct((b, l, c), BF16),
        scratch_shapes=[pltpu.VMEM((SUBLANES, tl + 2 * HALO, c), F32)],
        compiler_params=_cp(("parallel", "parallel")),
        name="conformer_conv",
    )(p_main, p_main, p_main, p_main, p_main, p_main, conv_w,
      conv_b.reshape(1, c), ln_g.reshape(1, c), ln_b.reshape(1, c))


def _shortconv_kernel(xc, xp, xn, w_ref, o_ref, g_scr, *, tl, rs):
    i = pl.program_id(1)
    nt = pl.num_programs(1)
    g_scr[0, HALO:HALO + tl, :] = xc[...].astype(F32)
    g_scr[0, 0:HALO, :] = jnp.where(i > 0, xp[...].astype(F32), 0.0)
    g_scr[0, HALO + tl:2 * HALO + tl, :] = jnp.where(i < nt - 1, xn[...].astype(F32), 0.0)
    _conv_shifted_copies(g_scr, DN_CONV_WIDTH, tl)
    is_qk = pl.program_id(2) < 2
    for s in range(tl // rs):
        y = _silu(_conv_rows(g_scr, w_ref, DN_CONV_WIDTH, s * rs, rs))
        for h in range(DN_HEADS):
            yh = y[:, h * DN_HEAD_DIM:(h + 1) * DN_HEAD_DIM]
            yh = jnp.where(is_qk, _l2n(yh), yh)
            o_ref[s * rs:(s + 1) * rs, h * DN_HEAD_DIM:(h + 1) * DN_HEAD_DIM] = yh.astype(o_ref.dtype)


def deltanet_shortconv(p_main, dn_conv_w):
    b, l, _ = p_main.shape
    c = 512
    tl = min(l, 512)
    rs = 64
    hb = tl // HALO
    nhb = l // HALO
    cb0 = MAIN_DN // c
    return pl.pallas_call(
        functools.partial(_shortconv_kernel, tl=tl, rs=rs),
        grid=(b, l // tl, 3),
        in_specs=[pl.BlockSpec((None, tl, c), lambda bi, i, j: (bi, i, cb0 + j)),
                  pl.BlockSpec((None, HALO, c), lambda bi, i, j: (bi, jnp.maximum(i * hb - 1, 0), cb0 + j)),
                  pl.BlockSpec((None, HALO, c), lambda bi, i, j: (bi, jnp.minimum((i + 1) * hb, nhb - 1), cb0 + j)),
                  pl.BlockSpec((DN_CONV_WIDTH, c), lambda bi, i, j: (0, j))],
        out_specs=pl.BlockSpec((None, tl, c), lambda bi, i, j: (bi, i, j)),
        out_shape=jax.ShapeDtypeStruct((b, l, 3 * c), BF16),
        scratch_shapes=[pltpu.VMEM((SUBLANES, tl + 2 * HALO, c), F32)],
        compiler_params=_cp(("parallel", "parallel", "parallel")),
        name="deltanet_shortconv",
    )(p_main, p_main, p_main, dn_conv_w)


def _chunk_masks(rev):
    r = lax.broadcasted_iota(jnp.int32, (CHUNK, CHUNK), 0)
    c = lax.broadcasted_iota(jnp.int32, (CHUNK, CHUNK), 1)
    if rev:
        cum, sx, incl, strict = c >= r, r < c, c >= r, c > r
    else:
        cum, sx, incl, strict = c <= r, r > c, c <= r, c < r
    return cum.astype(F32), sx.astype(F32), incl, strict, (r == c).astype(F32)


def _dn_chunks(chains):
    each = lambda f: [f(c) for c in chains]
    scale = DN_HEAD_DIM ** -0.5
    n = CHUNK
    rows = lambda top, bot: jnp.concatenate([top, bot], axis=0)
    decay = each(lambda c: jnp.where(c["incl"], jnp.exp(c["gs"] - c["gs_row"]), 0.0))
    kb = each(lambda c: c["k"] * c["beta"])
    qs = each(lambda c: c["q"] * scale)
    kq = [_dot_nt(rows(kbi, qi), c["k"]) for c, kbi, qi in zip(chains, kb, qs)]
    m = [jnp.where(c["strict"], r[:n] * di, 0.0) for c, r, di in zip(chains, kq, decay)]
    a = [r[n:] * di for r, di in zip(kq, decay)]
    inv = [c["eye"] - mi for c, mi in zip(chains, m)]
    p = [_dot(mi, mi) for mi in m]
    for _ in range(int(math.log2(CHUNK)) - 2):
        r = [_dot(rows(pi, ii), pi) for pi, ii in zip(p, inv)]
        inv = [ii + ri[n:] for ii, ri in zip(inv, r)]
        p = [ri[:n] for ri in r]
    inv = [ii + _dot(ii, pi) for ii, pi in zip(inv, p)]
    uw = [_dot(ii, jnp.concatenate([c["v"] * c["beta"], kbi * c["eg"]], axis=1)) for c, ii, kbi in zip(chains, inv, kb)]
    hd = DN_HEAD_DIM
    ws = [_dot(rows(r[:, hd:], qi * c["eg"]), c["s"]) for c, r, qi in zip(chains, uw, qs)]
    v_new = [r[:, :hd] - wsi[:n] for r, wsi in zip(uw, ws)]
    o = [wsi[n:] + _dot(ai, vi) for wsi, ai, vi in zip(ws, a, v_new)]
    s_new = [c["s"] * c["etot"] + _dot_tn(c["k"] * c["egt"], vi) for c, vi in zip(chains, v_new)]
    return list(zip(o, s_new))


def _l2n(t):
    return t * lax.rsqrt(jnp.sum(t * t, axis=-1, keepdims=True) + EPS)


def _dn_scan_kernel(xf_ref, bgf_ref, xb_ref, bgb_ref, par_ref, s0_ref, of_ref, ob_ref, sfin_ref, *s_scr, nb):
    i = pl.program_id(0)
    n = pl.num_programs(0)
    hd = DN_HEAD_DIM
    chain = lambda b, d, h: s_scr[(b * 2 + d) * DN_HEADS + h]

    @pl.when(i == 0)
    def _():
        for b in range(nb):
            for d in range(2):
                for h in range(DN_HEADS):
                    chain(b, d, h)[...] = s0_ref[b, d, h]

    chains, sinks = [], []
    for d, (x_ref, bg_ref, o_ref) in enumerate(((xf_ref, bgf_ref, of_ref), (xb_ref, bgb_ref, ob_ref))):
        cum, _, incl, strict, eye = _chunk_masks(rev=bool(d))
        for b in range(nb):
            bg = bg_ref[b]
            beta_all = _sigmoid(bg)
            g_all = -jnp.exp(par_ref[0:1, :]) * _softplus(bg + par_ref[1:2, :])
            gs_all = _dot_hi(cum, g_all)
            tot = gs_all[0:1, :] if d else gs_all[CHUNK - 1:CHUNK, :]
            eg_all, egt_all, etot_all = jnp.exp(gs_all), jnp.exp(tot - gs_all), jnp.exp(tot)
            gs_t = jnp.concatenate([gs_all, jnp.zeros_like(gs_all)], axis=0).T
            for h in range(DN_HEADS):
                cb = d * DN_HEADS + h
                col = 2 * DN_HEADS + cb
                chains.append(dict(
                    q=x_ref[b, :, h * hd:(h + 1) * hd].astype(F32),
                    k=x_ref[b, :, DN_DIM + h * hd:DN_DIM + (h + 1) * hd].astype(F32),
                    v=x_ref[b, :, 2 * DN_DIM + h * hd:2 * DN_DIM + (h + 1) * hd].astype(F32),
                    beta=beta_all[:, cb:cb + 1], gs=gs_all[:, col:col + 1], eg=eg_all[:, col:col + 1],
                    egt=egt_all[:, col:col + 1], etot=etot_all[:, col:col + 1], gs_row=gs_t[col:col + 1, :CHUNK],
                    s=chain(b, d, h)[...], incl=incl, strict=strict, eye=eye))
                sinks.append((o_ref, b, h, chain(b, d, h)))
    for (o_ref, b, h, s_ref), (o, s_new) in zip(sinks, _dn_chunks(chains)):
        o_ref[b, :, h * hd:(h + 1) * hd] = o.astype(o_ref.dtype)
        s_ref[...] = s_new

    @pl.when(i == n - 1)
    def _():
        for b in range(nb):
            for d in range(2):
                for h in range(DN_HEADS):
                    sfin_ref[b, d, h] = chain(b, d, h)[...]


def deltanet_scan(qkv, bg, par, s0):
    b, l, _ = qkv.shape
    n = l // CHUNK
    fwd = lambda w: pl.BlockSpec((b, CHUNK, w), lambda i: (0, i, 0))
    bwd = lambda w: pl.BlockSpec((b, CHUNK, w), lambda i: (0, n - 1 - i, 0))
    st = pl.BlockSpec(s0.shape, lambda i: (0, 0, 0, 0, 0))
    return pl.pallas_call(
        functools.partial(_dn_scan_kernel, nb=b),
        grid=(n,),
        in_specs=[fwd(3 * DN_DIM), fwd(128), bwd(3 * DN_DIM), bwd(128),
                  pl.BlockSpec((8, 128), lambda i: (0, 0)), st],
        out_specs=[fwd(DN_DIM), bwd(DN_DIM), st],
        out_shape=[jax.ShapeDtypeStruct((b, l, DN_DIM), BF16), jax.ShapeDtypeStruct((b, l, DN_DIM), BF16),
                   jax.ShapeDtypeStruct(s0.shape, F32)],
        scratch_shapes=[pltpu.VMEM((DN_HEAD_DIM, DN_HEAD_DIM), F32)] * (b * 2 * DN_HEADS),
        compiler_params=_cp(("arbitrary",)),
        name="deltanet_scan",
    )(qkv, bg, qkv, bg, par, s0)


GLA_SUB = 16


def _gla_chunks(chains):
    row = lax.broadcasted_iota(jnp.int32, (CHUNK, 1), 0)
    lane = lax.broadcasted_iota(jnp.int32, (1, 2 * GLA_DK), 1)
    o_inter = [_dot_nt(c["qp"] * jnp.exp(c["bp"]), c["st"]) for c in chains]
    blocks = [([], []) for _ in chains]
    for blk in range(CHUNK // GLA_SUB):
        i0 = blk * GLA_SUB
        mid = i0 + GLA_SUB // 2
        ri = lax.broadcasted_iota(jnp.int32, (GLA_SUB, CHUNK), 0) + i0
        ci = lax.broadcasted_iota(jnp.int32, (GLA_SUB, CHUNK), 1)
        for c, blks in zip(chains, blocks):
            bp = c["bp"]
            ref = bp[mid:mid + 1, :]
            qt = c["qp"][i0:i0 + GLA_SUB, :] * jnp.exp(bp[i0:i0 + GLA_SUB, :] - ref)
            valid = (row >= i0) if c["rev"] else (row < i0 + GLA_SUB)
            kt = c["kp"] * jnp.exp(jnp.where(valid, ref - bp, 0.0))
            causal = (ci >= ri) if c["rev"] else (ci <= ri)
            qh = [jnp.where((lane >= hh * GLA_DK) & (lane < (hh + 1) * GLA_DK), qt, 0.0) for hh in range(2)]
            both = _dot_nt(jnp.concatenate(qh, axis=0), kt)
            for hh in range(2):
                blks[hh].append(jnp.where(causal, both[hh * GLA_SUB:(hh + 1) * GLA_SUB], 0.0))
    o_intra = [[_dot(jnp.concatenate(blks[hh], axis=0), c["vp"][:, hh * GLA_DV:(hh + 1) * GLA_DV]) for hh in range(2)]
               for c, blks in zip(chains, blocks)]
    srow = lax.broadcasted_iota(jnp.int32, (2 * GLA_DV, 2 * GLA_DK), 0)
    scol = lax.broadcasted_iota(jnp.int32, (2 * GLA_DV, 2 * GLA_DK), 1)
    bd = (srow < GLA_DV) == (scol < GLA_DK)
    out = []
    for c, oi, ox in zip(chains, o_inter, o_intra):
        bp = c["bp"]
        b_last = bp[0:1, :] if c["rev"] else bp[CHUNK - 1:CHUNK, :]
        st_new = jnp.where(bd, c["st"] * jnp.exp(b_last) + _dot_tn(c["vp"], c["kp"] * jnp.exp(b_last - bp)), 0.0)
        out.append((oi + jnp.concatenate(ox, axis=1), st_new))
    return out


def _gla_scan_kernel(qkf, vf, lrf, qkb, vb, lrb, w2_ref, b2_ref, s0_ref, of_ref, ob_ref, sfin_ref, *s_scr, nb):
    i = pl.program_id(0)
    n = pl.num_programs(0)
    npair = GLA_HEADS // 2
    chain = lambda b, d, p: s_scr[(b * 2 + d) * npair + p]

    @pl.when(i == 0)
    def _():
        for b in range(nb):
            for d in range(2):
                for p in range(npair):
                    chain(b, d, p)[...] = s0_ref[b, d, p]

    chains, sinks = [], []
    zs = [[_dot_hi(lr_ref[b], w2_ref[d]) + b2_ref[d] for b in range(nb)]
          for d, lr_ref in enumerate((lrf, lrb))]
    for d, (qk_ref, v_ref, o_ref) in enumerate(((qkf, vf, of_ref), (qkb, vb, ob_ref))):
        cum = _chunk_masks(rev=bool(d))[0]
        for b in range(nb):
            bs = _dot_hi(cum, -_softplus(-zs[d][b]) * (1.0 / GLA_TAU))
            for p in range(npair):
                lo = p * 2 * GLA_DK
                chains.append(dict(
                    qp=qk_ref[b, :, lo:lo + 2 * GLA_DK].astype(F32) * (GLA_DK ** -0.5),
                    kp=qk_ref[b, :, GLA_KDIM + lo:GLA_KDIM + lo + 2 * GLA_DK].astype(F32),
                    vp=v_ref[b, :, p * 2 * GLA_DV:(p + 1) * 2 * GLA_DV].astype(F32),
                    bp=bs[:, lo:lo + 2 * GLA_DK], st=chain(b, d, p)[...], rev=bool(d)))
                sinks.append((o_ref, b, p, chain(b, d, p)))
    for (o_ref, b, p, s_ref), (o, st_new) in zip(sinks, _gla_chunks(chains)):
        o_ref[b, :, p * 2 * GLA_DV:(p + 1) * 2 * GLA_DV] = o.astype(o_ref.dtype)
        s_ref[...] = st_new

    @pl.when(i == n - 1)
    def _():
        for b in range(nb):
            for d in range(2):
                for p in range(npair):
                    sfin_ref[b, d, p] = chain(b, d, p)[...]


def gla_scan(p, col0, small, w2p, b2, s0):
    b, l, _ = p.shape
    n = l // CHUNK
    fwd = lambda w, cb: pl.BlockSpec((b, CHUNK, w), lambda i: (0, i, cb))
    bwd = lambda w, cb: pl.BlockSpec((b, CHUNK, w), lambda i: (0, n - 1 - i, cb))
    st = pl.BlockSpec(s0.shape, lambda i: (0, 0, 0, 0, 0))
    return pl.pallas_call(
        functools.partial(_gla_scan_kernel, nb=b),
        grid=(n,),
        in_specs=[fwd(512, col0), fwd(512, col0 + 1), fwd(128, 0), bwd(512, col0), bwd(512, col0 + 1), bwd(128, 0),
                  pl.BlockSpec((2, 128, GLA_KDIM), lambda i: (0, 0, 0)),
                  pl.BlockSpec((2, 1, GLA_KDIM), lambda i: (0, 0, 0)), st],
        out_specs=[fwd(GLA_VDIM, 0), bwd(GLA_VDIM, 0), st],
        out_shape=[jax.ShapeDtypeStruct((b, l, GLA_VDIM), BF16)] * 2 + [jax.ShapeDtypeStruct(s0.shape, F32)],
        scratch_shapes=[pltpu.VMEM((2 * GLA_DV, 2 * GLA_DK), F32)] * (b * 2 * (GLA_HEADS // 2)),
        compiler_params=_cp(("arbitrary",)),
        name="gla_scan",
    )(p, p, small, p, p, small, w2p, b2, s0)


CM_ROWS = 16
LANE = 128


def _proj_colmajor_kernel(x_ref, w_ref, b_ref, ws_ref, bs_ref, o_ref, os_ref, scr, *, ncol):
    res = jnp.dot(x_ref[...], w_ref[...], preferred_element_type=F32) + b_ref[...]
    for k in range(ncol):
        scr[k] = res[:, k * LANE:(k + 1) * LANE]
    scr[ncol] = jnp.dot(x_ref[...], ws_ref[...], preferred_element_type=F32) + bs_ref[...]
    for c in range(GRID_W):
        for k in range(ncol):
            o_ref[c, :, k * LANE:(k + 1) * LANE] = scr[k, pl.ds(c, CM_ROWS, stride=GRID_W), :].astype(o_ref.dtype)
        os_ref[c] = scr[ncol, pl.ds(c, CM_ROWS, stride=GRID_W), :]


def project_colmajor(u, w, bias, col0, n, w_small, b_small):
    b, l, d = u.shape
    cblk = col0 // n
    bias = bias[col0:col0 + n]
    rows = l // GRID_W
    tm = CM_ROWS * GRID_W
    nt = rows // CM_ROWS
    o, os_ = pl.pallas_call(
        functools.partial(_proj_colmajor_kernel, ncol=n // LANE),
        grid=(b, nt),
        in_specs=[pl.BlockSpec((None, tm, d), lambda bi, i: (bi, i, 0)),
                  pl.BlockSpec((d, n), lambda bi, i: (0, cblk)), pl.BlockSpec((1, n), lambda bi, i: (0, 0)),
                  pl.BlockSpec((d, LANE), lambda bi, i: (0, 0)), pl.BlockSpec((1, LANE), lambda bi, i: (0, 0))],
        out_specs=[pl.BlockSpec((None, GRID_W, CM_ROWS, n), lambda bi, i: (bi, 0, i, 0)),
                   pl.BlockSpec((None, GRID_W, CM_ROWS, LANE), lambda bi, i: (bi, 0, i, 0))],
        out_shape=[jax.ShapeDtypeStruct((b, GRID_W, rows, n), BF16), jax.ShapeDtypeStruct((b, GRID_W, rows, LANE), F32)],
        scratch_shapes=[pltpu.VMEM((n // LANE + 1, tm, LANE), F32)],
        compiler_params=_cp(("parallel", "parallel")),
        name="project_colmajor",
    )(u, w, bias.reshape(1, n).astype(F32), w_small, b_small.reshape(1, LANE).astype(F32))
    return o.reshape(b, l, n), os_.reshape(b, l, LANE)


def _headnorm_colmajor_kernel(of_ref, ob_ref, z_ref, g_ref, o_ref, scr):
    o = of_ref[...].astype(F32) + ob_ref[...].astype(F32)
    z = z_ref[...].astype(F32)
    for h in range(4):
        sl = slice(h * LANE, (h + 1) * LANE)
        oh = o[:, :, sl]
        y = oh * lax.rsqrt(jnp.mean(oh * oh, axis=-1, keepdims=True) + EPS) * g_ref[...] * _silu(z[:, :, sl])
        for c in range(GRID_W):
            scr[h, pl.ds(c, CM_ROWS, stride=GRID_W), :] = y[c]
    for h in range(4):
        o_ref[:, h * LANE:(h + 1) * LANE] = scr[h].astype(o_ref.dtype)


def gated_head_norm_colmajor(o_f, o_b, p_gla, z_blk, g):
    b, l, c = o_f.shape
    rows = l // GRID_W
    tm = CM_ROWS * GRID_W
    cm = lambda t: t.reshape(b, GRID_W, rows, t.shape[-1])
    blk = lambda cb: pl.BlockSpec((None, GRID_W, CM_ROWS, c), lambda bi, i: (bi, 0, i, cb))
    return pl.pallas_call(
        _headnorm_colmajor_kernel,
        grid=(b, rows // CM_ROWS),
        in_specs=[blk(0), blk(0), blk(z_blk), pl.BlockSpec((1, LANE), lambda bi, i: (0, 0))],
        out_specs=pl.BlockSpec((None, tm, c), lambda bi, i: (bi, i, 0)),
        out_shape=jax.ShapeDtypeStruct((b, l, c), BF16),
        scratch_shapes=[pltpu.VMEM((4, tm, LANE), F32)],
        compiler_params=_cp(("parallel", "parallel")),
        name="gated_head_norm_colmajor",
    )(cm(o_f), cm(o_b), cm(p_gla), g.reshape(1, LANE).astype(F32))


def _headnorm_kernel(of_ref, ob_ref, z_ref, g_ref, o_ref):
    o = of_ref[...].astype(F32) + ob_ref[...].astype(F32)
    z = z_ref[...].astype(F32)
    for h in range(4):
        sl = slice(h * 128, (h + 1) * 128)
        oh = o[:, sl]
        oh = oh * lax.rsqrt(jnp.mean(oh * oh, axis=-1, keepdims=True) + EPS) * g_ref[...]
        o_ref[:, sl] = (oh * _silu(z[:, sl])).astype(o_ref.dtype)


def gated_head_norm(o_f, o_b, z_arr, z_blk, g):
    b, l, c = o_f.shape
    tl = min(l, 512)
    blk = pl.BlockSpec((None, tl, c), lambda bi, i: (bi, i, 0))
    return pl.pallas_call(
        _headnorm_kernel,
        grid=(b, l // tl),
        in_specs=[blk, blk, pl.BlockSpec((None, tl, c), lambda bi, i: (bi, i, z_blk)),
                  pl.BlockSpec((1, 128), lambda bi, i: (0, 0))],
        out_specs=blk,
        out_shape=jax.ShapeDtypeStruct((b, l, c), BF16),
        compiler_params=_cp(("parallel", "parallel")),
        name="gated_head_norm",
    )(o_f, o_b, z_arr, g.reshape(1, 128).astype(F32))


def _dft_tables(n):
    ang = 2.0 * np.pi * (np.outer(np.arange(n), np.arange(n)) % n) / n
    return np.cos(ang), np.sin(ang)


def _fnet_small_kernel(x_ref, wc_ref, cl_ref, sl_ref, o_ref):
    y = jnp.dot(x_ref[...], wc_ref[...], preferred_element_type=F32)
    out = _dot(cl_ref[...], y[:, :FN_GROUP_DIM]) + _dot(sl_ref[...], y[:, FN_GROUP_DIM:])
    o_ref[...] = out.astype(o_ref.dtype)


def _fnet_big_kernel(x_ref, wc_ref, f1_ref, twc_ref, tws_ref, f2_ref, o_ref, y_scr, yi_scr, b_scr, *, n1):
    n2 = FN_GROUP_DIM
    l = n1 * n2
    rb = 512

    def step0(r, c):
        r0 = pl.multiple_of(r * rb, rb)
        y = jnp.dot(x_ref[pl.ds(r0, rb), :], wc_ref[...], preferred_element_type=F32)
        y_scr[pl.ds(r0, rb), :] = y[:, :n2]
        yi_scr[pl.ds(r0, rb), :] = y[:, n2:]
        return c

    lax.fori_loop(0, l // rb, step0, 0)

    def step1(j, c):
        mr = _dot(f1_ref[...], y_scr[pl.ds(j, n1, stride=n2), :])
        mi = _dot(f1_ref[...], yi_scr[pl.ds(j, n1, stride=n2), :])
        ar = mr[:n1] + mi[n1:]
        ai = mi[:n1] - mr[n1:]
        tc, ts = twc_ref[j], tws_ref[j]
        b_scr[pl.ds(j, n1, stride=2 * n2), :] = ar * tc + ai * ts
        b_scr[pl.ds(n2 + j, n1, stride=2 * n2), :] = ai * tc - ar * ts
        return c

    lax.fori_loop(0, n2, step1, 0, unroll=8)

    def step2(k1, c):
        bk = b_scr[pl.ds(pl.multiple_of(k1 * 2 * n2, 2 * n2), 2 * n2), :]
        y_scr[pl.ds(k1, n2, stride=n1), :] = _dot(f2_ref[...], bk)
        return c

    lax.fori_loop(0, n1, step2, 0, unroll=8)

    def step3(r, c):
        r0 = pl.multiple_of(r * rb, rb)
        o_ref[pl.ds(r0, rb), :] = y_scr[pl.ds(r0, rb), :].astype(o_ref.dtype)
        return c

    lax.fori_loop(0, l // rb, step3, 0)


def fourier_mix(p_main):
    b, l, _ = p_main.shape
    gd = FN_GROUP_DIM
    cc, sc = _dft_tables(gd)
    wc = jnp.asarray(np.concatenate([cc, -sc], axis=1) / math.sqrt(gd), BF16)
    x_spec = pl.BlockSpec((None, l, gd), lambda bi, g: (bi, 0, MAIN_FN // gd + g))
    o_spec = pl.BlockSpec((None, l, gd), lambda bi, g: (bi, 0, g))
    full = lambda shape: pl.BlockSpec(shape, lambda bi, g: (0,) * len(shape))
    out_shape = jax.ShapeDtypeStruct((b, l, FN_DIM), BF16)
    if l <= 512:
        cl, sl = _dft_tables(l)
        scale = 1.0 / math.sqrt(l)
        return pl.pallas_call(
            _fnet_small_kernel, grid=(b, FN_GROUPS),
            in_specs=[x_spec, full((gd, 2 * gd)), full((l, l)), full((l, l))],
            out_specs=o_spec, out_shape=out_shape,
            compiler_params=_cp(("parallel", "parallel")), name="fourier_mix_small",
        )(p_main, wc, jnp.asarray(cl * scale, BF16), jnp.asarray(sl * scale, BF16))
    n1, n2 = l // gd, gd
    c1, s1 = _dft_tables(n1)
    c2, s2 = _dft_tables(n2)
    f1 = jnp.asarray(np.concatenate([c1, s1], axis=0), BF16)
    f2 = jnp.asarray(np.concatenate([c2, s2], axis=1) / math.sqrt(l), BF16)
    ang = 2.0 * np.pi * np.outer(np.arange(n2), np.arange(n1)) / l
    twc = jnp.asarray(np.broadcast_to(np.cos(ang)[:, :, None], (n2, n1, gd)), F32)
    tws = jnp.asarray(np.broadcast_to(np.sin(ang)[:, :, None], (n2, n1, gd)), F32)
    return pl.pallas_call(
        functools.partial(_fnet_big_kernel, n1=n1), grid=(b, FN_GROUPS),
        in_specs=[x_spec, full((gd, 2 * gd)), full((2 * n1, n1)), full((n2, n1, gd)), full((n2, n1, gd)),
                  full((n2, 2 * n2))],
        out_specs=o_spec, out_shape=out_shape,
        scratch_shapes=[pltpu.VMEM((l, gd), F32), pltpu.VMEM((l, gd), F32), pltpu.VMEM((n1 * 2 * n2, gd), F32)],
        compiler_params=_cp(("parallel", "parallel")), name="fourier_mix_big",
    )(p_main, wc, f1, twc, tws, f2)


def _merge_kernel(fa, fb, fc, fd, gt_ref, wb_ref, wo_ref, h_ref, mod_ref, lng_ref, lnb_ref, wr_ref,
                  h1_ref, v_ref, lg_ref):
    half = h_ref.shape[0] // 2
    for r0 in (0, half):
        rs = slice(r0, r0 + half)
        acc = None
        for n, f_ref in enumerate((fa, fb, fc, fd)):
            proj = jnp.dot(f_ref[rs, :], wb_ref[n], preferred_element_type=F32)
            term = _sigmoid(gt_ref[rs, n * D_MODEL:(n + 1) * D_MODEL].astype(F32)) * proj
            acc = term if acc is None else acc + term
        y = _dot(acc, wo_ref[...])
        h1 = _ln(DEEPNORM_ALPHA * h_ref[rs, :] + mod_ref[2:3, :] * y) * lng_ref[...] + lnb_ref[...]
        h1_ref[rs, :] = h1
        v = _ln(h1) * (1.0 + mod_ref[4:5, :]) + mod_ref[3:4, :]
        v_ref[rs, :] = _pack_bf16_pair(v)
        lg_ref[:, rs] = lax.dot_general(wr_ref[...], v, (((1,), (1,)), ((), ())), precision=HI,
                                        preferred_element_type=F32)


def merge_branches(feats, p_main, w_branch, w_o, h, mod, ln_g, ln_b, w_router_t):
    b, l, d = h.shape
    tm = 256
    nt = l // tm
    fblk = pl.BlockSpec((None, tm, BRANCH_DIM), lambda bi, i: (bi, i, 0))
    hblk = pl.BlockSpec((None, tm, d), lambda bi, i: (bi, i, 0))
    vec = pl.BlockSpec((1, d), lambda bi, i: (0, 0))
    return pl.pallas_call(
        _merge_kernel,
        grid=(b, l // tm),
        in_specs=[fblk, fblk, fblk, fblk,
                  pl.BlockSpec((None, tm, N_BRANCH * d), lambda bi, i: (bi, i, MAIN_GATE // (N_BRANCH * d))),
                  pl.BlockSpec((N_BRANCH, BRANCH_DIM, d), lambda bi, i: (0, 0, 0)),
                  pl.BlockSpec((d, d), lambda bi, i: (0, 0)),
                  hblk, pl.BlockSpec((None, 8, d), lambda bi, i: (bi, 0, 0)), vec, vec,
                  pl.BlockSpec((N_EXPERTS, d), lambda bi, i: (0, 0))],
        out_specs=[hblk, pl.BlockSpec((None, tm, d // 2), lambda bi, i: (bi, i, 0)),
                   pl.BlockSpec((N_EXPERTS, tm), lambda bi, i: (0, bi * nt + i))],
        out_shape=[jax.ShapeDtypeStruct((b, l, d), F32), jax.ShapeDtypeStruct((b, l, d // 2), jnp.int32),
                   jax.ShapeDtypeStruct((N_EXPERTS, b * l), F32)],
        compiler_params=_cp(("parallel", "parallel")),
        name="merge_branches",
    )(*feats, p_main, w_branch, w_o, h, mod, ln_g.reshape(1, d), ln_b.reshape(1, d), w_router_t)


def _first_argmax(x, axis, size):
    m = jnp.max(x, axis=axis, keepdims=True)
    idx = lax.broadcasted_iota(jnp.int32, x.shape, axis)
    first = jnp.min(jnp.where(x == m, idx, size), axis=axis, keepdims=True)
    return m, idx == first


def _route_kernel(lg_ref, rb_ref, pos_ref, w_ref, cnt_ref, cnt_scr, *, capacity):
    i = pl.program_id(0)

    @pl.when(i == 0)
    def _():
        cnt_scr[...] = jnp.zeros_like(cnt_scr)

    tm = lg_ref.shape[-1]
    per = N_EXPERTS // N_EXPERT_GROUPS
    scores = _sigmoid(lg_ref[...])
    biased = scores + rb_ref[...]
    x3 = biased.reshape(N_EXPERT_GROUPS, per, tm)
    m1, hit = _first_argmax(x3, 1, per)
    m2 = jnp.max(jnp.where(hit, -jnp.inf, x3), axis=1, keepdims=True)
    gscore = (m1 + m2).reshape(N_EXPERT_GROUPS, tm)
    gsel = jnp.zeros(gscore.shape, F32)
    for _ in range(TOP_GROUPS):
        _, hit = _first_argmax(gscore, 0, N_EXPERT_GROUPS)
        gsel = jnp.where(hit, 1.0, gsel)
        gscore = jnp.where(hit, -jnp.inf, gscore)
    masked = jnp.where(gsel.reshape(N_EXPERT_GROUPS, 1, tm) > 0.0, x3, -jnp.inf).reshape(N_EXPERTS, tm)
    sel = jnp.zeros(masked.shape, F32)
    for _ in range(TOP_K):
        _, hit = _first_argmax(masked, 0, N_EXPERTS)
        sel = jnp.where(hit, 1.0, sel)
        masked = jnp.where(hit, -jnp.inf, masked)
    w = sel * scores
    comb = w / jnp.sum(w, axis=0, keepdims=True) * ROUTED_SCALE
    tr = lax.broadcasted_iota(jnp.int32, (tm, tm), 0)
    tc = lax.broadcasted_iota(jnp.int32, (tm, tm), 1)
    rank = cnt_scr[:, 0:1] + _dot(sel, (tr < tc).astype(F32))
    slot = lax.broadcasted_iota(jnp.int32, sel.shape, 0).astype(F32) * float(capacity) + rank
    pos_rows, w_rows = [], []
    remaining = sel
    for _ in range(TOP_K):
        _, hit = _first_argmax(remaining, 0, N_EXPERTS)
        pos_rows.append(jnp.sum(jnp.where(hit, slot, 0.0), axis=0, keepdims=True))
        w_rows.append(jnp.sum(jnp.where(hit, comb, 0.0), axis=0, keepdims=True))
        remaining = jnp.where(hit, 0.0, remaining)
    zero = jnp.zeros((8 - TOP_K, tm), F32)
    pos_ref[...] = jnp.concatenate(pos_rows + [zero], axis=0).astype(jnp.int32)
    w_ref[...] = jnp.concatenate(w_rows + [zero], axis=0)
    cnt_scr[...] = cnt_scr[...] + jnp.sum(sel, axis=1, keepdims=True)

    @pl.when(i == pl.num_programs(0) - 1)
    def _():
        cnt_ref[...] = cnt_scr[...]


def moe_route(logits_t, b_router):
    e, t = logits_t.shape
    tm = 256
    blk = pl.BlockSpec((e, tm), lambda i: (0, i))
    oblk = pl.BlockSpec((8, tm), lambda i: (0, i))
    return pl.pallas_call(
        functools.partial(_route_kernel, capacity=t), grid=(t // tm,),
        in_specs=[blk, pl.BlockSpec((e, tm), lambda i: (0, 0))],
        out_specs=[oblk, oblk, pl.BlockSpec((e, 128), lambda i: (0, 0))],
        out_shape=[jax.ShapeDtypeStruct((8, t), jnp.int32), jax.ShapeDtypeStruct((8, t), F32),
                   jax.ShapeDtypeStruct((e, 128), F32)],
        scratch_shapes=[pltpu.VMEM((e, 128), F32)],
        compiler_params=_cp(("arbitrary",)), name="moe_route",
    )(logits_t, jnp.broadcast_to(b_router.astype(F32)[:, None], (e, tm)))


def _sc_workers():
    info = plsc.get_sparse_core_info()
    return info.num_cores, info.num_subcores


def _sc_chunk(per_worker):
    return max(c for c in range(8, 129, 8) if per_worker % c == 0)


def sc_scatter_rows(rows, pos, n_out):
    t, w = rows.shape
    nc, ns = _sc_workers()
    nw = nc * ns
    per_w = t // nw
    ch = _sc_chunk(per_w)
    nch = per_w // ch
    pos_w = pos[:TOP_K].reshape(TOP_K, nw, nch, ch).transpose(1, 2, 0, 3)
    mesh = plsc.VectorSubcoreMesh(core_axis_name="c", subcore_axis_name="s")

    @functools.partial(
        pl.kernel, mesh=mesh, out_type=jax.ShapeDtypeStruct((n_out, w), jnp.int32),
        scratch_types=[pltpu.VMEM((TOP_K, ch), jnp.int32), pltpu.VMEM((ch, w), jnp.int32), pltpu.SemaphoreType.DMA])
    def scatter(rows_hbm, pos_hbm, out_hbm, idx_v, rows_v, sem):
        wid = lax.axis_index("s") * nc + lax.axis_index("c")

        @pl.loop(0, nch)
        def _(j):
            pltpu.sync_copy(pos_hbm.at[wid, j], idx_v)
            pltpu.sync_copy(rows_hbm.at[pl.ds(wid * per_w + j * ch, ch)], rows_v)
            copies = [pltpu.async_copy(rows_v, out_hbm.at[idx_v.at[k]], sem) for k in range(TOP_K)]
            for cp in copies:
                cp.wait()

    return scatter(rows, pos_w)


def sc_gather_rows(table, idx):
    m = idx.shape[0]
    w = table.shape[1]
    nc, ns = _sc_workers()
    nw = nc * ns
    per_w = m // nw
    ch = _sc_chunk(per_w)
    nch = per_w // ch
    mesh = plsc.VectorSubcoreMesh(core_axis_name="c", subcore_axis_name="s")

    @functools.partial(
        pl.kernel, mesh=mesh, out_type=jax.ShapeDtypeStruct((m, w), jnp.int32),
        scratch_types=[pltpu.VMEM((ch,), jnp.int32), pltpu.VMEM((ch, w), jnp.int32), pltpu.SemaphoreType.DMA])
    def gather(table_hbm, idx_hbm, out_hbm, idx_v, rows_v, sem):
        wid = lax.axis_index("s") * nc + lax.axis_index("c")

        @pl.loop(0, nch)
        def _(j):
            off = wid * per_w + j * ch
            pltpu.sync_copy(idx_hbm.at[pl.ds(off, ch)], idx_v)
            pltpu.async_copy(table_hbm.at[idx_v], rows_v, sem).wait()
            pltpu.sync_copy(rows_v, out_hbm.at[pl.ds(off, ch)])

    return gather(table, idx)


MOE_TM = 512


def _gmm_kernel(te_ref, tj_ref, na_ref, x_ref, wg_ref, wu_ref, wd_ref, y_ref, wg_s, wu_s, wd_s):
    i = pl.program_id(0)

    @pl.when(i < na_ref[0])
    def _():
        @pl.when(tj_ref[i] == 0)
        def _():
            wg_s[...] = wg_ref[...].astype(BF16)
            wu_s[...] = wu_ref[...].astype(BF16)
            wd_s[...] = wd_ref[...].astype(BF16)

        half = D_MODEL // 2
        lo, hi = _unpack_bf16_pair(x_ref[...])
        gate = _dot(lo, wg_s[:half, :]) + _dot(hi, wg_s[half:, :])
        up = _dot(lo, wu_s[:half, :]) + _dot(hi, wu_s[half:, :])
        y_ref[...] = _pack_bf16_pair(_dot(_silu(gate) * up, wd_s[...]))


def moe_grouped_experts(xs, counts, w_gate, w_up, w_down, layer, capacity):
    _, n_e, d, hdim = w_gate.shape
    blocks_per_e = capacity // MOE_TM
    n_tiles = capacity * TOP_K // MOE_TM + n_e
    tiles_e = (counts.astype(jnp.int32) + MOE_TM - 1) // MOE_TM
    ends = jnp.cumsum(tiles_e)
    n_active = ends[-1]
    step = jnp.minimum(jnp.arange(n_tiles, dtype=jnp.int32), n_active - 1)
    owned = step[:, None] >= ends[None, :]
    te = jnp.sum(owned, axis=1).astype(jnp.int32)
    tj = step - jnp.sum(jnp.where(owned, tiles_e[None, :], 0), axis=1).astype(jnp.int32)
    row_blk = lambda i, te_r, tj_r, na_r: (te_r[i] * blocks_per_e + tj_r[i], 0)
    wmap = lambda i, te_r, tj_r, na_r: (layer, te_r[i], 0, 0)
    return pl.pallas_call(
        _gmm_kernel,
        grid_spec=pltpu.PrefetchScalarGridSpec(
            num_scalar_prefetch=3, grid=(n_tiles,),
            in_specs=[pl.BlockSpec((MOE_TM, d // 2), row_blk),
                      pl.BlockSpec((None, None, d, hdim), wmap), pl.BlockSpec((None, None, d, hdim), wmap),
                      pl.BlockSpec((None, None, hdim, d), wmap)],
            out_specs=pl.BlockSpec((MOE_TM, d // 2), row_blk),
            scratch_shapes=[pltpu.VMEM((d, hdim), BF16), pltpu.VMEM((d, hdim), BF16), pltpu.VMEM((hdim, d), BF16)]),
        out_shape=jax.ShapeDtypeStruct(xs.shape, jnp.int32),
        compiler_params=_cp(("arbitrary",)),
        name="moe_grouped_experts",
    )(te, tj, n_active.reshape(1).astype(jnp.int32), xs, w_gate, w_up, w_down)


def _moe_out_kernel(v_ref, g_ref, w_ref, sg_ref, su_ref, sd_ref, h_ref, mod_ref, nmod_ref, lng_ref, lnb_ref,
                    h2_ref, u_ref):
    half = D_MODEL // 2
    lo, hi = _unpack_bf16_pair(v_ref[...])
    gate = _dot(lo, sg_ref[:half, :]) + _dot(hi, sg_ref[half:, :])
    up = _dot(lo, su_ref[:half, :]) + _dot(hi, su_ref[half:, :])
    f = _dot(_silu(gate) * up, sd_ref[...])
    acc_lo = acc_hi = None
    for k in range(TOP_K):
        ylo, yhi = _unpack_bf16_pair(g_ref[k])
        wk = w_ref[:, k:k + 1]
        acc_lo = ylo * wk if acc_lo is None else acc_lo + ylo * wk
        acc_hi = yhi * wk if acc_hi is None else acc_hi + yhi * wk
    f = f + jnp.concatenate([acc_lo, acc_hi], axis=1)
    h2 = _ln(DEEPNORM_ALPHA * h_ref[...] + mod_ref[5:6, :] * f) * lng_ref[...] + lnb_ref[...]
    h2_ref[...] = h2
    u_ref[...] = (_ln(h2) * (1.0 + nmod_ref[1:2, :]) + nmod_ref[0:1, :]).astype(u_ref.dtype)


def moe_output(vp, gathered, w_tok, tok0, ws_gate, ws_up, ws_down, h1, mod, next_mod, ln_g, ln_b):
    b, l, d = h1.shape
    tm = 256
    nt = l // tm
    blk0 = tok0 // tm
    xblk = pl.BlockSpec((None, tm, d), lambda bi, i: (bi, i, 0))
    pblk = pl.BlockSpec((None, tm, d // 2), lambda bi, i: (bi, i, 0))
    vec = pl.BlockSpec((1, d), lambda bi, i: (0, 0))
    modblk = pl.BlockSpec((None, 8, d), lambda bi, i: (bi, 0, 0))
    return pl.pallas_call(
        _moe_out_kernel,
        grid=(b, nt),
        in_specs=[pblk, pl.BlockSpec((TOP_K, tm, d // 2), lambda bi, i: (0, blk0 + bi * nt + i, 0)),
                  pl.BlockSpec((tm, 8), lambda bi, i: (blk0 + bi * nt + i, 0)),
                  pl.BlockSpec((d, EXPERT_DIM), lambda bi, i: (0, 0)),
                  pl.BlockSpec((d, EXPERT_DIM), lambda bi, i: (0, 0)),
                  pl.BlockSpec((EXPERT_DIM, d), lambda bi, i: (0, 0)),
                  xblk, modblk, modblk, vec, vec],
        out_specs=[xblk, xblk],
        out_shape=[jax.ShapeDtypeStruct((b, l, d), F32), jax.ShapeDtypeStruct((b, l, d), BF16)],
        compiler_params=_cp(("parallel", "parallel")),
        name="moe_output",
    )(vp, gathered, w_tok, ws_gate, ws_up, ws_down, h1, mod, next_mod, ln_g.reshape(1, d), ln_b.reshape(1, d))


def moe_layer(streams, layer, b_router, w_gate, w_up, w_down, ws_gate, ws_up, ws_down, ln_g, ln_b):
    half = D_MODEL // 2
    vps = [s[0].reshape(-1, half) for s in streams]
    sizes = [v.shape[0] for v in vps]
    t = sum(sizes)
    vp_all = vps[0] if len(vps) == 1 else jnp.concatenate(vps, axis=0)
    lg_all = streams[0][1] if len(vps) == 1 else jnp.concatenate([s[1] for s in streams], axis=1)
    pos, w_rows, counts = moe_route(lg_all, b_router)
    xs = sc_scatter_rows(vp_all, pos, N_EXPERTS * t)
    ys = moe_grouped_experts(xs, counts[:, 0], w_gate, w_up, w_down, layer, t)
    gathered = sc_gather_rows(ys, pos[:TOP_K].reshape(-1)).reshape(TOP_K, t, half)
    w_tok = w_rows.T
    outs, tok0 = [], 0
    for (vp, _, h1, mod, next_mod), n in zip(streams, sizes):
        outs.append(moe_output(vp, gathered, w_tok, tok0, ws_gate, ws_up, ws_down, h1, mod, next_mod, ln_g, ln_b))
        tok0 += n
    return outs


def _layer_weights(l, w_in, b_in, dn_a_log, dn_dt_bias, gla_w2, gla_b2):
    w_main, w_small = w_in[0][l], w_in[1][l]
    bvec = b_in[l]
    cols = lambda a, n: bvec[a:a + n]
    b_main = jnp.concatenate([cols(_GATE0, 4096), cols(_DN0, 2048), cols(_CONV0, 1024), cols(_FN0, 512),
                              cols(_GLA0, 1536)])
    b_small = jnp.pad(jnp.concatenate([cols(2048, 16), cols(_GLA0 + 1536, 32)]), (0, 128 - 48))
    par = jnp.zeros((8, 128), F32)
    par = par.at[0, 8:16].set(dn_a_log[l].reshape(-1)).at[1, 8:16].set(dn_dt_bias[l].reshape(-1))
    w2p = jnp.zeros((2, 128, GLA_KDIM), F32)
    w2p = w2p.at[0, 16:32].set(gla_w2[l, 0]).at[1, 32:48].set(gla_w2[l, 1])
    return dict(w_main=w_main, b_main=b_main, w_small=w_small, b_small=b_small, par=par, w2p=w2p,
                b2=gla_b2[l].reshape(2, 1, GLA_KDIM).astype(F32))


def _project(u, lw, with_gla):
    b, l, d = u.shape
    flat = u.reshape(b * l, d)
    n = MAIN_COLS if with_gla else MAIN_GLA
    p_main = matmul_bias(flat, lw["w_main"], lw["b_main"], BF16, 1536, n).reshape(b, l, n)
    small = matmul_bias(flat, lw["w_small"], lw["b_small"], F32, 128).reshape(b, l, 128)
    return p_main, small


def kernel(x, c, ctx, c_ctx, w_mod, b_mod, w_in, b_in, dn_conv_w, dn_a_log, dn_dt_bias, dn_norm_g, gla_w2, gla_b2, gla_norm_g, conv_w, conv_b, conv_ln_g, conv_ln_b, w_branch, w_o, ln_g, ln_b, w_router, b_router, w_gate, w_up, w_down, ws_gate, ws_up, ws_down):
    batch, seq, d = x.shape
    c8 = jnp.zeros((8, d), F32).at[:batch].set(c).at[batch].set(c_ctx)
    mods = adaln_vectors(c8, w_mod, b_mod).reshape(DEPTH, 8, 6, d)
    zrow = jnp.zeros((DEPTH, 2, d), F32)
    mod_x = [jnp.concatenate([mods[l, :batch], jnp.broadcast_to(zrow[l][None], (batch, 2, d))], axis=1)
             for l in range(DEPTH)]
    mod_c = [jnp.broadcast_to(jnp.concatenate([mods[l, batch], zrow[l]], axis=0)[None], (batch, 8, d))
             for l in range(DEPTH)]

    w_packed = repack_projection_weights(w_in)
    h, hc = x, ctx
    u_x, u_c = ln_modulate(h, mod_x[0]), ln_modulate(hc, mod_c[0])
    for l in range(DEPTH):
        lw = _layer_weights(l, w_packed, b_in, dn_a_log, dn_dt_bias, gla_w2, gla_b2)
        px = _project(u_x, lw, False)
        pc = _project(u_c, lw, True)
        pg_x, lr_x = project_colmajor(u_x, lw["w_main"], lw["b_main"], MAIN_GLA, MAIN_COLS - MAIN_GLA,
                                      lw["w_small"], lw["b_small"])
        qkv_c, qkv_x = deltanet_shortconv(pc[0], dn_conv_w[l]), deltanet_shortconv(px[0], dn_conv_w[l])
        s0 = jnp.zeros((batch, 2, DN_HEADS, DN_HEAD_DIM, DN_HEAD_DIM), F32)
        ocf, ocb, s_c = deltanet_scan(qkv_c, pc[1], lw["par"], s0)
        oxf, oxb, _ = deltanet_scan(qkv_x, px[1], lw["par"], s_c)
        dn_x = gated_head_norm(oxf, oxb, px[0], MAIN_DN // 512 + 3, dn_norm_g[l])
        dn_c = gated_head_norm(ocf, ocb, pc[0], MAIN_DN // 512 + 3, dn_norm_g[l])
        g0 = jnp.zeros((batch, 2, 2, 2 * GLA_DV, 2 * GLA_DK), F32)
        gcf, gcb, gs_c = gla_scan(pc[0], MAIN_GLA // 512, pc[1], lw["w2p"], lw["b2"], g0)
        gxf, gxb, _ = gla_scan(pg_x, 0, lr_x, lw["w2p"], lw["b2"], gs_c)
        gla_x = gated_head_norm_colmajor(gxf, gxb, pg_x, 2, gla_norm_g[l])
        gla_c = gated_head_norm(gcf, gcb, pc[0], MAIN_GLA // 512 + 2, gla_norm_g[l])
        wb = w_branch[l].astype(BF16)
        wo = w_o[l].astype(BF16)
        wrt = w_router[l].T
        last = l == DEPTH - 1
        nxt = min(l + 1, DEPTH - 1)
        streams = []
        for (p, dn_f, gla_f, hh, mod, nmod) in ((px, dn_x, gla_x, h, mod_x[l], mod_x[nxt]),
                                               (pc, dn_c, gla_c, hc, mod_c[l], mod_c[nxt]))[:1 if last else 2]:
            feats = (conformer_conv(p[0], conv_w[l], conv_b[l], conv_ln_g[l], conv_ln_b[l]), dn_f,
                     fourier_mix(p[0]), gla_f)
            h1, vp, lg = merge_branches(feats, p[0], wb, wo, hh, mod, ln_g[l, 0], ln_b[l, 0], wrt)
            streams.append((vp, lg, h1, mod, nmod))
        outs = moe_layer(streams, l, b_router[l], w_gate, w_up, w_down, ws_gate[l], ws_up[l], ws_down[l],
                         ln_g[l, 1], ln_b[l, 1])
        (h, u_x) = outs[0]
        if not last:
            (hc, u_c) = outs[1]
    return h
```
